```python
import math
import jax, jax.numpy as jnp
from jax import lax
import numpy as np

D_MODEL = 1024
BATCH = 4
SEQ = 4096
DEPTH = 1
DEC_BATCH = 32
DEC_SEQ = 4
PAST_LEN = 16384
PAGE_SIZE = 128

N_HEADS = 8
N_KV_HEADS = 4
HEAD_DIM = D_MODEL // N_HEADS
GROUP = N_HEADS // N_KV_HEADS
MOBA_BLOCK = 256
MOBA_TOPK = 3
Q_CHUNK = 16
ALIBI_SLOPES = tuple(2.0 ** (-8.0 * (h + 1) / N_HEADS) for h in range(N_HEADS))
D_RNN = D_MODEL
N_RNN_BLOCKS = 8
RNN_BLOCK = D_RNN // N_RNN_BLOCKS
CONV_WIDTH = 4
RG_C = 8.0
D_FF = 2816
N_SUBLAYERS = 3
EPS = 1e-6
IN_SPLITS = (D_RNN, D_RNN, N_HEADS * HEAD_DIM, N_KV_HEADS * HEAD_DIM, N_KV_HEADS * HEAD_DIM, D_MODEL, D_MODEL)
D_IN = sum(IN_SPLITS)

kernel_name = "hawk_moba_macaron_adaln_step"


def rmsnorm(x, g):
    xf = x.astype(jnp.float32)
    y = xf * lax.rsqrt(jnp.mean(xf * xf, axis=-1, keepdims=True) + EPS)
    return (y * g.astype(jnp.float32)).astype(x.dtype)


def modulate(n, shift, scale):
    return n * (1.0 + scale) + shift


def swiglu(h, w_gate, w_up, w_down):
    return (jax.nn.silu(h @ w_gate) * (h @ w_up)) @ w_down


def causal_conv(x, buf, w, b):
    s = x.shape[1]
    xp = jnp.concatenate([buf.astype(x.dtype), x], axis=1)
    y = b + w[0] * xp[:, 0:s]
    for k in range(1, CONV_WIDTH):
        y = y + w[k] * xp[:, k:k + s]
    return y, xp[:, -(CONV_WIDTH - 1):]


def rglru(xc, h0, w_ra, b_ra, w_rx, b_rx, lam):
    B, S, W = xc.shape
    xb = xc.reshape(B, S, N_RNN_BLOCKS, RNN_BLOCK)
    r = jax.nn.sigmoid(jnp.einsum('bsni,nij->bsnj', xb, w_ra).reshape(B, S, W) + b_ra)
    i = jax.nn.sigmoid(jnp.einsum('bsni,nij->bsnj', xb, w_rx).reshape(B, S, W) + b_rx)
    log_a = -RG_C * r.astype(jnp.float32) * jax.nn.softplus(-lam.astype(jnp.float32))
    a = jnp.exp(log_a)
    mult = jnp.sqrt(-jnp.expm1(2.0 * log_a))
    bx = mult * (i * xc).astype(jnp.float32)
    bx = bx.at[:, 0].add(a[:, 0] * h0.astype(jnp.float32))

    def combine(lhs, rhs):
        a1, b1 = lhs
        a2, b2 = rhs
        return a1 * a2, a2 * b1 + b2

    _, h = lax.associative_scan(combine, (a, bx), axis=1)
    return h.astype(xc.dtype), h[:, -1]


def moba_attention(q, k_all, v_all, q_pos, qc):
    B, Sq, H, dh = q.shape
    T = k_all.shape[1]
    nb = -(-T // MOBA_BLOCK)
    pad = nb * MOBA_BLOCK - T
    zpad = jnp.zeros((B, pad, N_KV_HEADS, HEAD_DIM), k_all.dtype)
    kb = jnp.concatenate([k_all, zpad], axis=1).reshape(B, nb, MOBA_BLOCK, N_KV_HEADS, HEAD_DIM)
    vb = jnp.concatenate([v_all, zpad.astype(v_all.dtype)], axis=1).reshape(B, nb, MOBA_BLOCK, N_KV_HEADS, HEAD_DIM)
    k_mean = jnp.mean(kb, axis=2, dtype=jnp.float32)
    n_sel = min(MOBA_TOPK, nb - 1)
    slopes = jnp.asarray(ALIBI_SLOPES, jnp.float32)[None, None, :, None, None]
    offs = jnp.arange(MOBA_BLOCK, dtype=jnp.int32)
    b_ix = jnp.arange(B)[:, None, None, None]
    kv_ix = (jnp.arange(H) // GROUP)[None, None, :, None]
    blk_ids = jnp.arange(nb)
    scale = HEAD_DIM ** -0.5

    def attend_chunk(args):
        qch, pos = args
        qf = qch.astype(jnp.float32)
        own = pos // MOBA_BLOCK
        own_b = jnp.broadcast_to(own[None, :, None, None], (B, qc, H, 1))
        if n_sel > 0:
            gs = jnp.einsum('bqgjd,bngd->bqgjn', qf.reshape(B, qc, N_KV_HEADS, GROUP, HEAD_DIM), k_mean)
            gs = gs.reshape(B, qc, H, nb)
            fully_past = (blk_ids[None, :] < own[:, None])[None, :, None, :]
            gs = jnp.where(fully_past, gs, -jnp.inf)
            _, top = lax.top_k(gs, n_sel)
            idx = jnp.concatenate([top, own_b], axis=-1)
            slot_ok = jnp.concatenate([top < own_b, jnp.ones_like(own_b, dtype=bool)], axis=-1)
        else:
            idx = own_b
            slot_ok = jnp.ones_like(own_b, dtype=bool)
        kg = kb[b_ix, idx, :, kv_ix]
        vg = vb[b_ix, idx, :, kv_ix]
        key_pos = idx[..., None] * MOBA_BLOCK + offs
        qp = pos[None, :, None, None, None]
        s = jnp.einsum('bqhd,bqhnkd->bqhnk', qf, kg.astype(jnp.float32)) * scale
        s = s - slopes * (qp - key_pos).astype(jnp.float32)
        s = jnp.where(slot_ok[..., None] & (key_pos <= qp), s, -jnp.inf)
        pr = jax.nn.softmax(s.reshape(B, qc, H, -1), axis=-1).reshape(s.shape)
        o = jnp.einsum('bqhnk,bqhnkd->bqhd', pr, vg.astype(jnp.float32))
        return o.astype(q.dtype)

    nc = Sq // qc
    q_ch = q.reshape(B, nc, qc, H, dh).transpose(1, 0, 2, 3, 4)
    out = lax.map(attend_chunk, (q_ch, q_pos.reshape(nc, qc)))
    return out.transpose(1, 0, 2, 3, 4).reshape(B, Sq, H, dh)


def layer(x, c, conv_buf, h0, k_past, v_past, qc, p):
    B, S, _ = x.shape
    start = k_past.shape[1]
    mod = jax.nn.silu(c) @ p['w_ada'] + p['b_ada']
    sh1, sc1, g1, sh2, sc2, g2, sh3, sc3, g3 = jnp.split(mod[:, None, :], 3 * N_SUBLAYERS, axis=-1)
    h = modulate(rmsnorm(x, p['norm1_g']), sh1, sc1)
    x = x + 0.5 * g1 * swiglu(h, p['ffn1_gate'], p['ffn1_up'], p['ffn1_down'])
    h = modulate(rmsnorm(x, p['norm2_g']), sh2, sc2)
    proj = h @ p['w_in']
    split_at = [int(v) for v in np.cumsum(IN_SPLITS)[:-1]]
    x_rnn, x_gate, q, k, v, gate_a, gate_b = jnp.split(proj, split_at, axis=-1)
    xc, new_conv = causal_conv(x_rnn, conv_buf, p['conv_w'], p['conv_b'])
    hr, h_last = rglru(xc, h0, p['rg_wa'], p['rg_ba'], p['rg_wx'], p['rg_bx'], p['rg_lambda'])
    y_a = (hr * jax.nn.gelu(x_gate)) @ p['w_branch_a']
    q = rmsnorm(q.reshape(B, S, N_HEADS, HEAD_DIM), p['q_norm_g'])
    k = rmsnorm(k.reshape(B, S, N_KV_HEADS, HEAD_DIM), p['k_norm_g'])
    v = v.reshape(B, S, N_KV_HEADS, HEAD_DIM)
    k_all = jnp.concatenate([k_past.astype(k.dtype), k], axis=1)
    v_all = jnp.concatenate([v_past.astype(v.dtype), v], axis=1)
    q_pos = start + jnp.arange(S, dtype=jnp.int32)
    o = moba_attention(q, k_all, v_all, q_pos, qc)
    y_b = o.reshape(B, S, N_HEADS * HEAD_DIM) @ p['w_branch_b']
    merged = jax.nn.sigmoid(gate_a) * y_a + jax.nn.sigmoid(gate_b) * y_b
    x = x + g2 * (merged @ p['w_out'])
    h = modulate(rmsnorm(x, p['norm3_g']), sh3, sc3)
    x = x + 0.5 * g3 * swiglu(h, p['ffn2_gate'], p['ffn2_up'], p['ffn2_down'])
    return x, k, v, new_conv, h_last


def setup_inputs(seed: int = 0) -> dict:
    key = jax.random.key(seed)
    keys = iter(jax.random.split(key, 48))
    f32 = jnp.float32

    def nrm(shape, scale):
        return jax.random.normal(next(keys), shape, f32) * scale

    n_pages = PAST_LEN // PAGE_SIZE
    n_pool = (DEC_BATCH * n_pages * 5) // 4
    x_prompt = nrm((BATCH, SEQ, D_MODEL), 1.0)
    x_sample = nrm((DEC_BATCH, DEC_SEQ, D_MODEL), 1.0)
    c_prompt = nrm((BATCH, D_MODEL), 1.0)
    c_sample = nrm((DEC_BATCH, D_MODEL), 1.0)
    cache_k = nrm((DEPTH, n_pool, PAGE_SIZE, N_KV_HEADS, HEAD_DIM), 1.0)
    cache_v = nrm((DEPTH, n_pool, PAGE_SIZE, N_KV_HEADS, HEAD_DIM), 1.0)
    state_conv = nrm((DEPTH, DEC_BATCH, CONV_WIDTH - 1, D_RNN), 1.0)
    state_rglru = nrm((DEPTH, DEC_BATCH, D_RNN), 0.5)
    page_table = jax.random.permutation(next(keys), n_pool)[:DEC_BATCH * n_pages].reshape(DEC_BATCH, n_pages).astype(jnp.int32)
    u = jax.random.uniform(next(keys), (DEPTH, D_RNN), f32, 0.9, 0.999)
    rg_lambda = jnp.log(u) - jnp.log1p(-u)
    return {
        'x_prompt': x_prompt, 'x_sample': x_sample, 'c_prompt': c_prompt, 'c_sample': c_sample,
        'cache_k': cache_k, 'cache_v': cache_v, 'state_conv': state_conv, 'state_rglru': state_rglru,
        'page_table': page_table,
        'w_ada': nrm((DEPTH, D_MODEL, 3 * N_SUBLAYERS * D_MODEL), 0.5 * D_MODEL ** -0.5),
        'b_ada': nrm((DEPTH, 3 * N_SUBLAYERS * D_MODEL), 0.01),
        'norm1_g': 1.0 + nrm((DEPTH, D_MODEL), 0.02),
        'ffn1_gate': nrm((DEPTH, D_MODEL, D_FF), D_MODEL ** -0.5),
        'ffn1_up': nrm((DEPTH, D_MODEL, D_FF), D_MODEL ** -0.5),
        'ffn1_down': nrm((DEPTH, D_FF, D_MODEL), D_FF ** -0.5),
        'norm2_g': 1.0 + nrm((DEPTH, D_MODEL), 0.02),
        'w_in': nrm((DEPTH, D_MODEL, D_IN), D_MODEL ** -0.5),
        'conv_w': nrm((DEPTH, CONV_WIDTH, D_RNN), CONV_WIDTH ** -0.5),
        'conv_b': nrm((DEPTH, D_RNN), 0.01),
        'rg_wa': nrm((DEPTH, N_RNN_BLOCKS, RNN_BLOCK, RNN_BLOCK), RNN_BLOCK ** -0.5),
        'rg_ba': nrm((DEPTH, D_RNN), 0.01),
        'rg_wx': nrm((DEPTH, N_RNN_BLOCKS, RNN_BLOCK, RNN_BLOCK), RNN_BLOCK ** -0.5),
        'rg_bx': nrm((DEPTH, D_RNN), 0.01),
        'rg_lambda': rg_lambda,
        'q_norm_g': 1.0 + nrm((DEPTH, HEAD_DIM), 0.02),
        'k_norm_g': 1.0 + nrm((DEPTH, HEAD_DIM), 0.02),
        'w_branch_a': nrm((DEPTH, D_RNN, D_MODEL), D_RNN ** -0.5),
        'w_branch_b': nrm((DEPTH, N_HEADS * HEAD_DIM, D_MODEL), (N_HEADS * HEAD_DIM) ** -0.5),
        'w_out': nrm((DEPTH, D_MODEL, D_MODEL), D_MODEL ** -0.5),
        'norm3_g': 1.0 + nrm((DEPTH, D_MODEL), 0.02),
        'ffn2_gate': nrm((DEPTH, D_MODEL, D_FF), D_MODEL ** -0.5),
        'ffn2_up': nrm((DEPTH, D_MODEL, D_FF), D_MODEL ** -0.5),
        'ffn2_down': nrm((DEPTH, D_FF, D_MODEL), D_FF ** -0.5),
    }


def reference(x_prompt, x_sample, c_prompt, c_sample, cache_k, cache_v, state_conv, state_rglru, page_table,
              w_ada, b_ada, norm1_g, ffn1_gate, ffn1_up, ffn1_down, norm2_g, w_in, conv_w, conv_b,
              rg_wa, rg_ba, rg_wx, rg_bx, rg_lambda, q_norm_g, k_norm_g, w_branch_a, w_branch_b, w_out,
              norm3_g, ffn2_gate, ffn2_up, ffn2_down):
    bp, sp, _ = x_prompt.shape
    bs = x_sample.shape[0]
    n_pages = page_table.shape[1]
    past = n_pages * PAGE_SIZE
    qc_prompt = math.gcd(Q_CHUNK, sp)
    yp, ys = x_prompt, x_sample
    kp_l, vp_l, cp_l, hp_l, ks_l, vs_l, cs_l, hs_l = [], [], [], [], [], [], [], []
    for l in range(DEPTH):
        p = dict(w_ada=w_ada[l], b_ada=b_ada[l], norm1_g=norm1_g[l], ffn1_gate=ffn1_gate[l], ffn1_up=ffn1_up[l],
                 ffn1_down=ffn1_down[l], norm2_g=norm2_g[l], w_in=w_in[l], conv_w=conv_w[l], conv_b=conv_b[l],
                 rg_wa=rg_wa[l], rg_ba=rg_ba[l], rg_wx=rg_wx[l], rg_bx=rg_bx[l], rg_lambda=rg_lambda[l],
                 q_norm_g=q_norm_g[l], k_norm_g=k_norm_g[l], w_branch_a=w_branch_a[l], w_branch_b=w_branch_b[l],
                 w_out=w_out[l], norm3_g=norm3_g[l], ffn2_gate=ffn2_gate[l], ffn2_up=ffn2_up[l],
                 ffn2_down=ffn2_down[l])
        zero_conv = jnp.zeros((bp, CONV_WIDTH - 1, D_RNN), x_prompt.dtype)
        zero_h = jnp.zeros((bp, D_RNN), jnp.float32)
        empty_kv = jnp.zeros((bp, 0, N_KV_HEADS, HEAD_DIM), x_prompt.dtype)
        yp, kp, vp, cp, hp = layer(yp, c_prompt, zero_conv, zero_h, empty_kv, empty_kv, qc_prompt, p)
        k_past = cache_k[l][page_table].reshape(bs, past, N_KV_HEADS, HEAD_DIM)
        v_past = cache_v[l][page_table].reshape(bs, past, N_KV_HEADS, HEAD_DIM)
        ys, ks, vs, cs, hs = layer(ys, c_sample, state_conv[l], state_rglru[l], k_past, v_past, 1, p)
        kp_l.append(kp); vp_l.append(vp); cp_l.append(cp); hp_l.append(hp)
        ks_l.append(ks); vs_l.append(vs); cs_l.append(cs); hs_l.append(hs)
    return (yp, ys, jnp.stack(kp_l), jnp.stack(vp_l), jnp.stack(cp_l), jnp.stack(hp_l),
            jnp.stack(ks_l), jnp.stack(vs_l), jnp.stack(cs_l), jnp.stack(hs_l))
```

```python
import functools

import jax
import jax.numpy as jnp
from jax import lax
from jax.experimental import pallas as pl
from jax.experimental.pallas import tpu as pltpu

F32 = jnp.float32
BF16 = jnp.bfloat16

N_HEADS = 8
N_KV_HEADS = 4
HEAD_DIM = 128
GROUP = N_HEADS // N_KV_HEADS
MOBA_BLOCK = 256
MOBA_TOPK = 3
PAGE_SIZE = 128
N_RNN_BLOCKS = 8
RNN_BLOCK = 128
CONV_WIDTH = 4
RG_C = 8.0
N_SUBLAYERS = 3
EPS = 1e-6
ALIBI_SLOPES = tuple(2.0 ** (-8.0 * (h + 1) / N_HEADS) for h in range(N_HEADS))
ATTN_SCALE = HEAD_DIM ** -0.5
NEG = -1e30
SUBLANES = 8
VMEM_LIMIT = 56 * 1024 * 1024
PAGES_PER_STEP = 8


def _params(n_axes):
    return pltpu.CompilerParams(dimension_semantics=("arbitrary",) * n_axes, vmem_limit_bytes=VMEM_LIMIT)


def _resident(shape):
    nd = len(shape)
    return pl.BlockSpec(shape, lambda *_: (0,) * nd, pipeline_mode=pl.Buffered(1))


def _dot(a, b):
    return jnp.dot(a, b, preferred_element_type=F32)


def _dot_nt(a, b, precision=None):
    return lax.dot_general(a, b, (((1,), (1,)), ((), ())), precision=precision, preferred_element_type=F32)


def _rms(x, g):
    return x * lax.rsqrt(jnp.mean(x * x, axis=-1, keepdims=True) + EPS) * g


def _silu(x):
    return x * jax.nn.sigmoid(x)


def _ada_kernel(c_ref, w_ref, b_ref, o_ref):
    a = _silu(c_ref[...]).astype(BF16)
    o_ref[...] = _dot(a, w_ref[...].astype(BF16)) + b_ref[...]


def _ada(c, w, b):
    m, d = c.shape
    n = w.shape[1]
    tn = n // 8
    return pl.pallas_call(
        _ada_kernel,
        grid=(n // tn,),
        in_specs=[pl.BlockSpec((m, d), lambda i: (0, 0)),
                  pl.BlockSpec((d, tn), lambda i: (0, i)),
                  pl.BlockSpec((1, tn), lambda i: (0, i))],
        out_specs=pl.BlockSpec((m, tn), lambda i: (0, i)),
        out_shape=jax.ShapeDtypeStruct((m, n), F32),
        compiler_params=_params(1),
        name="ada",
    )(c, w, b.reshape(1, n))


def _swiglu_acc(h, wg_ref, wu_ref, wd_ref, chunk):
    d_ff = wg_ref.shape[1]
    acc = None
    for c in range(d_ff // chunk):
        sl = slice(c * chunk, (c + 1) * chunk)
        act = (_silu(_dot(h, wg_ref[:, sl])) * _dot(h, wu_ref[:, sl])).astype(BF16)
        part = _dot(act, wd_ref[sl, :])
        acc = part if acc is None else acc + part
    return acc


def _ffn_kernel(x_ref, sh_ref, sc_ref, g_ref, ng_ref, wg_ref, wu_ref, wd_ref, o_ref, *, chunk):
    x = x_ref[...]
    h = (_rms(x, ng_ref[...]) * (1.0 + sc_ref[...]) + sh_ref[...]).astype(BF16)
    o_ref[...] = x + 0.5 * g_ref[...] * _swiglu_acc(h, wg_ref, wu_ref, wd_ref, chunk)


def _mod_spec(mod, tm, tiles_per_group):
    r, d = mod.shape[1], mod.shape[2]
    return pl.BlockSpec((None, r, d), lambda i: (i // tiles_per_group, 0, 0))


def _ffn_chunk(d_ff):
    return 256 if d_ff % 256 == 0 else 128


def _ffn(x, sh, sc, g, ng, wg, wu, wd, tm, tiles_per_group):
    t, d = x.shape
    d_ff = wg.shape[1]
    row = pl.BlockSpec((tm, d), lambda i: (i, 0))
    return pl.pallas_call(
        functools.partial(_ffn_kernel, chunk=_ffn_chunk(d_ff)),
        grid=(t // tm,),
        in_specs=[row, _mod_spec(sh, tm, tiles_per_group), _mod_spec(sc, tm, tiles_per_group),
                  _mod_spec(g, tm, tiles_per_group), _resident((1, d)),
                  _resident(wg.shape), _resident(wu.shape), _resident(wd.shape)],
        out_specs=row,
        out_shape=jax.ShapeDtypeStruct((t, d), F32),
        compiler_params=_params(1),
        name="ffn",
    )(x, sh, sc, g, ng, wg, wu, wd)


def _inproj_kernel(x_ref, sh_ref, sc_ref, ng_ref, w_ref, qg_ref, kg_ref,
                   xr_ref, ug_ref, q_ref, k_ref, v_ref, sga_ref, sgb_ref):
    d_rnn = xr_ref.shape[1]
    dq = q_ref.shape[1]
    dk = k_ref.shape[1]
    x = x_ref[...]
    h = (_rms(x, ng_ref[...]) * (1.0 + sc_ref[...]) + sh_ref[...]).astype(BF16)
    o = 0
    xr_ref[...] = _dot(h, w_ref[:, o:o + d_rnn])
    o += d_rnn
    ug_ref[...] = jax.nn.gelu(_dot(h, w_ref[:, o:o + d_rnn]))
    o += d_rnn
    for hd in range(dq // HEAD_DIM):
        sl = slice(hd * HEAD_DIM, (hd + 1) * HEAD_DIM)
        q_ref[:, sl] = _rms(_dot(h, w_ref[:, o + hd * HEAD_DIM:o + (hd + 1) * HEAD_DIM]), qg_ref[...])
    o += dq
    for hd in range(dk // HEAD_DIM):
        sl = slice(hd * HEAD_DIM, (hd + 1) * HEAD_DIM)
        k_ref[:, sl] = _rms(_dot(h, w_ref[:, o + hd * HEAD_DIM:o + (hd + 1) * HEAD_DIM]), kg_ref[...])
    o += dk
    v_ref[...] = _dot(h, w_ref[:, o:o + dk])
    o += dk
    d = sga_ref.shape[1]
    sga_ref[...] = jax.nn.sigmoid(_dot(h, w_ref[:, o:o + d]))
    o += d
    sgb_ref[...] = jax.nn.sigmoid(_dot(h, w_ref[:, o:o + d]))


def _inproj(x, sh, sc, ng, w_in, qg, kg, tm, tiles_per_group):
    t, d = x.shape
    dq, dk = N_HEADS * HEAD_DIM, N_KV_HEADS * HEAD_DIM
    d_rnn = (w_in.shape[1] - dq - 2 * dk - 2 * d) // 2

    def row(w):
        return pl.BlockSpec((tm, w), lambda i: (i, 0))

    widths = (d_rnn, d_rnn, dq, dk, dk, d, d)
    return pl.pallas_call(
        _inproj_kernel,
        grid=(t // tm,),
        in_specs=[row(d), _mod_spec(sh, tm, tiles_per_group), _mod_spec(sc, tm, tiles_per_group),
                  _resident((1, d)), _resident(w_in.shape), _resident((1, HEAD_DIM)), _resident((1, HEAD_DIM))],
        out_specs=[row(w) for w in widths],
        out_shape=[jax.ShapeDtypeStruct((t, w), F32) for w in widths],
        compiler_params=_params(1),
        name="inproj",
    )(x, sh, sc, ng, w_in, qg, kg)


def _rg_gates(xc, wg_ref, bra_ref, brx_ref, lam_ref, a_ref, b_ref):
    sp = jax.nn.softplus(-lam_ref[...])
    for n in range(N_RNN_BLOCKS):
        sl = slice(n * RNN_BLOCK, (n + 1) * RNN_BLOCK)
        xcn = xc[:, sl]
        z = _dot(xcn.astype(BF16), wg_ref[n])
        r = jax.nn.sigmoid(z[:, :RNN_BLOCK] + bra_ref[:, sl])
        i = jax.nn.sigmoid(z[:, RNN_BLOCK:] + brx_ref[:, sl])
        log_a = -RG_C * r * sp[:, sl]
        a = jnp.exp(log_a)
        a_ref[:, sl] = a
        b_ref[:, sl] = jnp.sqrt(1.0 - a * a) * (i * xcn)


def _scan_rows(a, b, row_in_seg, steps):
    for d in steps:
        keep = row_in_seg >= d
        a_prev = jnp.where(keep, pltpu.roll(a, d, 0), 1.0)
        b_prev = jnp.where(keep, pltpu.roll(b, d, 0), 0.0)
        b = a * b_prev + b
        a = a * a_prev
    return a, b


def _rnn_prompt_kernel(xr_ref, ug_ref, sga_ref, conv0_ref, h0_ref, cw_ref, cb_ref, wg_ref, bra_ref, brx_ref,
                       lam_ref, wba_ref, ya_ref, nconv_ref, hlast_ref, xbuf, a_s, b_s, hc_s, *, ts):
    s = pl.program_id(1)
    w = xr_ref.shape[1]
    tail = CONV_WIDTH - 1
    lo = SUBLANES - tail

    @pl.when(s == 0)
    def _():
        xbuf[lo:SUBLANES, :] = conv0_ref[...]
        hc_s[...] = jnp.broadcast_to(h0_ref[...], hc_s.shape)

    xbuf[SUBLANES:SUBLANES + ts, :] = xr_ref[...]
    xc = cb_ref[...] + cw_ref[0:1, :] * xbuf[lo:lo + ts, :]
    for k in range(1, CONV_WIDTH):
        xc = xc + cw_ref[k:k + 1, :] * xbuf[lo + k:lo + k + ts, :]
    new_tail = xbuf[ts + lo:ts + SUBLANES, :]
    nconv_ref[...] = new_tail
    xbuf[lo:SUBLANES, :] = new_tail
    _rg_gates(xc, wg_ref, bra_ref, brx_ref, lam_ref, a_s, b_s)

    row = lax.broadcasted_iota(jnp.int32, (SUBLANES, w), 0)

    def body(c, hc):
        off = pl.multiple_of(c * SUBLANES, SUBLANES)
        a_cum, b_cum = _scan_rows(a_s[pl.ds(off, SUBLANES), :], b_s[pl.ds(off, SUBLANES), :], row, (1, 2, 4))
        h = a_cum * hc + b_cum
        a_s[pl.ds(off, SUBLANES), :] = h
        return jnp.broadcast_to(h[SUBLANES - 1:SUBLANES, :], (SUBLANES, w))

    hc = lax.fori_loop(0, ts // SUBLANES, body, hc_s[...])
    hc_s[...] = hc
    hlast_ref[...] = hc[0:1, :]
    u = (a_s[...] * ug_ref[...]).astype(BF16)
    ya_ref[...] = sga_ref[...] * _dot(u, wba_ref[...])


def _rnn_prompt(xr, ug, sga, conv0, h0, cw, cb, wg, bra, brx, lam, wba, ts):
    b, s, w = xr.shape
    d = wba.shape[1]
    tail = CONV_WIDTH - 1
    seq = pl.BlockSpec((None, ts, w), lambda i, j: (i, j, 0))
    return pl.pallas_call(
        functools.partial(_rnn_prompt_kernel, ts=ts),
        grid=(b, s // ts),
        in_specs=[seq, seq, pl.BlockSpec((None, ts, d), lambda i, j: (i, j, 0)),
                  pl.BlockSpec((None, tail, w), lambda i, j: (i, 0, 0)),
                  pl.BlockSpec((None, 1, w), lambda i, j: (i, 0, 0)),
                  _resident(cw.shape), _resident((1, w)), _resident(wg.shape), _resident((1, w)), _resident((1, w)),
                  _resident((1, w)), _resident(wba.shape)],
        out_specs=[pl.BlockSpec((None, ts, d), lambda i, j: (i, j, 0)),
                   pl.BlockSpec((None, tail, w), lambda i, j: (i, 0, 0)),
                   pl.BlockSpec((None, 1, w), lambda i, j: (i, 0, 0))],
        out_shape=[jax.ShapeDtypeStruct((b, s, d), F32), jax.ShapeDtypeStruct((b, tail, w), F32),
                   jax.ShapeDtypeStruct((b, 1, w), F32)],
        scratch_shapes=[pltpu.VMEM((ts + SUBLANES, w), F32), pltpu.VMEM((ts, w), F32), pltpu.VMEM((ts, w), F32),
                        pltpu.VMEM((SUBLANES, w), F32)],
        compiler_params=_params(2),
        name="rnn_prompt",
    )(xr, ug, sga, conv0, h0, cw, cb, wg, bra, brx, lam, wba)


def _rnn_sample_kernel(xs_ref, ug_ref, sga_ref, h0_ref, cw_ref, cb_ref, wg_ref, bra_ref, brx_ref, lam_ref, wba_ref,
                       ya_ref, h_ref, a_s, b_s, *, seg):
    m, w = ug_ref.shape
    xc = cb_ref[...] + cw_ref[0:1, :] * xs_ref[0]
    for k in range(1, CONV_WIDTH):
        xc = xc + cw_ref[k:k + 1, :] * xs_ref[k]
    _rg_gates(xc, wg_ref, bra_ref, brx_ref, lam_ref, a_s, b_s)
    row_in_seg = lax.broadcasted_iota(jnp.int32, (SUBLANES, w), 0) % seg
    steps = tuple(d for d in (1, 2, 4) if d < seg)
    for c in range(m // SUBLANES):
        rows = slice(c * SUBLANES, (c + 1) * SUBLANES)
        a_cum, b_cum = _scan_rows(a_s[rows, :], b_s[rows, :], row_in_seg, steps)
        h_ref[rows, :] = a_cum * h0_ref[rows, :] + b_cum
    u = (h_ref[...] * ug_ref[...]).astype(BF16)
    ya_ref[...] = sga_ref[...] * _dot(u, wba_ref[...])


def _rnn_sample(xs, ug, sga, h0_rows, cw, cb, wg, bra, brx, lam, wba, seg):
    m, w = ug.shape
    d = wba.shape[1]
    return pl.pallas_call(
        functools.partial(_rnn_sample_kernel, seg=seg),
        out_shape=[jax.ShapeDtypeStruct((m, d), F32), jax.ShapeDtypeStruct((m, w), F32)],
        scratch_shapes=[pltpu.VMEM((m, w), F32), pltpu.VMEM((m, w), F32)],
        compiler_params=pltpu.CompilerParams(vmem_limit_bytes=VMEM_LIMIT),
        name="rnn_sample",
    )(xs, ug, sga, h0_rows, cw, cb, wg, bra, brx, lam, wba)


def _attn_prompt_kernel(slope_ref, q_ref, k_ref, v_ref, o_ref, kb_s, vt_s, km_s, bias_s, *, nb):
    g = pl.program_id(1)
    j = pl.program_id(2)
    blk = MOBA_BLOCK

    @pl.when(j == 0)
    def _():
        for n in range(nb):
            rows = slice(n * blk, (n + 1) * blk)
            kn = k_ref[rows, :]
            kb_s[n] = kn.astype(BF16)
            vt_s[n] = v_ref[rows, :].T.astype(BF16)
            km_s[n:n + 1, :] = jnp.sum(kn, axis=0, keepdims=True) * (1.0 / blk)

    key_off = lax.broadcasted_iota(jnp.int32, (blk, blk), 0)
    qry_off = lax.broadcasted_iota(jnp.int32, (blk, blk), 1)
    dist = (qry_off - key_off).astype(F32)
    causal = key_off <= qry_off
    blk_id = lax.broadcasted_iota(jnp.int32, (nb, blk), 0)
    fully_past = blk_id < j
    blocks_back = (j - blk_id).astype(F32) * float(blk)
    km = km_s[...]

    for hh in range(GROUP):
        slope = slope_ref[g * GROUP + hh]
        cols = slice(hh * HEAD_DIM, (hh + 1) * HEAD_DIM)
        qh = q_ref[:, cols]
        qb = qh.astype(BF16)
        gs = jnp.where(fully_past, _dot_nt(km, qh, precision=lax.Precision.HIGHEST), -jnp.inf)
        rank = jnp.zeros((nb, blk), jnp.int32)
        for m in range(nb):
            gm = gs[m:m + 1, :]
            tie = (blk_id > m).astype(jnp.int32)
            rank = rank + jnp.where(gm > gs, 1, jnp.where(gm == gs, tie, 0))
        chosen = jnp.logical_and(fully_past, rank < MOBA_TOPK)
        bias_s[...] = jnp.where(chosen, -slope * blocks_back, NEG)
        base = -slope * dist

        kd = kb_s[j]
        s = jnp.where(causal, _dot_nt(kd, qb) * ATTN_SCALE + base, NEG)
        m0 = jnp.max(s, axis=0, keepdims=True)
        p = jnp.exp(s - m0)
        l0 = jnp.sum(p, axis=0, keepdims=True)
        acc0 = _dot(vt_s[j], p.astype(BF16))

        def body(n, carry):
            m_run, l_run, acc = carry
            s = _dot_nt(kb_s[n], qb) * ATTN_SCALE + base + bias_s[pl.ds(n, 1), :]
            m_new = jnp.maximum(m_run, jnp.max(s, axis=0, keepdims=True))
            alpha = jnp.exp(m_run - m_new)
            p = jnp.exp(s - m_new)
            l_new = alpha * l_run + jnp.sum(p, axis=0, keepdims=True)
            return m_new, l_new, alpha * acc + _dot(vt_s[n], p.astype(BF16))

        _, l_fin, acc = lax.fori_loop(0, j, body, (m0, l0, acc0))
        o_ref[:, cols] = (acc / l_fin).T.astype(o_ref.dtype)


def _attn_prompt(q, k, v):
    b, s, _ = q.shape
    assert s % MOBA_BLOCK == 0
    nb = s // MOBA_BLOCK
    gw = GROUP * HEAD_DIM
    slopes = jnp.asarray(ALIBI_SLOPES, F32)
    kv_spec = pl.BlockSpec((None, s, HEAD_DIM), lambda i, g, j: (i, 0, g))
    return pl.pallas_call(
        functools.partial(_attn_prompt_kernel, nb=nb),
        grid=(b, N_KV_HEADS, nb),
        in_specs=[pl.BlockSpec(memory_space=pltpu.SMEM),
                  pl.BlockSpec((None, MOBA_BLOCK, gw), lambda i, g, j: (i, j, g)), kv_spec, kv_spec],
        out_specs=pl.BlockSpec((None, MOBA_BLOCK, gw), lambda i, g, j: (i, j, g)),
        out_shape=jax.ShapeDtypeStruct(q.shape, BF16),
        scratch_shapes=[pltpu.VMEM((nb, MOBA_BLOCK, HEAD_DIM), BF16), pltpu.VMEM((nb, HEAD_DIM, MOBA_BLOCK), BF16),
                        pltpu.VMEM((nb, HEAD_DIM), F32), pltpu.VMEM((nb, MOBA_BLOCK), F32)],
        compiler_params=_params(3),
        name="attn_prompt",
    )(slopes, q, k, v)


def _row_slopes(head):
    out = jnp.zeros(head.shape, F32)
    for h, sl in enumerate(ALIBI_SLOPES):
        out = jnp.where(head == h, sl, out)
    return out


def _attn_sample_kernel(pt_ref, q_ref, kn_ref, vn_ref, *rest, pp, n_blk, past, n_new):
    k_pages, v_pages = rest[:pp], rest[pp:2 * pp]
    o_ref = rest[2 * pp]
    km_s, m_s, l_s, op_s, bias_s = rest[2 * pp + 1:]
    del pt_ref
    s_id = pl.program_id(1)
    nq = q_ref.shape[0]
    kv = N_KV_HEADS
    cols = MOBA_BLOCK * kv
    q = q_ref[...]
    qb = q.astype(BF16)

    row = lax.broadcasted_iota(jnp.int32, (nq, cols), 0)
    col = lax.broadcasted_iota(jnp.int32, (nq, cols), 1)
    head = row % N_HEADS
    step = row // N_HEADS

    @pl.when(s_id == 0)
    def _():
        same_kv = (col % kv) == head // GROUP
        back = (past + step - col // kv).astype(F32)
        bias_s[...] = jnp.where(same_kv, -_row_slopes(head) * back, NEG)
        m_s[...] = jnp.zeros_like(m_s)
        l_s[...] = jnp.zeros_like(l_s)

    slope_col = _row_slopes(head[:, 0:1])
    lane = lax.broadcasted_iota(jnp.int32, (nq, 128), 1)
    per_step = pp // 2
    for i in range(per_step):
        n = s_id * per_step + i
        k0, k1 = k_pages[2 * i][...], k_pages[2 * i + 1][...]
        ksum = (jnp.sum(k0.reshape(-1, SUBLANES, HEAD_DIM), axis=0)
                + jnp.sum(k1.reshape(-1, SUBLANES, HEAD_DIM), axis=0))
        km_s[pl.ds(pl.multiple_of(n * SUBLANES, SUBLANES), SUBLANES), :] = ksum * (1.0 / MOBA_BLOCK)
        kb = jnp.concatenate([k0, k1], axis=0).astype(BF16)
        s = _dot_nt(qb, kb) * ATTN_SCALE + bias_s[...] + slope_col * (n * MOBA_BLOCK).astype(F32)
        m = jnp.max(s, axis=-1, keepdims=True)
        p = jnp.exp(s - m)
        l = jnp.sum(p, axis=-1, keepdims=True)
        vb = jnp.concatenate([v_pages[2 * i][...], v_pages[2 * i + 1][...]], axis=0).astype(BF16)
        op_s[n] = _dot(p.astype(BF16), vb)
        m_s[...] = jnp.where(lane == n, m, m_s[...])
        l_s[...] = jnp.where(lane == n, l, l_s[...])

    @pl.when(s_id == pl.num_programs(1) - 1)
    def _():
        gcols = n_blk * SUBLANES
        gfull = _dot_nt(q, km_s[...], precision=lax.Precision.HIGHEST)
        gsum = gfull + pltpu.roll(gfull, gcols - kv, 1)
        grow = lax.broadcasted_iota(jnp.int32, (nq, gcols), 0)
        gcol = lax.broadcasted_iota(jnp.int32, (nq, gcols), 1)
        gs = jnp.where((gcol % SUBLANES) == (grow % N_HEADS) // GROUP, gsum, -jnp.inf)
        gcol_f = gcol.astype(F32)
        chosen = jnp.zeros((nq, 128), F32)
        for _ in range(MOBA_TOPK):
            best = jnp.max(gs, axis=-1, keepdims=True)
            first = jnp.min(jnp.where(gs == best, gcol_f, float(gcols)), axis=-1, keepdims=True)
            chosen = jnp.where(lane == (first.astype(jnp.int32) // SUBLANES), 1.0, chosen)
            gs = jnp.where(gcol_f == first, -jnp.inf, gs)
        is_chosen = chosen > 0.0

        nk = n_new * kv
        orow = lax.broadcasted_iota(jnp.int32, (nq, nk), 0)
        ocol = lax.broadcasted_iota(jnp.int32, (nq, nk), 1)
        ohead, ostep = orow % N_HEADS, orow // N_HEADS
        ok = jnp.logical_and((ocol % kv) == ohead // GROUP, ocol // kv <= ostep)
        so = _dot_nt(qb, kn_ref[...].astype(BF16)) * ATTN_SCALE - _row_slopes(ohead) * (ostep - ocol // kv).astype(F32)
        so = jnp.where(ok, so, NEG)

        m_all = jnp.maximum(jnp.max(so, axis=-1, keepdims=True),
                            jnp.max(jnp.where(is_chosen, m_s[...], NEG), axis=-1, keepdims=True))
        wgt = jnp.where(is_chosen, jnp.exp(m_s[...] - m_all), 0.0)
        po = jnp.exp(so - m_all)
        l_all = jnp.sum(wgt * l_s[...], axis=-1, keepdims=True) + jnp.sum(po, axis=-1, keepdims=True)
        acc = _dot(po.astype(BF16), vn_ref[...].astype(BF16))
        for n in range(n_blk):
            acc = acc + wgt[:, n:n + 1] * op_s[n]
        o_ref[...] = acc / l_all


def _attn_sample(q, k_new, v_new, cache_k, cache_v, page_ids, n_new):
    b, nq, _ = q.shape
    n_pages = page_ids.shape[1]
    past = n_pages * PAGE_SIZE
    pp = PAGES_PER_STEP
    assert MOBA_BLOCK == 2 * PAGE_SIZE and past % MOBA_BLOCK == 0 and n_pages % pp == 0
    n_blk = past // MOBA_BLOCK
    assert MOBA_TOPK <= n_blk <= 128 and n_new <= MOBA_BLOCK
    rows = PAGE_SIZE * N_KV_HEADS

    def page(i):
        return pl.BlockSpec((None, rows, HEAD_DIM), lambda bi, si, pt: (pt[bi * n_pages + si * pp + i], 0, 0))

    def per_seq(r):
        return pl.BlockSpec((None, r, HEAD_DIM), lambda bi, si, pt: (bi, 0, 0))

    return pl.pallas_call(
        functools.partial(_attn_sample_kernel, pp=pp, n_blk=n_blk, past=past, n_new=n_new),
        grid_spec=pltpu.PrefetchScalarGridSpec(
            num_scalar_prefetch=1,
            grid=(b, n_pages // pp),
            in_specs=[per_seq(nq), per_seq(n_new * N_KV_HEADS), per_seq(n_new * N_KV_HEADS)]
            + [page(i) for i in range(pp)] * 2,
            out_specs=per_seq(nq),
            scratch_shapes=[pltpu.VMEM((n_blk * SUBLANES, HEAD_DIM), F32), pltpu.VMEM((nq, 128), F32),
                            pltpu.VMEM((nq, 128), F32), pltpu.VMEM((n_blk, nq, HEAD_DIM), F32),
                            pltpu.VMEM((nq, MOBA_BLOCK * N_KV_HEADS), F32)]),
        out_shape=jax.ShapeDtypeStruct((b, nq, HEAD_DIM), F32),
        compiler_params=_params(2),
        name="attn_sample",
    )(page_ids.reshape(-1), q, k_new, v_new, *([cache_k] * pp), *([cache_v] * pp))


def _merge_ffn_kernel(x_ref, ya_ref, o_ref_in, sgb_ref, g2_ref, sh_ref, sc_ref, g3_ref, ng_ref,
                      wbb_ref, wout_ref, wg_ref, wu_ref, wd_ref, out_ref, *, chunk):
    yb = _dot(o_ref_in[...].astype(BF16), wbb_ref[...])
    merged = (ya_ref[...] + sgb_ref[...] * yb).astype(BF16)
    x = x_ref[...] + g2_ref[...] * _dot(merged, wout_ref[...])
    h = (_rms(x, ng_ref[...]) * (1.0 + sc_ref[...]) + sh_ref[...]).astype(BF16)
    out_ref[...] = x + 0.5 * g3_ref[...] * _swiglu_acc(h, wg_ref, wu_ref, wd_ref, chunk)


def _merge_ffn(x, ya, o, sgb, g2, sh, sc, g3, ng, wbb, wout, wg, wu, wd, tm, tiles_per_group):
    t, d = x.shape
    row = pl.BlockSpec((tm, d), lambda i: (i, 0))
    mod = functools.partial(_mod_spec, tm=tm, tiles_per_group=tiles_per_group)
    return pl.pallas_call(
        functools.partial(_merge_ffn_kernel, chunk=_ffn_chunk(wg.shape[1])),
        grid=(t // tm,),
        in_specs=[row, row, pl.BlockSpec((tm, o.shape[1]), lambda i: (i, 0)), row,
                  mod(g2), mod(sh), mod(sc), mod(g3), _resident((1, d)),
                  _resident(wbb.shape), _resident(wout.shape), _resident(wg.shape), _resident(wu.shape),
                  _resident(wd.shape)],
        out_specs=row,
        out_shape=jax.ShapeDtypeStruct((t, d), F32),
        compiler_params=_params(1),
        name="merge_ffn",
    )(x, ya, o, sgb, g2, sh, sc, g3, ng, wbb, wout, wg, wu, wd)


def _token_tile(t):
    return 512 if t % 512 == 0 else t


def _layer_weights(p):
    w = dict(p)
    for name in ("ffn1_gate", "ffn1_up", "ffn1_down", "w_in", "w_branch_a", "w_branch_b", "w_out",
                 "ffn2_gate", "ffn2_up", "ffn2_down"):
        w[name] = p[name].astype(BF16)
    w["rg_w"] = jnp.concatenate([p["rg_wa"], p["rg_wx"]], axis=-1).astype(BF16)
    for name in ("norm1_g", "norm2_g", "norm3_g", "conv_b", "rg_ba", "rg_bx", "rg_lambda", "q_norm_g", "k_norm_g"):
        w[name] = p[name].reshape(1, -1)
    return w


def _prompt_layer(x, mod, w):
    b, s, d = x.shape
    t = b * s
    tm = _token_tile(s)
    tpg = s // tm
    sh1, sc1, g1, sh2, sc2, g2, sh3, sc3, g3 = [m[:, None, :] for m in jnp.split(mod, 3 * N_SUBLAYERS, axis=-1)]
    x1 = _ffn(x.reshape(t, d), sh1, sc1, g1, w["norm1_g"], w["ffn1_gate"], w["ffn1_up"], w["ffn1_down"], tm, tpg)
    tm_in = min(tm, 256)
    xr, ug, q, k, v, sga, sgb = _inproj(x1, sh2, sc2, w["norm2_g"], w["w_in"], w["q_norm_g"], w["k_norm_g"],
                                        tm_in, s // tm_in)
    d_rnn = xr.shape[1]
    ts = min(s, 256)
    ya, new_conv, h_last = _rnn_prompt(
        xr.reshape(b, s, d_rnn), ug.reshape(b, s, d_rnn), sga.reshape(b, s, d),
        jnp.zeros((b, CONV_WIDTH - 1, d_rnn), F32), jnp.zeros((b, 1, d_rnn), F32),
        w["conv_w"], w["conv_b"], w["rg_w"], w["rg_ba"], w["rg_bx"], w["rg_lambda"], w["w_branch_a"], ts)
    o = _attn_prompt(q.reshape(b, s, -1), k.reshape(b, s, -1), v.reshape(b, s, -1))
    y = _merge_ffn(x1, ya.reshape(t, d), o.reshape(t, -1), sgb, g2, sh3, sc3, g3, w["norm3_g"],
                   w["w_branch_b"], w["w_out"], w["ffn2_gate"], w["ffn2_up"], w["ffn2_down"], tm, tpg)
    return (y.reshape(b, s, d), k.reshape(b, s, N_KV_HEADS, HEAD_DIM), v.reshape(b, s, N_KV_HEADS, HEAD_DIM),
            new_conv, h_last.reshape(b, d_rnn))


def _sample_layer(x, mod, conv_buf, h0, cache_k, cache_v, page_ids, w):
    b, s, d = x.shape
    t = b * s
    rep = lambda m: jnp.repeat(m, s, axis=0)[None]
    sh1, sc1, g1, sh2, sc2, g2, sh3, sc3, g3 = [rep(m) for m in jnp.split(mod, 3 * N_SUBLAYERS, axis=-1)]
    x1 = _ffn(x.reshape(t, d), sh1, sc1, g1, w["norm1_g"], w["ffn1_gate"], w["ffn1_up"], w["ffn1_down"], t, 1)
    xr, ug, q, k, v, sga, sgb = _inproj(x1, sh2, sc2, w["norm2_g"], w["w_in"], w["q_norm_g"], w["k_norm_g"], t, 1)
    d_rnn = xr.shape[1]
    xp = jnp.concatenate([conv_buf, xr.reshape(b, s, d_rnn)], axis=1)
    taps = jnp.stack([xp[:, kk:kk + s].reshape(t, d_rnn) for kk in range(CONV_WIDTH)])
    ya, h_all = _rnn_sample(taps, ug, sga, jnp.repeat(h0, s, axis=0), w["conv_w"], w["conv_b"], w["rg_w"],
                            w["rg_ba"], w["rg_bx"], w["rg_lambda"], w["w_branch_a"], s)
    o = _attn_sample(q.reshape(b, s * N_HEADS, HEAD_DIM), k.reshape(b, s * N_KV_HEADS, HEAD_DIM),
                     v.reshape(b, s * N_KV_HEADS, HEAD_DIM), cache_k, cache_v, page_ids, s)
    y = _merge_ffn(x1, ya, o.reshape(t, -1), sgb, g2, sh3, sc3, g3, w["norm3_g"],
                   w["w_branch_b"], w["w_out"], w["ffn2_gate"], w["ffn2_up"], w["ffn2_down"], t, 1)
    return (y.reshape(b, s, d), k.reshape(b, s, N_KV_HEADS, HEAD_DIM), v.reshape(b, s, N_KV_HEADS, HEAD_DIM),
            xp[:, -(CONV_WIDTH - 1):], h_all.reshape(b, s, d_rnn)[:, -1])


def kernel(x_prompt, x_sample, c_prompt, c_sample, cache_k, cache_v, state_conv, state_rglru, page_table, w_ada, b_ada, norm1_g, ffn1_gate, ffn1_up, ffn1_down, norm2_g, w_in, conv_w, conv_b, rg_wa, rg_ba, rg_wx, rg_bx, rg_lambda, q_norm_g, k_norm_g, w_branch_a, w_branch_b, w_out, norm3_g, ffn2_gate, ffn2_up, ffn2_down):
    depth, n_pool = cache_k.shape[0], cache_k.shape[1]
    bp = x_prompt.shape[0]
    assert cache_k.shape[2:] == (PAGE_SIZE, N_KV_HEADS, HEAD_DIM)
    ck = cache_k.reshape(depth * n_pool, PAGE_SIZE * N_KV_HEADS, HEAD_DIM)
    cv = cache_v.reshape(depth * n_pool, PAGE_SIZE * N_KV_HEADS, HEAD_DIM)
    c_all = jnp.concatenate([c_prompt, c_sample], axis=0)
    pad = -c_all.shape[0] % SUBLANES
    c_all = jnp.pad(c_all, ((0, pad), (0, 0)))
    yp, ys = x_prompt, x_sample
    outs = [[] for _ in range(8)]
    for l in range(depth):
        p = dict(norm1_g=norm1_g[l], ffn1_gate=ffn1_gate[l], ffn1_up=ffn1_up[l], ffn1_down=ffn1_down[l],
                 norm2_g=norm2_g[l], w_in=w_in[l], conv_w=conv_w[l], conv_b=conv_b[l], rg_wa=rg_wa[l],
                 rg_ba=rg_ba[l], rg_wx=rg_wx[l], rg_bx=rg_bx[l], rg_lambda=rg_lambda[l], q_norm_g=q_norm_g[l],
                 k_norm_g=k_norm_g[l], w_branch_a=w_branch_a[l], w_branch_b=w_branch_b[l], w_out=w_out[l],
                 norm3_g=norm3_g[l], ffn2_gate=ffn2_gate[l], ffn2_up=ffn2_up[l], ffn2_down=ffn2_down[l])
        w = _layer_weights(p)
        mod = _ada(c_all, w_ada[l], b_ada[l])
        yp, kp, vp, cp, hp = _prompt_layer(yp, mod[:bp], w)
        ys, ks, vs, cs, hs = _sample_layer(ys, mod[bp:bp + x_sample.shape[0]], state_conv[l], state_rglru[l],
                                           ck, cv, page_table + l * n_pool, w)
        for lst, val in zip(outs, (kp, vp, cp, hp, ks, vs, cs, hs)):
            lst.append(val)
    return (yp, ys) + tuple(jnp.stack(o) for o in outs)
```

```python
import functools

import jax
import jax.numpy as jnp
from jax import lax
from jax.experimental import pallas as pl
from jax.experimental.pallas import tpu as pltpu

F32 = jnp.float32
BF16 = jnp.bfloat16

N_HEADS = 8
N_KV_HEADS = 4
HEAD_DIM = 128
GROUP = N_HEADS // N_KV_HEADS
MOBA_BLOCK = 256
MOBA_TOPK = 3
PAGE_SIZE = 128
N_RNN_BLOCKS = 8
RNN_BLOCK = 128
CONV_WIDTH = 4
RG_C = 8.0
N_SUBLAYERS = 3
EPS = 1e-6
ALIBI_SLOPES = tuple(2.0 ** (-8.0 * (h + 1) / N_HEADS) for h in range(N_HEADS))
ATTN_SCALE = HEAD_DIM ** -0.5
NEG = -1e30
LOG2E = 1.4426950408889634
SUBLANES = 8
VMEM_LIMIT = 56 * 1024 * 1024
PAGES_PER_STEP = 16
KV_BLOCKS_PER_STEP = 4


def _params(n_axes):
    return pltpu.CompilerParams(dimension_semantics=("arbitrary",) * n_axes, vmem_limit_bytes=VMEM_LIMIT)


def _resident(shape):
    nd = len(shape)
    return pl.BlockSpec(shape, lambda *_: (0,) * nd, pipeline_mode=pl.Buffered(1))


def _dot(a, b):
    return jnp.dot(a, b, preferred_element_type=F32)


def _dot_nt(a, b, precision=None):
    return lax.dot_general(a, b, (((1,), (1,)), ((), ())), precision=precision, preferred_element_type=F32)


def _rms(x, g):
    return x * lax.rsqrt(jnp.mean(x * x, axis=-1, keepdims=True) + EPS) * g


def _silu(x):
    return x * jax.nn.sigmoid(x)


def _ada_kernel(c_ref, w_ref, b_ref, o_ref):
    a = _silu(c_ref[...]).astype(BF16)
    o_ref[...] = _dot(a, w_ref[...].astype(BF16)) + b_ref[...]


def _ada(c, w, b):
    m, d = c.shape
    n = w.shape[1]
    tn = n // 8
    return pl.pallas_call(
        _ada_kernel,
        grid=(n // tn,),
        in_specs=[pl.BlockSpec((m, d), lambda i: (0, 0)),
                  pl.BlockSpec((d, tn), lambda i: (0, i)),
                  pl.BlockSpec((1, tn), lambda i: (0, i))],
        out_specs=pl.BlockSpec((m, tn), lambda i: (0, i)),
        out_shape=jax.ShapeDtypeStruct((m, n), F32),
        compiler_params=_params(1),
        name="ada",
    )(c, w, b.reshape(1, n))


def _swiglu_acc(h, wg_ref, wu_ref, wd_ref, chunk):
    d_ff = wg_ref.shape[1]
    acc = None
    for c in range(d_ff // chunk):
        sl = slice(c * chunk, (c + 1) * chunk)
        act = (_silu(_dot(h, wg_ref[:, sl])) * _dot(h, wu_ref[:, sl])).astype(BF16)
        part = _dot(act, wd_ref[sl, :])
        acc = part if acc is None else acc + part
    return acc


def _ffn_kernel(x_ref, sh_ref, sc_ref, g_ref, ng_ref, wg_ref, wu_ref, wd_ref, o_ref, *, chunk):
    x = x_ref[...]
    h = (_rms(x, ng_ref[...]) * (1.0 + sc_ref[...]) + sh_ref[...]).astype(BF16)
    o_ref[...] = x + 0.5 * g_ref[...] * _swiglu_acc(h, wg_ref, wu_ref, wd_ref, chunk)


def _mod_spec(mod, tm, tiles_per_group):
    r, d = mod.shape[1], mod.shape[2]
    return pl.BlockSpec((None, r, d), lambda i: (i // tiles_per_group, 0, 0))


def _ffn_chunk(d_ff):
    return 256 if d_ff % 256 == 0 else 128


def _ffn(x, sh, sc, g, ng, wg, wu, wd, tm, tiles_per_group):
    t, d = x.shape
    d_ff = wg.shape[1]
    row = pl.BlockSpec((tm, d), lambda i: (i, 0))
    return pl.pallas_call(
        functools.partial(_ffn_kernel, chunk=_ffn_chunk(d_ff)),
        grid=(t // tm,),
        in_specs=[row, _mod_spec(sh, tm, tiles_per_group), _mod_spec(sc, tm, tiles_per_group),
                  _mod_spec(g, tm, tiles_per_group), _resident((1, d)),
                  _resident(wg.shape), _resident(wu.shape), _resident(wd.shape)],
        out_specs=row,
        out_shape=jax.ShapeDtypeStruct((t, d), F32),
        compiler_params=_params(1),
        name="ffn",
    )(x, sh, sc, g, ng, wg, wu, wd)


def _inproj_kernel(x_ref, sh_ref, sc_ref, ng_ref, w_ref, qg_ref, kg_ref,
                   xr_ref, ug_ref, q_ref, k_ref, v_ref, sga_ref, sgb_ref):
    d_rnn = xr_ref.shape[1]
    dq = q_ref.shape[1]
    dk = k_ref.shape[1]
    x = x_ref[...]
    h = (_rms(x, ng_ref[...]) * (1.0 + sc_ref[...]) + sh_ref[...]).astype(BF16)
    o = 0
    xr_ref[...] = _dot(h, w_ref[:, o:o + d_rnn])
    o += d_rnn
    ug_ref[...] = jax.nn.gelu(_dot(h, w_ref[:, o:o + d_rnn]))
    o += d_rnn
    for hd in range(dq // HEAD_DIM):
        sl = slice(hd * HEAD_DIM, (hd + 1) * HEAD_DIM)
        q_ref[:, sl] = _rms(_dot(h, w_ref[:, o + hd * HEAD_DIM:o + (hd + 1) * HEAD_DIM]), qg_ref[...])
    o += dq
    for hd in range(dk // HEAD_DIM):
        sl = slice(hd * HEAD_DIM, (hd + 1) * HEAD_DIM)
        k_ref[:, sl] = _rms(_dot(h, w_ref[:, o + hd * HEAD_DIM:o + (hd + 1) * HEAD_DIM]), kg_ref[...])
    o += dk
    v_ref[...] = _dot(h, w_ref[:, o:o + dk])
    o += dk
    d = sga_ref.shape[1]
    sga_ref[...] = jax.nn.sigmoid(_dot(h, w_ref[:, o:o + d]))
    o += d
    sgb_ref[...] = jax.nn.sigmoid(_dot(h, w_ref[:, o:o + d]))


def _inproj(x, sh, sc, ng, w_in, qg, kg, tm, tiles_per_group):
    t, d = x.shape
    dq, dk = N_HEADS * HEAD_DIM, N_KV_HEADS * HEAD_DIM
    d_rnn = (w_in.shape[1] - dq - 2 * dk - 2 * d) // 2

    def row(w):
        return pl.BlockSpec((tm, w), lambda i: (i, 0))

    widths = (d_rnn, d_rnn, dq, dk, dk, d, d)
    return pl.pallas_call(
        _inproj_kernel,
        grid=(t // tm,),
        in_specs=[row(d), _mod_spec(sh, tm, tiles_per_group), _mod_spec(sc, tm, tiles_per_group),
                  _resident((1, d)), _resident(w_in.shape), _resident((1, HEAD_DIM)), _resident((1, HEAD_DIM))],
        out_specs=[row(w) for w in widths],
        out_shape=[jax.ShapeDtypeStruct((t, w), F32) for w in widths],
        compiler_params=_params(1),
        name="inproj",
    )(x, sh, sc, ng, w_in, qg, kg)


def _rg_gates(xc, wg_ref, bra_ref, brx_ref, lam_ref, a_ref, b_ref):
    sp = jax.nn.softplus(-lam_ref[...])
    for n in range(N_RNN_BLOCKS):
        sl = slice(n * RNN_BLOCK, (n + 1) * RNN_BLOCK)
        xcn = xc[:, sl]
        z = _dot(xcn.astype(BF16), wg_ref[n])
        r = jax.nn.sigmoid(z[:, :RNN_BLOCK] + bra_ref[:, sl])
        i = jax.nn.sigmoid(z[:, RNN_BLOCK:] + brx_ref[:, sl])
        log_a = -RG_C * r * sp[:, sl]
        a = jnp.exp(log_a)
        a_ref[:, sl] = a
        b_ref[:, sl] = jnp.sqrt(1.0 - a * a) * (i * xcn)


def _scan_rows(a, b, row_in_seg, steps):
    for d in steps:
        keep = row_in_seg >= d
        a_prev = jnp.where(keep, pltpu.roll(a, d, 0), 1.0)
        b_prev = jnp.where(keep, pltpu.roll(b, d, 0), 0.0)
        b = a * b_prev + b
        a = a * a_prev
    return a, b


def _rnn_prompt_kernel(xr_ref, ug_ref, sga_ref, conv0_ref, h0_ref, cw_ref, cb_ref, wg_ref, bra_ref, brx_ref,
                       lam_ref, wba_ref, ya_ref, nconv_ref, hlast_ref, xbuf, a_s, b_s, hc_s, *, ts):
    s = pl.program_id(1)
    w = xr_ref.shape[1]
    tail = CONV_WIDTH - 1
    lo = SUBLANES - tail

    @pl.when(s == 0)
    def _():
        xbuf[lo:SUBLANES, :] = conv0_ref[...]
        hc_s[...] = jnp.broadcast_to(h0_ref[...], hc_s.shape)

    xbuf[SUBLANES:SUBLANES + ts, :] = xr_ref[...]
    xc = cb_ref[...] + cw_ref[0:1, :] * xbuf[lo:lo + ts, :]
    for k in range(1, CONV_WIDTH):
        xc = xc + cw_ref[k:k + 1, :] * xbuf[lo + k:lo + k + ts, :]
    new_tail = xbuf[ts + lo:ts + SUBLANES, :]
    nconv_ref[...] = new_tail
    xbuf[lo:SUBLANES, :] = new_tail
    _rg_gates(xc, wg_ref, bra_ref, brx_ref, lam_ref, a_s, b_s)

    row = lax.broadcasted_iota(jnp.int32, (SUBLANES, w), 0)

    def body(c, hc):
        off = pl.multiple_of(c * SUBLANES, SUBLANES)
        a_cum, b_cum = _scan_rows(a_s[pl.ds(off, SUBLANES), :], b_s[pl.ds(off, SUBLANES), :], row, (1, 2, 4))
        h = a_cum * hc + b_cum
        a_s[pl.ds(off, SUBLANES), :] = h
        return jnp.broadcast_to(h[SUBLANES - 1:SUBLANES, :], (SUBLANES, w))

    hc = lax.fori_loop(0, ts // SUBLANES, body, hc_s[...])
    hc_s[...] = hc
    hlast_ref[...] = hc[0:1, :]
    u = (a_s[...] * ug_ref[...]).astype(BF16)
    ya_ref[...] = sga_ref[...] * _dot(u, wba_ref[...])


def _rnn_prompt(xr, ug, sga, conv0, h0, cw, cb, wg, bra, brx, lam, wba, ts):
    b, s, w = xr.shape
    d = wba.shape[1]
    tail = CONV_WIDTH - 1
    seq = pl.BlockSpec((None, ts, w), lambda i, j: (i, j, 0))
    return pl.pallas_call(
        functools.partial(_rnn_prompt_kernel, ts=ts),
        grid=(b, s // ts),
        in_specs=[seq, seq, pl.BlockSpec((None, ts, d), lambda i, j: (i, j, 0)),
                  pl.BlockSpec((None, tail, w), lambda i, j: (i, 0, 0)),
                  pl.BlockSpec((None, 1, w), lambda i, j: (i, 0, 0)),
                  _resident(cw.shape), _resident((1, w)), _resident(wg.shape), _resident((1, w)), _resident((1, w)),
                  _resident((1, w)), _resident(wba.shape)],
        out_specs=[pl.BlockSpec((None, ts, d), lambda i, j: (i, j, 0)),
                   pl.BlockSpec((None, tail, w), lambda i, j: (i, 0, 0)),
                   pl.BlockSpec((None, 1, w), lambda i, j: (i, 0, 0))],
        out_shape=[jax.ShapeDtypeStruct((b, s, d), F32), jax.ShapeDtypeStruct((b, tail, w), F32),
                   jax.ShapeDtypeStruct((b, 1, w), F32)],
        scratch_shapes=[pltpu.VMEM((ts + SUBLANES, w), F32), pltpu.VMEM((ts, w), F32), pltpu.VMEM((ts, w), F32),
                        pltpu.VMEM((SUBLANES, w), F32)],
        compiler_params=_params(2),
        name="rnn_prompt",
    )(xr, ug, sga, conv0, h0, cw, cb, wg, bra, brx, lam, wba)


def _rnn_sample_kernel(xs_ref, ug_ref, sga_ref, h0_ref, cw_ref, cb_ref, wg_ref, bra_ref, brx_ref, lam_ref, wba_ref,
                       ya_ref, h_ref, a_s, b_s, *, seg):
    m, w = ug_ref.shape
    xc = cb_ref[...] + cw_ref[0:1, :] * xs_ref[0]
    for k in range(1, CONV_WIDTH):
        xc = xc + cw_ref[k:k + 1, :] * xs_ref[k]
    _rg_gates(xc, wg_ref, bra_ref, brx_ref, lam_ref, a_s, b_s)
    row_in_seg = lax.broadcasted_iota(jnp.int32, (SUBLANES, w), 0) % seg
    steps = tuple(d for d in (1, 2, 4) if d < seg)
    for c in range(m // SUBLANES):
        rows = slice(c * SUBLANES, (c + 1) * SUBLANES)
        a_cum, b_cum = _scan_rows(a_s[rows, :], b_s[rows, :], row_in_seg, steps)
        h_ref[rows, :] = a_cum * h0_ref[rows, :] + b_cum
    u = (h_ref[...] * ug_ref[...]).astype(BF16)
    ya_ref[...] = sga_ref[...] * _dot(u, wba_ref[...])


def _rnn_sample(xs, ug, sga, h0_rows, cw, cb, wg, bra, brx, lam, wba, seg):
    m, w = ug.shape
    d = wba.shape[1]
    return pl.pallas_call(
        functools.partial(_rnn_sample_kernel, seg=seg),
        out_shape=[jax.ShapeDtypeStruct((m, d), F32), jax.ShapeDtypeStruct((m, w), F32)],
        scratch_shapes=[pltpu.VMEM((m, w), F32), pltpu.VMEM((m, w), F32)],
        compiler_params=pltpu.CompilerParams(vmem_limit_bytes=VMEM_LIMIT),
        name="rnn_sample",
    )(xs, ug, sga, h0_rows, cw, cb, wg, bra, brx, lam, wba)


V_ROWS = HEAD_DIM + 16


def _attn_prompt_kernel(slope_ref, q_ref, k_ref, v_ref, o_ref, kb_s, vt_s, kbg_s, vtg_s, km_s, q2_s, bias_s, colb_s,
                        acc_s, *, nb, gt):
    g = pl.program_id(1)
    blk = MOBA_BLOCK
    s_len = nb * blk
    q_scale = ATTN_SCALE * LOG2E

    ones_rows = (lax.broadcasted_iota(jnp.int32, (V_ROWS - HEAD_DIM, blk), 0) == 0).astype(BF16)
    for n in range(nb):
        rows = slice(n * blk, (n + 1) * blk)
        grp, sub = n // gt, slice((n % gt) * blk, (n % gt + 1) * blk)
        kn = k_ref[rows, :]
        kb = kn.astype(BF16)
        vt = v_ref[rows, :].T.astype(BF16)
        kb_s[n] = kb
        kbg_s[grp, sub, :] = kb
        vt_s[n, 0:HEAD_DIM, :] = vt
        vt_s[n, HEAD_DIM:V_ROWS, :] = ones_rows
        vtg_s[grp, 0:HEAD_DIM, sub] = vt
        vtg_s[grp, HEAD_DIM:V_ROWS, sub] = ones_rows
        km_s[n:n + 1, :] = jnp.sum(kn, axis=0, keepdims=True) * (1.0 / blk)
        q2_s[n] = (q_ref[rows, :] * q_scale).astype(BF16)

    blk_id = lax.broadcasted_iota(jnp.int32, (nb, s_len), 0)
    q_blk = lax.broadcasted_iota(jnp.int32, (nb, s_len), 1) // blk
    fully_past = blk_id < q_blk
    blocks_ahead = ((blk_id - q_blk) * blk).astype(F32)
    key_off = lax.broadcasted_iota(jnp.int32, (blk, blk), 0)
    causal = key_off <= lax.broadcasted_iota(jnp.int32, (blk, blk), 1)
    km = km_s[...]
    for hh in range(GROUP):
        slope2 = slope_ref[g * GROUP + hh] * LOG2E
        cols = slice(hh * HEAD_DIM, (hh + 1) * HEAD_DIM)
        gs = jnp.where(fully_past, _dot_nt(km, q_ref[:, cols], precision=lax.Precision.HIGHEST), -jnp.inf)
        rank = jnp.zeros((nb, s_len), jnp.int32)
        for m in range(nb):
            gm = gs[m:m + 1, :]
            tie = (blk_id > m).astype(jnp.int32)
            rank = rank + jnp.where(gm > gs, 1, jnp.where(gm == gs, tie, 0))
        chosen = jnp.logical_and(fully_past, rank < MOBA_TOPK)
        bias = jnp.where(chosen, slope2 * blocks_ahead, NEG)
        for jq in range(nb):
            bias_s[hh, jq] = bias[:, jq * blk:(jq + 1) * blk]
        colb_s[hh] = slope2 * key_off.astype(F32)

    heads = [slice(hh * HEAD_DIM, (hh + 1) * HEAD_DIM) for hh in range(GROUP)]

    def q_block(j, _):
        q2 = q2_s[j]
        kd = kb_s[j]
        vd = vt_s[j]
        s_own = [_dot_nt(kd, q2[:, cols]) for cols in heads]
        m_run = []
        for hh in range(GROUP):
            s = jnp.where(causal, s_own[hh] + colb_s[hh], NEG)
            m0 = jnp.max(s, axis=0, keepdims=True)
            acc_s[hh] = _dot(vd, jnp.exp2(s - m0).astype(BF16))
            m_run.append(m0)

        def kv_group(gi, m_run):
            kg = kbg_s[gi]
            vg = vtg_s[gi]
            s_grp = [_dot_nt(kg, q2[:, cols]) for cols in heads]
            out, probs = [], []
            for hh in range(GROUP):
                parts = [s_grp[hh][t * blk:(t + 1) * blk, :] + colb_s[hh] + bias_s[hh, j, pl.ds(gi * gt + t, 1), :]
                         for t in range(gt)]
                m_new = m_run[hh]
                for part in parts:
                    m_new = jnp.maximum(m_new, jnp.max(part, axis=0, keepdims=True))
                probs.append(jnp.concatenate([jnp.exp2(part - m_new).astype(BF16) for part in parts], axis=0))
                out.append(m_new)
            pv = [_dot(vg, p) for p in probs]
            for hh in range(GROUP):
                acc_s[hh] = jnp.exp2(m_run[hh] - out[hh]) * acc_s[hh] + pv[hh]
            return tuple(out)

        lax.fori_loop(0, (j + gt - 1) // gt, kv_group, tuple(m_run))
        rows = pl.ds(pl.multiple_of(j * blk, blk), blk)
        for hh in range(GROUP):
            acc = acc_s[hh]
            o = acc[0:HEAD_DIM, :] / acc[HEAD_DIM:HEAD_DIM + 1, :]
            o_ref[rows, heads[hh]] = o.T.astype(o_ref.dtype)
        return 0

    lax.fori_loop(0, nb, q_block, 0)


def _attn_prompt(q, k, v):
    b, s, _ = q.shape
    assert s % MOBA_BLOCK == 0
    nb = s // MOBA_BLOCK
    gt = KV_BLOCKS_PER_STEP if nb % KV_BLOCKS_PER_STEP == 0 else 1
    gw = GROUP * HEAD_DIM
    slopes = jnp.asarray(ALIBI_SLOPES, F32)
    kv_spec = pl.BlockSpec((None, s, HEAD_DIM), lambda i, g: (i, 0, g))
    q_spec = pl.BlockSpec((None, s, gw), lambda i, g: (i, 0, g))
    return pl.pallas_call(
        functools.partial(_attn_prompt_kernel, nb=nb, gt=gt),
        grid=(b, N_KV_HEADS),
        in_specs=[pl.BlockSpec(memory_space=pltpu.SMEM), q_spec, kv_spec, kv_spec],
        out_specs=q_spec,
        out_shape=jax.ShapeDtypeStruct(q.shape, BF16),
        scratch_shapes=[pltpu.VMEM((nb, MOBA_BLOCK, HEAD_DIM), BF16), pltpu.VMEM((nb, V_ROWS, MOBA_BLOCK), BF16),
                        pltpu.VMEM((nb // gt, gt * MOBA_BLOCK, HEAD_DIM), BF16),
                        pltpu.VMEM((nb // gt, V_ROWS, gt * MOBA_BLOCK), BF16),
                        pltpu.VMEM((nb, HEAD_DIM), F32), pltpu.VMEM((nb, MOBA_BLOCK, gw), BF16),
                        pltpu.VMEM((GROUP, nb, nb, MOBA_BLOCK), F32), pltpu.VMEM((GROUP, MOBA_BLOCK, MOBA_BLOCK), F32),
                        pltpu.VMEM((GROUP, V_ROWS, MOBA_BLOCK), F32)],
        compiler_params=_params(2),
        name="attn_prompt",
    )(slopes, q, k, v)


def _row_slopes(head):
    out = jnp.zeros(head.shape, F32)
    for h, sl in enumerate(ALIBI_SLOPES):
        out = jnp.where(head == h, sl, out)
    return out


def _attn_sample_kernel(pt_ref, q_ref, kn_ref, vn_ref, *rest, pp, n_blk, past, n_new):
    k_pages, v_pages = rest[:pp], rest[pp:2 * pp]
    o_ref = rest[2 * pp]
    km_s, m_s, l_s, op_s, bias_s = rest[2 * pp + 1:]
    del pt_ref
    s_id = pl.program_id(1)
    nq = q_ref.shape[0]
    kv = N_KV_HEADS
    cols = MOBA_BLOCK * kv
    q = q_ref[...]
    qb = (q * (ATTN_SCALE * LOG2E)).astype(BF16)

    row = lax.broadcasted_iota(jnp.int32, (nq, cols), 0)
    col = lax.broadcasted_iota(jnp.int32, (nq, cols), 1)
    head = row % N_HEADS
    step = row // N_HEADS

    @pl.when(s_id == 0)
    def _():
        same_kv = (col % kv) == head // GROUP
        back = (step + MOBA_BLOCK - col // kv).astype(F32)
        bias_s[...] = jnp.where(same_kv, -(_row_slopes(head) * LOG2E) * back, NEG)
        m_s[...] = jnp.zeros_like(m_s)
        l_s[...] = jnp.zeros_like(l_s)

    slope_col = _row_slopes(head[:, 0:1]) * LOG2E
    lane = lax.broadcasted_iota(jnp.int32, (nq, 128), 1)
    per_step = pp // 2
    first_blk = s_id * per_step
    ksums, scores, probs, partials = [], [], [], []
    for i in range(per_step):
        k0, k1 = k_pages[2 * i][...], k_pages[2 * i + 1][...]
        ksums.append(jnp.sum(k0.reshape(-1, SUBLANES, HEAD_DIM), axis=0)
                     + jnp.sum(k1.reshape(-1, SUBLANES, HEAD_DIM), axis=0))
        kb = jnp.concatenate([k0, k1], axis=0).astype(BF16)
        scores.append(_dot_nt(qb, kb) + bias_s[...])
    for s in scores:
        m = jnp.max(s, axis=-1, keepdims=True)
        p = jnp.exp2(s - m)
        probs.append((m, jnp.sum(p, axis=-1, keepdims=True), p.astype(BF16)))
    for i, (m, l, p) in enumerate(probs):
        vb = jnp.concatenate([v_pages[2 * i][...], v_pages[2 * i + 1][...]], axis=0).astype(BF16)
        partials.append((m, l, _dot(p, vb)))
    m_new, l_new = m_s[...], l_s[...]
    for i, (m, l, o_part) in enumerate(partials):
        n = first_blk + i
        km_s[pl.ds(pl.multiple_of(n * SUBLANES, SUBLANES), SUBLANES), :] = ksums[i] * (1.0 / MOBA_BLOCK)
        op_s[n] = o_part
        m_new = jnp.where(lane == n, m + slope_col * ((n + 1) * MOBA_BLOCK - past).astype(F32), m_new)
        l_new = jnp.where(lane == n, l, l_new)
    m_s[...] = m_new
    l_s[...] = l_new

    @pl.when(s_id == pl.num_programs(1) - 1)
    def _():
        gcols = n_blk * SUBLANES
        gfull = _dot_nt(q, km_s[...], precision=lax.Precision.HIGHEST)
        gsum = gfull + pltpu.roll(gfull, gcols - kv, 1)
        grow = lax.broadcasted_iota(jnp.int32, (nq, gcols), 0)
        gcol = lax.broadcasted_iota(jnp.int32, (nq, gcols), 1)
        gs = jnp.where((gcol % SUBLANES) == (grow % N_HEADS) // GROUP, gsum, -jnp.inf)
        gcol_f = gcol.astype(F32)
        chosen = jnp.zeros((nq, 128), F32)
        for _ in range(MOBA_TOPK):
            best = jnp.max(gs, axis=-1, keepdims=True)
            first = jnp.min(jnp.where(gs == best, gcol_f, float(gcols)), axis=-1, keepdims=True)
            chosen = jnp.where(lane == (first.astype(jnp.int32) // SUBLANES), 1.0, chosen)
            gs = jnp.where(gcol_f == first, -jnp.inf, gs)
        is_chosen = chosen > 0.0

        nk = n_new * kv
        orow = lax.broadcasted_iota(jnp.int32, (nq, nk), 0)
        ocol = lax.broadcasted_iota(jnp.int32, (nq, nk), 1)
        ohead, ostep = orow % N_HEADS, orow // N_HEADS
        ok = jnp.logical_and((ocol % kv) == ohead // GROUP, ocol // kv <= ostep)
        so = _dot_nt(qb, kn_ref[...].astype(BF16)) - (_row_slopes(ohead) * LOG2E) * (ostep - ocol // kv).astype(F32)
        so = jnp.where(ok, so, NEG)

        m_all = jnp.maximum(jnp.max(so, axis=-1, keepdims=True),
                            jnp.max(jnp.where(is_chosen, m_s[...], NEG), axis=-1, keepdims=True))
        wgt = jnp.where(is_chosen, jnp.exp2(m_s[...] - m_all), 0.0)
        po = jnp.exp2(so - m_all)
        l_all = jnp.sum(wgt * l_s[...], axis=-1, keepdims=True) + jnp.sum(po, axis=-1, keepdims=True)
        acc = _dot(po.astype(BF16), vn_ref[...].astype(BF16))
        for n in range(n_blk):
            acc = acc + wgt[:, n:n + 1] * op_s[n]
        o_ref[...] = acc / l_all


def _attn_sample(q, k_new, v_new, cache_k, cache_v, page_ids, n_new):
    b, nq, _ = q.shape
    n_pages = page_ids.shape[1]
    past = n_pages * PAGE_SIZE
    pp = PAGES_PER_STEP
    assert MOBA_BLOCK == 2 * PAGE_SIZE and past % MOBA_BLOCK == 0 and n_pages % pp == 0
    n_blk = past // MOBA_BLOCK
    assert MOBA_TOPK <= n_blk <= 128 and n_new <= MOBA_BLOCK
    rows = PAGE_SIZE * N_KV_HEADS

    def page(i):
        return pl.BlockSpec((None, rows, HEAD_DIM), lambda bi, si, pt: (pt[bi * n_pages + si * pp + i], 0, 0))

    def per_seq(r):
        return pl.BlockSpec((None, r, HEAD_DIM), lambda bi, si, pt: (bi, 0, 0))

    return pl.pallas_call(
        functools.partial(_attn_sample_kernel, pp=pp, n_blk=n_blk, past=past, n_new=n_new),
        grid_spec=pltpu.PrefetchScalarGridSpec(
            num_scalar_prefetch=1,
            grid=(b, n_pages // pp),
            in_specs=[per_seq(nq), per_seq(n_new * N_KV_HEADS), per_seq(n_new * N_KV_HEADS)]
            + [page(i) for i in range(pp)] * 2,
            out_specs=per_seq(nq),
            scratch_shapes=[pltpu.VMEM((n_blk * SUBLANES, HEAD_DIM), F32), pltpu.VMEM((nq, 128), F32),
                            pltpu.VMEM((nq, 128), F32), pltpu.VMEM((n_blk, nq, HEAD_DIM), F32),
                            pltpu.VMEM((nq, MOBA_BLOCK * N_KV_HEADS), F32)]),
        out_shape=jax.ShapeDtypeStruct((b, nq, HEAD_DIM), F32),
        compiler_params=_params(2),
        name="attn_sample",
    )(page_ids.reshape(-1), q, k_new, v_new, *([cache_k] * pp), *([cache_v] * pp))


def _merge_ffn_kernel(x_ref, ya_ref, o_ref_in, sgb_ref, g2_ref, sh_ref, sc_ref, g3_ref, ng_ref,
                      wbb_ref, wout_ref, wg_ref, wu_ref, wd_ref, out_ref, *, chunk):
    yb = _dot(o_ref_in[...].astype(BF16), wbb_ref[...])
    merged = (ya_ref[...] + sgb_ref[...] * yb).astype(BF16)
    x = x_ref[...] + g2_ref[...] * _dot(merged, wout_ref[...])
    h = (_rms(x, ng_ref[...]) * (1.0 + sc_ref[...]) + sh_ref[...]).astype(BF16)
    out_ref[...] = x + 0.5 * g3_ref[...] * _swiglu_acc(h, wg_ref, wu_ref, wd_ref, chunk)


def _merge_ffn(x, ya, o, sgb, g2, sh, sc, g3, ng, wbb, wout, wg, wu, wd, tm, tiles_per_group):
    t, d = x.shape
    row = pl.BlockSpec((tm, d), lambda i: (i, 0))
    mod = functools.partial(_mod_spec, tm=tm, tiles_per_group=tiles_per_group)
    return pl.pallas_call(
        functools.partial(_merge_ffn_kernel, chunk=_ffn_chunk(wg.shape[1])),
        grid=(t // tm,),
        in_specs=[row, row, pl.BlockSpec((tm, o.shape[1]), lambda i: (i, 0)), row,
                  mod(g2), mod(sh), mod(sc), mod(g3), _resident((1, d)),
                  _resident(wbb.shape), _resident(wout.shape), _resident(wg.shape), _resident(wu.shape),
                  _resident(wd.shape)],
        out_specs=row,
        out_shape=jax.ShapeDtypeStruct((t, d), F32),
        compiler_params=_params(1),
        name="merge_ffn",
    )(x, ya, o, sgb, g2, sh, sc, g3, ng, wbb, wout, wg, wu, wd)


def _token_tile(t):
    return 512 if t % 512 == 0 else t


def _layer_weights(p):
    w = dict(p)
    for name in ("ffn1_gate", "ffn1_up", "ffn1_down", "w_in", "w_branch_a", "w_branch_b", "w_out",
                 "ffn2_gate", "ffn2_up", "ffn2_down"):
        w[name] = p[name].astype(BF16)
    w["rg_w"] = jnp.concatenate([p["rg_wa"], p["rg_wx"]], axis=-1).astype(BF16)
    for name in ("norm1_g", "norm2_g", "norm3_g", "conv_b", "rg_ba", "rg_bx", "rg_lambda", "q_norm_g", "k_norm_g"):
        w[name] = p[name].reshape(1, -1)
    return w


def _prompt_layer(x, mod, w):
    b, s, d = x.shape
    t = b * s
    tm = _token_tile(s)
    tpg = s // tm
    sh1, sc1, g1, sh2, sc2, g2, sh3, sc3, g3 = [m[:, None, :] for m in jnp.split(mod, 3 * N_SUBLAYERS, axis=-1)]
    x1 = _ffn(x.reshape(t, d), sh1, sc1, g1, w["norm1_g"], w["ffn1_gate"], w["ffn1_up"], w["ffn1_down"], tm, tpg)
    tm_in = min(tm, 512)
    xr, ug, q, k, v, sga, sgb = _inproj(x1, sh2, sc2, w["norm2_g"], w["w_in"], w["q_norm_g"], w["k_norm_g"],
                                        tm_in, s // tm_in)
    d_rnn = xr.shape[1]
    ts = min(s, 256)
    ya, new_conv, h_last = _rnn_prompt(
        xr.reshape(b, s, d_rnn), ug.reshape(b, s, d_rnn), sga.reshape(b, s, d),
        jnp.zeros((b, CONV_WIDTH - 1, d_rnn), F32), jnp.zeros((b, 1, d_rnn), F32),
        w["conv_w"], w["conv_b"], w["rg_w"], w["rg_ba"], w["rg_bx"], w["rg_lambda"], w["w_branch_a"], ts)
    o = _attn_prompt(q.reshape(b, s, -1), k.reshape(b, s, -1), v.reshape(b, s, -1))
    y = _merge_ffn(x1, ya.reshape(t, d), o.reshape(t, -1), sgb, g2, sh3, sc3, g3, w["norm3_g"],
                   w["w_branch_b"], w["w_out"], w["ffn2_gate"], w["ffn2_up"], w["ffn2_down"], tm, tpg)
    return (y.reshape(b, s, d), k.reshape(b, s, N_KV_HEADS, HEAD_DIM), v.reshape(b, s, N_KV_HEADS, HEAD_DIM),
            new_conv, h_last.reshape(b, d_rnn))


def _sample_layer(x, mod, conv_buf, h0, cache_k, cache_v, page_ids, w):
    b, s, d = x.shape
    t = b * s
    rep = lambda m: jnp.repeat(m, s, axis=0)[None]
    sh1, sc1, g1, sh2, sc2, g2, sh3, sc3, g3 = [rep(m) for m in jnp.split(mod, 3 * N_SUBLAYERS, axis=-1)]
    x1 = _ffn(x.reshape(t, d), sh1, sc1, g1, w["norm1_g"], w["ffn1_gate"], w["ffn1_up"], w["ffn1_down"], t, 1)
    xr, ug, q, k, v, sga, sgb = _inproj(x1, sh2, sc2, w["norm2_g"], w["w_in"], w["q_norm_g"], w["k_norm_g"], t, 1)
    d_rnn = xr.shape[1]
    xp = jnp.concatenate([conv_buf, xr.reshape(b, s, d_rnn)], axis=1)
    taps = jnp.stack([xp[:, kk:kk + s].reshape(t, d_rnn) for kk in range(CONV_WIDTH)])
    ya, h_all = _rnn_sample(taps, ug, sga, jnp.repeat(h0, s, axis=0), w["conv_w"], w["conv_b"], w["rg_w"],
                            w["rg_ba"], w["rg_bx"], w["rg_lambda"], w["w_branch_a"], s)
    o = _attn_sample(q.reshape(b, s * N_HEADS, HEAD_DIM), k.reshape(b, s * N_KV_HEADS, HEAD_DIM),
                     v.reshape(b, s * N_KV_HEADS, HEAD_DIM), cache_k, cache_v, page_ids, s)
    y = _merge_ffn(x1, ya, o.reshape(t, -1), sgb, g2, sh3, sc3, g3, w["norm3_g"],
                   w["w_branch_b"], w["w_out"], w["ffn2_gate"], w["ffn2_up"], w["ffn2_down"], t, 1)
    return (y.reshape(b, s, d), k.reshape(b, s, N_KV_HEADS, HEAD_DIM), v.reshape(b, s, N_KV_HEADS, HEAD_DIM),
            xp[:, -(CONV_WIDTH - 1):], h_all.reshape(b, s, d_rnn)[:, -1])


def kernel(x_prompt, x_sample, c_prompt, c_sample, cache_k, cache_v, state_conv, state_rglru, page_table, w_ada, b_ada, norm1_g, ffn1_gate, ffn1_up, ffn1_down, norm2_g, w_in, conv_w, conv_b, rg_wa, rg_ba, rg_wx, rg_bx, rg_lambda, q_norm_g, k_norm_g, w_branch_a, w_branch_b, w_out, norm3_g, ffn2_gate, ffn2_up, ffn2_down):
    depth, n_pool = cache_k.shape[0], cache_k.shape[1]
    bp = x_prompt.shape[0]
    assert cache_k.shape[2:] == (PAGE_SIZE, N_KV_HEADS, HEAD_DIM)
    ck = cache_k.reshape(depth * n_pool, PAGE_SIZE * N_KV_HEADS, HEAD_DIM)
    cv = cache_v.reshape(depth * n_pool, PAGE_SIZE * N_KV_HEADS, HEAD_DIM)
    c_all = jnp.concatenate([c_prompt, c_sample], axis=0)
    pad = -c_all.shape[0] % SUBLANES
    c_all = jnp.pad(c_all, ((0, pad), (0, 0)))
    yp, ys = x_prompt, x_sample
    outs = [[] for _ in range(8)]
    for l in range(depth):
        p = dict(norm1_g=norm1_g[l], ffn1_gate=ffn1_gate[l], ffn1_up=ffn1_up[l], ffn1_down=ffn1_down[l],
                 norm2_g=norm2_g[l], w_in=w_in[l], conv_w=conv_w[l], conv_b=conv_b[l], rg_wa=rg_wa[l],
                 rg_ba=rg_ba[l], rg_wx=rg_wx[l], rg_bx=rg_bx[l], rg_lambda=rg_lambda[l], q_norm_g=q_norm_g[l],
                 k_norm_g=k_norm_g[l], w_branch_a=w_branch_a[l], w_branch_b=w_branch_b[l], w_out=w_out[l],
                 norm3_g=norm3_g[l], ffn2_gate=ffn2_gate[l], ffn2_up=ffn2_up[l], ffn2_down=ffn2_down[l])
        w = _layer_weights(p)
        mod = _ada(c_all, w_ada[l], b_ada[l])
        yp, kp, vp, cp, hp = _prompt_layer(yp, mod[:bp], w)
        ys, ks, vs, cs, hs = _sample_layer(ys, mod[bp:bp + x_sample.shape[0]], state_conv[l], state_rglru[l],
                                           ck, cv, page_table + l * n_pool, w)
        for lst, val in zip(outs, (kp, vp, cp, hp, ks, vs, cs, hs)):
            lst.append(val)
    return (yp, ys) + tuple(jnp.stack(o) for o in outs)
```

```python
import functools

import jax
import jax.numpy as jnp
from jax import lax
from jax.experimental import pallas as pl
from jax.experimental.pallas import tpu as pltpu

F32 = jnp.float32
BF16 = jnp.bfloat16

N_HEADS = 8
N_KV_HEADS = 4
HEAD_DIM = 128
GROUP = N_HEADS // N_KV_HEADS
MOBA_BLOCK = 256
MOBA_TOPK = 3
PAGE_SIZE = 128
N_RNN_BLOCKS = 8
RNN_BLOCK = 128
CONV_WIDTH = 4
RG_C = 8.0
N_SUBLAYERS = 3
EPS = 1e-6
ALIBI_SLOPES = tuple(2.0 ** (-8.0 * (h + 1) / N_HEADS) for h in range(N_HEADS))
ATTN_SCALE = HEAD_DIM ** -0.5
NEG = -1e30
LOG2E = 1.4426950408889634
SUBLANES = 8
VMEM_LIMIT = 56 * 1024 * 1024
PAGES_PER_STEP = 8
KV_BLOCKS_PER_STEP = 4


def _params(n_axes):
    return pltpu.CompilerParams(dimension_semantics=("arbitrary",) * n_axes, vmem_limit_bytes=VMEM_LIMIT)


def _resident(shape):
    nd = len(shape)
    return pl.BlockSpec(shape, lambda *_: (0,) * nd, pipeline_mode=pl.Buffered(1))


def _dot(a, b):
    return jnp.dot(a, b, preferred_element_type=F32)


def _dot_nt(a, b, precision=None):
    return lax.dot_general(a, b, (((1,), (1,)), ((), ())), precision=precision, preferred_element_type=F32)


def _rms(x, g):
    return x * lax.rsqrt(jnp.mean(x * x, axis=-1, keepdims=True) + EPS) * g


def _sigmoid(x):
    return 0.5 * jnp.tanh(0.5 * x) + 0.5


def _silu(x):
    return x * _sigmoid(x)


def _ada_kernel(c_ref, w_ref, b_ref, o_ref):
    a = _silu(c_ref[...]).astype(BF16)
    o_ref[...] = _dot(a, w_ref[...].astype(BF16)) + b_ref[...]


def _ada(c, w, b):
    m, d = c.shape
    n = w.shape[1]
    tn = n // 8
    return pl.pallas_call(
        _ada_kernel,
        grid=(n // tn,),
        in_specs=[pl.BlockSpec((m, d), lambda i: (0, 0)),
                  pl.BlockSpec((d, tn), lambda i: (0, i)),
                  pl.BlockSpec((1, tn), lambda i: (0, i))],
        out_specs=pl.BlockSpec((m, tn), lambda i: (0, i)),
        out_shape=jax.ShapeDtypeStruct((m, n), F32),
        compiler_params=_params(1),
        name="ada",
    )(c, w, b.reshape(1, n))


def _swiglu_acc(h, wg_ref, wu_ref, wd_ref, chunk):
    d_ff = wg_ref.shape[1]
    acc = None
    for c in range(d_ff // chunk):
        sl = slice(c * chunk, (c + 1) * chunk)
        act = (_silu(_dot(h, wg_ref[:, sl])) * _dot(h, wu_ref[:, sl])).astype(BF16)
        part = _dot(act, wd_ref[sl, :])
        acc = part if acc is None else acc + part
    return acc


def _ffn_kernel(x_ref, sh_ref, sc_ref, g_ref, ng_ref, wg_ref, wu_ref, wd_ref, o_ref, *, chunk):
    x = x_ref[...]
    h = (_rms(x, ng_ref[...]) * (1.0 + sc_ref[...]) + sh_ref[...]).astype(BF16)
    o_ref[...] = x + 0.5 * g_ref[...] * _swiglu_acc(h, wg_ref, wu_ref, wd_ref, chunk)


def _mod_spec(mod, tm, tiles_per_group):
    r, d = mod.shape[1], mod.shape[2]
    return pl.BlockSpec((None, r, d), lambda i: (i // tiles_per_group, 0, 0))


def _ffn_chunk(d_ff):
    return 256 if d_ff % 256 == 0 else 128


def _ffn(x, sh, sc, g, ng, wg, wu, wd, tm, tiles_per_group):
    t, d = x.shape
    d_ff = wg.shape[1]
    row = pl.BlockSpec((tm, d), lambda i: (i, 0))
    return pl.pallas_call(
        functools.partial(_ffn_kernel, chunk=_ffn_chunk(d_ff)),
        grid=(t // tm,),
        in_specs=[row, _mod_spec(sh, tm, tiles_per_group), _mod_spec(sc, tm, tiles_per_group),
                  _mod_spec(g, tm, tiles_per_group), _resident((1, d)),
                  _resident(wg.shape), _resident(wu.shape), _resident(wd.shape)],
        out_specs=row,
        out_shape=jax.ShapeDtypeStruct((t, d), F32),
        compiler_params=_params(1),
        name="ffn",
    )(x, sh, sc, g, ng, wg, wu, wd)


def _inproj_kernel(x_ref, sh_ref, sc_ref, ng_ref, w_ref, qg_ref, kg_ref,
                   xr_ref, ug_ref, q_ref, k_ref, v_ref, sga_ref, sgb_ref):
    d_rnn = xr_ref.shape[1]
    dq = q_ref.shape[1]
    dk = k_ref.shape[1]
    x = x_ref[...]
    h = (_rms(x, ng_ref[...]) * (1.0 + sc_ref[...]) + sh_ref[...]).astype(BF16)
    o = 0
    xr_ref[...] = _dot(h, w_ref[:, o:o + d_rnn])
    o += d_rnn
    ug_ref[...] = jax.nn.gelu(_dot(h, w_ref[:, o:o + d_rnn]))
    o += d_rnn
    for hd in range(dq // HEAD_DIM):
        sl = slice(hd * HEAD_DIM, (hd + 1) * HEAD_DIM)
        q_ref[:, sl] = _rms(_dot(h, w_ref[:, o + hd * HEAD_DIM:o + (hd + 1) * HEAD_DIM]), qg_ref[...])
    o += dq
    for hd in range(dk // HEAD_DIM):
        sl = slice(hd * HEAD_DIM, (hd + 1) * HEAD_DIM)
        k_ref[:, sl] = _rms(_dot(h, w_ref[:, o + hd * HEAD_DIM:o + (hd + 1) * HEAD_DIM]), kg_ref[...])
    o += dk
    v_ref[...] = _dot(h, w_ref[:, o:o + dk])
    o += dk
    d = sga_ref.shape[1]
    sga_ref[...] = _sigmoid(_dot(h, w_ref[:, o:o + d]))
    o += d
    sgb_ref[...] = _sigmoid(_dot(h, w_ref[:, o:o + d]))


def _inproj(x, sh, sc, ng, w_in, qg, kg, tm, tiles_per_group):
    t, d = x.shape
    dq, dk = N_HEADS * HEAD_DIM, N_KV_HEADS * HEAD_DIM
    d_rnn = (w_in.shape[1] - dq - 2 * dk - 2 * d) // 2

    def row(w):
        return pl.BlockSpec((tm, w), lambda i: (i, 0))

    widths = (d_rnn, d_rnn, dq, dk, dk, d, d)
    return pl.pallas_call(
        _inproj_kernel,
        grid=(t // tm,),
        in_specs=[row(d), _mod_spec(sh, tm, tiles_per_group), _mod_spec(sc, tm, tiles_per_group),
                  _resident((1, d)), _resident(w_in.shape), _resident((1, HEAD_DIM)), _resident((1, HEAD_DIM))],
        out_specs=[row(w) for w in widths],
        out_shape=[jax.ShapeDtypeStruct((t, w), F32) for w in widths],
        compiler_params=_params(1),
        name="inproj",
    )(x, sh, sc, ng, w_in, qg, kg)


def _rg_gates(xc, wg_ref, bra_ref, brx_ref, lam_ref, a_ref, b_ref):
    sp = jax.nn.softplus(-lam_ref[...])
    for n in range(N_RNN_BLOCKS):
        sl = slice(n * RNN_BLOCK, (n + 1) * RNN_BLOCK)
        xcn = xc[:, sl]
        z = _dot(xcn.astype(BF16), wg_ref[n])
        r = _sigmoid(z[:, :RNN_BLOCK] + bra_ref[:, sl])
        i = _sigmoid(z[:, RNN_BLOCK:] + brx_ref[:, sl])
        log_a = -RG_C * r * sp[:, sl]
        a = jnp.exp(log_a)
        a_ref[:, sl] = a
        b_ref[:, sl] = jnp.sqrt(1.0 - a * a) * (i * xcn)


def _scan_rows(a, b, row_in_seg, steps):
    for d in steps:
        keep = row_in_seg >= d
        a_prev = jnp.where(keep, pltpu.roll(a, d, 0), 1.0)
        b_prev = jnp.where(keep, pltpu.roll(b, d, 0), 0.0)
        b = a * b_prev + b
        a = a * a_prev
    return a, b


def _rnn_prompt_kernel(xr_ref, ug_ref, sga_ref, conv0_ref, h0_ref, cw_ref, cb_ref, wg_ref, bra_ref, brx_ref,
                       lam_ref, wba_ref, ya_ref, nconv_ref, hlast_ref, xbuf, a_s, b_s, hc_s, *, ts):
    s = pl.program_id(1)
    w = xr_ref.shape[1]
    tail = CONV_WIDTH - 1
    lo = SUBLANES - tail

    @pl.when(s == 0)
    def _():
        xbuf[lo:SUBLANES, :] = conv0_ref[...]
        hc_s[...] = jnp.broadcast_to(h0_ref[...], hc_s.shape)

    xbuf[SUBLANES:SUBLANES + ts, :] = xr_ref[...]
    xc = cb_ref[...] + cw_ref[0:1, :] * xbuf[lo:lo + ts, :]
    for k in range(1, CONV_WIDTH):
        xc = xc + cw_ref[k:k + 1, :] * xbuf[lo + k:lo + k + ts, :]
    new_tail = xbuf[ts + lo:ts + SUBLANES, :]
    nconv_ref[...] = new_tail
    xbuf[lo:SUBLANES, :] = new_tail
    _rg_gates(xc, wg_ref, bra_ref, brx_ref, lam_ref, a_s, b_s)

    row = lax.broadcasted_iota(jnp.int32, (SUBLANES, w), 0)

    def body(c, hc):
        off = pl.multiple_of(c * SUBLANES, SUBLANES)
        a_cum, b_cum = _scan_rows(a_s[pl.ds(off, SUBLANES), :], b_s[pl.ds(off, SUBLANES), :], row, (1, 2, 4))
        h = a_cum * hc + b_cum
        a_s[pl.ds(off, SUBLANES), :] = h
        return jnp.broadcast_to(h[SUBLANES - 1:SUBLANES, :], (SUBLANES, w))

    hc = lax.fori_loop(0, ts // SUBLANES, body, hc_s[...])
    hc_s[...] = hc
    hlast_ref[...] = hc[0:1, :]
    u = (a_s[...] * ug_ref[...]).astype(BF16)
    ya_ref[...] = sga_ref[...] * _dot(u, wba_ref[...])


def _rnn_prompt(xr, ug, sga, conv0, h0, cw, cb, wg, bra, brx, lam, wba, ts):
    b, s, w = xr.shape
    d = wba.shape[1]
    tail = CONV_WIDTH - 1
    seq = pl.BlockSpec((None, ts, w), lambda i, j: (i, j, 0))
    return pl.pallas_call(
        functools.partial(_rnn_prompt_kernel, ts=ts),
        grid=(b, s // ts),
        in_specs=[seq, seq, pl.BlockSpec((None, ts, d), lambda i, j: (i, j, 0)),
                  pl.BlockSpec((None, tail, w), lambda i, j: (i, 0, 0)),
                  pl.BlockSpec((None, 1, w), lambda i, j: (i, 0, 0)),
                  _resident(cw.shape), _resident((1, w)), _resident(wg.shape), _resident((1, w)), _resident((1, w)),
                  _resident((1, w)), _resident(wba.shape)],
        out_specs=[pl.BlockSpec((None, ts, d), lambda i, j: (i, j, 0)),
                   pl.BlockSpec((None, tail, w), lambda i, j: (i, 0, 0)),
                   pl.BlockSpec((None, 1, w), lambda i, j: (i, 0, 0))],
        out_shape=[jax.ShapeDtypeStruct((b, s, d), F32), jax.ShapeDtypeStruct((b, tail, w), F32),
                   jax.ShapeDtypeStruct((b, 1, w), F32)],
        scratch_shapes=[pltpu.VMEM((ts + SUBLANES, w), F32), pltpu.VMEM((ts, w), F32), pltpu.VMEM((ts, w), F32),
                        pltpu.VMEM((SUBLANES, w), F32)],
        compiler_params=_params(2),
        name="rnn_prompt",
    )(xr, ug, sga, conv0, h0, cw, cb, wg, bra, brx, lam, wba)


def _rnn_sample_kernel(xs_ref, ug_ref, sga_ref, h0_ref, cw_ref, cb_ref, wg_ref, bra_ref, brx_ref, lam_ref, wba_ref,
                       ya_ref, h_ref, a_s, b_s, *, seg):
    m, w = ug_ref.shape
    xc = cb_ref[...] + cw_ref[0:1, :] * xs_ref[0]
    for k in range(1, CONV_WIDTH):
        xc = xc + cw_ref[k:k + 1, :] * xs_ref[k]
    _rg_gates(xc, wg_ref, bra_ref, brx_ref, lam_ref, a_s, b_s)
    row_in_seg = lax.broadcasted_iota(jnp.int32, (SUBLANES, w), 0) % seg
    steps = tuple(d for d in (1, 2, 4) if d < seg)
    for c in range(m // SUBLANES):
        rows = slice(c * SUBLANES, (c + 1) * SUBLANES)
        a_cum, b_cum = _scan_rows(a_s[rows, :], b_s[rows, :], row_in_seg, steps)
        h_ref[rows, :] = a_cum * h0_ref[rows, :] + b_cum
    u = (h_ref[...] * ug_ref[...]).astype(BF16)
    ya_ref[...] = sga_ref[...] * _dot(u, wba_ref[...])


def _rnn_sample(xs, ug, sga, h0_rows, cw, cb, wg, bra, brx, lam, wba, seg):
    m, w = ug.shape
    d = wba.shape[1]
    return pl.pallas_call(
        functools.partial(_rnn_sample_kernel, seg=seg),
        out_shape=[jax.ShapeDtypeStruct((m, d), F32), jax.ShapeDtypeStruct((m, w), F32)],
        scratch_shapes=[pltpu.VMEM((m, w), F32), pltpu.VMEM((m, w), F32)],
        compiler_params=pltpu.CompilerParams(vmem_limit_bytes=VMEM_LIMIT),
        name="rnn_sample",
    )(xs, ug, sga, h0_rows, cw, cb, wg, bra, brx, lam, wba)


def _row_slopes(head):
    out = jnp.zeros(head.shape, F32)
    for h, sl in enumerate(ALIBI_SLOPES):
        out = jnp.where(head == h, sl, out)
    return out


def _sample_step(step, sq_ref, kn_ref, vn_ref, o_ref, k_pages, v_pages, km_s, m_s, l_s, op_s, bias_s, *,
                 pp, n_blk, past, n_new, steps_per_seq):
    seq = step // steps_per_seq
    s_id = step % steps_per_seq
    nq = sq_ref.shape[1]
    kv = N_KV_HEADS
    cols = MOBA_BLOCK * kv
    q = sq_ref[seq]
    qb = (q * (ATTN_SCALE * LOG2E)).astype(BF16)

    row = lax.broadcasted_iota(jnp.int32, (nq, cols), 0)
    col = lax.broadcasted_iota(jnp.int32, (nq, cols), 1)
    head = row % N_HEADS
    new_step = row // N_HEADS

    @pl.when(s_id == 0)
    def _():
        same_kv = (col % kv) == head // GROUP
        back = (new_step + MOBA_BLOCK - col // kv).astype(F32)
        bias_s[...] = jnp.where(same_kv, -(_row_slopes(head) * LOG2E) * back, NEG)
        m_s[...] = jnp.zeros_like(m_s)
        l_s[...] = jnp.zeros_like(l_s)

    slope_col = _row_slopes(head[:, 0:1]) * LOG2E
    lane = lax.broadcasted_iota(jnp.int32, (nq, 128), 1)
    per_step = pp // 2
    first_blk = s_id * per_step
    ksums, scores, probs, partials = [], [], [], []
    for i in range(per_step):
        k0, k1 = k_pages[2 * i], k_pages[2 * i + 1]
        ksums.append(jnp.sum(k0.reshape(-1, SUBLANES, HEAD_DIM), axis=0)
                     + jnp.sum(k1.reshape(-1, SUBLANES, HEAD_DIM), axis=0))
        kb = jnp.concatenate([k0, k1], axis=0).astype(BF16)
        scores.append(_dot_nt(qb, kb) + bias_s[...])
    for s in scores:
        m = jnp.max(s, axis=-1, keepdims=True)
        p = jnp.exp2(s - m)
        probs.append((m, jnp.sum(p, axis=-1, keepdims=True), p.astype(BF16)))
    for i, (m, l, p) in enumerate(probs):
        vb = jnp.concatenate([v_pages[2 * i], v_pages[2 * i + 1]], axis=0).astype(BF16)
        partials.append((m, l, _dot(p, vb)))
    m_new, l_new = m_s[...], l_s[...]
    for i, (m, l, o_part) in enumerate(partials):
        n = first_blk + i
        km_s[pl.ds(pl.multiple_of(n * SUBLANES, SUBLANES), SUBLANES), :] = ksums[i] * (1.0 / MOBA_BLOCK)
        op_s[n] = o_part
        m_new = jnp.where(lane == n, m + slope_col * ((n + 1) * MOBA_BLOCK - past).astype(F32), m_new)
        l_new = jnp.where(lane == n, l, l_new)
    m_s[...] = m_new
    l_s[...] = l_new

    @pl.when(s_id == steps_per_seq - 1)
    def _():
        gcols = n_blk * SUBLANES
        gfull = _dot_nt(q, km_s[...], precision=lax.Precision.HIGHEST)
        gsum = gfull + pltpu.roll(gfull, gcols - kv, 1)
        grow = lax.broadcasted_iota(jnp.int32, (nq, gcols), 0)
        gcol = lax.broadcasted_iota(jnp.int32, (nq, gcols), 1)
        gs = jnp.where((gcol % SUBLANES) == (grow % N_HEADS) // GROUP, gsum, -jnp.inf)
        gcol_f = gcol.astype(F32)
        chosen = jnp.zeros((nq, 128), F32)
        for _ in range(MOBA_TOPK):
            best = jnp.max(gs, axis=-1, keepdims=True)
            first = jnp.min(jnp.where(gs == best, gcol_f, float(gcols)), axis=-1, keepdims=True)
            chosen = jnp.where(lane == (first.astype(jnp.int32) // SUBLANES), 1.0, chosen)
            gs = jnp.where(gcol_f == first, -jnp.inf, gs)
        is_chosen = chosen > 0.0

        nk = n_new * kv
        orow = lax.broadcasted_iota(jnp.int32, (nq, nk), 0)
        ocol = lax.broadcasted_iota(jnp.int32, (nq, nk), 1)
        ohead, ostep = orow % N_HEADS, orow // N_HEADS
        ok = jnp.logical_and((ocol % kv) == ohead // GROUP, ocol // kv <= ostep)
        so = _dot_nt(qb, kn_ref[seq].astype(BF16)) - (_row_slopes(ohead) * LOG2E) * (ostep - ocol // kv).astype(F32)
        so = jnp.where(ok, so, NEG)

        m_all = jnp.maximum(jnp.max(so, axis=-1, keepdims=True),
                            jnp.max(jnp.where(is_chosen, m_s[...], NEG), axis=-1, keepdims=True))
        wgt = jnp.where(is_chosen, jnp.exp2(m_s[...] - m_all), 0.0)
        po = jnp.exp2(so - m_all)
        l_all = jnp.sum(wgt * l_s[...], axis=-1, keepdims=True) + jnp.sum(po, axis=-1, keepdims=True)
        acc = _dot(po.astype(BF16), vn_ref[seq].astype(BF16))
        for n in range(n_blk):
            acc = acc + wgt[:, n:n + 1] * op_s[n]
        o_ref[seq] = acc / l_all


V_ROWS = HEAD_DIM + 16


def _attn_kernel(slope_ref, pt_ref, q_ref, k_ref, v_ref, sq_ref, kn_ref, vn_ref, ck_ref, cv_ref, o_ref, so_ref,
                 kb_s, vt_s, kbg_s, vtg_s, km_s, q2_s, bias_s, colb_s, acc_s,
                 kbuf, vbuf, sem, skm_s, sm_s, sl_s, sop_s, sbias_s, *, nb, gt, pp, n_steps, steps_per_q, guard, sample):
    g = pl.program_id(1)
    gidx = pl.program_id(0) * pl.num_programs(1) + g
    blk = MOBA_BLOCK

    def page_copies(step, slot):
        out = []
        for i in range(pp):
            page = pt_ref[step * pp + i]
            out.append(pltpu.make_async_copy(ck_ref.at[page], kbuf.at[slot, i], sem.at[slot]))
            out.append(pltpu.make_async_copy(cv_ref.at[page], vbuf.at[slot, i], sem.at[slot]))
        return out

    @pl.when(gidx == 0)
    def _():
        for c in page_copies(0, 0):
            c.start()

    def sample_step(step):
        slot = step % 2

        @pl.when(step + 1 < n_steps)
        def _():
            for c in page_copies(step + 1, 1 - slot):
                c.start()

        for c in page_copies(step, slot):
            c.wait()
        _sample_step(step, sq_ref, kn_ref, vn_ref, so_ref, [kbuf[slot, i] for i in range(pp)],
                     [vbuf[slot, i] for i in range(pp)], skm_s, sm_s, sl_s, sop_s, sbias_s, pp=pp, **sample)

    s_len = nb * blk
    q_scale = ATTN_SCALE * LOG2E

    ones_rows = (lax.broadcasted_iota(jnp.int32, (V_ROWS - HEAD_DIM, blk), 0) == 0).astype(BF16)
    for n in range(nb):
        rows = slice(n * blk, (n + 1) * blk)
        grp, sub = n // gt, slice((n % gt) * blk, (n % gt + 1) * blk)
        kn = k_ref[rows, :]
        kb = kn.astype(BF16)
        vt = v_ref[rows, :].T.astype(BF16)
        kb_s[n] = kb
        kbg_s[grp, sub, :] = kb
        vt_s[n, 0:HEAD_DIM, :] = vt
        vt_s[n, HEAD_DIM:V_ROWS, :] = ones_rows
        vtg_s[grp, 0:HEAD_DIM, sub] = vt
        vtg_s[grp, HEAD_DIM:V_ROWS, sub] = ones_rows
        km_s[n:n + 1, :] = jnp.sum(kn, axis=0, keepdims=True) * (1.0 / blk)
        q2_s[n] = (q_ref[rows, :] * q_scale).astype(BF16)

    blk_id = lax.broadcasted_iota(jnp.int32, (nb, s_len), 0)
    q_blk = lax.broadcasted_iota(jnp.int32, (nb, s_len), 1) // blk
    fully_past = blk_id < q_blk
    blocks_ahead = ((blk_id - q_blk) * blk).astype(F32)
    key_off = lax.broadcasted_iota(jnp.int32, (blk, blk), 0)
    causal = key_off <= lax.broadcasted_iota(jnp.int32, (blk, blk), 1)
    km = km_s[...]
    for hh in range(GROUP):
        slope2 = slope_ref[g * GROUP + hh] * LOG2E
        cols = slice(hh * HEAD_DIM, (hh + 1) * HEAD_DIM)
        gs = jnp.where(fully_past, _dot_nt(km, q_ref[:, cols], precision=lax.Precision.HIGHEST), -jnp.inf)
        rank = jnp.zeros((nb, s_len), jnp.int32)
        for m in range(nb):
            gm = gs[m:m + 1, :]
            tie = (blk_id > m).astype(jnp.int32)
            rank = rank + jnp.where(gm > gs, 1, jnp.where(gm == gs, tie, 0))
        chosen = jnp.logical_and(fully_past, rank < MOBA_TOPK)
        bias = jnp.where(chosen, slope2 * blocks_ahead, NEG)
        for jq in range(nb):
            bias_s[hh, jq] = bias[:, jq * blk:(jq + 1) * blk]
        colb_s[hh] = slope2 * key_off.astype(F32)

    heads = [slice(hh * HEAD_DIM, (hh + 1) * HEAD_DIM) for hh in range(GROUP)]

    def q_block(j, _):
        for i in range(steps_per_q):
            step = (gidx * nb + j) * steps_per_q + i
            if guard:
                pl.when(step < n_steps)(functools.partial(sample_step, step))
            else:
                sample_step(step)
        q2 = q2_s[j]
        kd = kb_s[j]
        vd = vt_s[j]
        s_own = [_dot_nt(kd, q2[:, cols]) for cols in heads]
        m_run = []
        for hh in range(GROUP):
            s = jnp.where(causal, s_own[hh] + colb_s[hh], NEG)
            m0 = jnp.max(s, axis=0, keepdims=True)
            acc_s[hh] = _dot(vd, jnp.exp2(s - m0).astype(BF16))
            m_run.append(m0)

        def kv_group(gi, m_run):
            kg = kbg_s[gi]
            vg = vtg_s[gi]
            s_grp = [_dot_nt(kg, q2[:, cols]) for cols in heads]
            out, probs = [], []
            for hh in range(GROUP):
                parts = [s_grp[hh][t * blk:(t + 1) * blk, :] + colb_s[hh] + bias_s[hh, j, pl.ds(gi * gt + t, 1), :]
                         for t in range(gt)]
                m_new = m_run[hh]
                for part in parts:
                    m_new = jnp.maximum(m_new, jnp.max(part, axis=0, keepdims=True))
                probs.append(jnp.concatenate([jnp.exp2(part - m_new).astype(BF16) for part in parts], axis=0))
                out.append(m_new)
            pv = [_dot(vg, p) for p in probs]
            for hh in range(GROUP):
                acc_s[hh] = jnp.exp2(m_run[hh] - out[hh]) * acc_s[hh] + pv[hh]
            return tuple(out)

        lax.fori_loop(0, (j + gt - 1) // gt, kv_group, tuple(m_run))
        rows = pl.ds(pl.multiple_of(j * blk, blk), blk)
        for hh in range(GROUP):
            acc = acc_s[hh]
            o = acc[0:HEAD_DIM, :] / acc[HEAD_DIM:HEAD_DIM + 1, :]
            o_ref[rows, heads[hh]] = o.T.astype(o_ref.dtype)
        return 0

    lax.fori_loop(0, nb, q_block, 0)


def _attention(q, k, v, sq, k_new, v_new, cache_k, cache_v, page_ids, n_new):
    b, s, _ = q.shape
    assert s % MOBA_BLOCK == 0
    nb = s // MOBA_BLOCK
    gt = KV_BLOCKS_PER_STEP if nb % KV_BLOCKS_PER_STEP == 0 else 1
    gw = GROUP * HEAD_DIM
    bs, nq, _ = sq.shape
    n_pages = page_ids.shape[1]
    past = n_pages * PAGE_SIZE
    pp = PAGES_PER_STEP
    assert MOBA_BLOCK == 2 * PAGE_SIZE and past % MOBA_BLOCK == 0 and n_pages % pp == 0
    n_blk = past // MOBA_BLOCK
    assert MOBA_TOPK <= n_blk <= 128 and n_new <= MOBA_BLOCK
    n_steps = bs * (n_pages // pp)
    n_q_blocks = b * N_KV_HEADS * nb
    steps_per_q = -(-n_steps // n_q_blocks)
    page_rows = PAGE_SIZE * N_KV_HEADS
    slopes = jnp.asarray(ALIBI_SLOPES, F32)
    kv_spec = pl.BlockSpec((None, s, HEAD_DIM), lambda i, g: (i, 0, g))
    q_spec = pl.BlockSpec((None, s, gw), lambda i, g: (i, 0, g))
    smem = pl.BlockSpec(memory_space=pltpu.SMEM)
    hbm = pl.BlockSpec(memory_space=pl.ANY)

    def whole(shape):
        return pl.BlockSpec(shape, lambda i, g: (0,) * len(shape))

    sample = dict(n_blk=n_blk, past=past, n_new=n_new, steps_per_seq=n_pages // pp)
    return pl.pallas_call(
        functools.partial(_attn_kernel, nb=nb, gt=gt, pp=pp, n_steps=n_steps, steps_per_q=steps_per_q,
                          guard=n_steps != n_q_blocks * steps_per_q, sample=sample),
        grid=(b, N_KV_HEADS),
        in_specs=[smem, smem, q_spec, kv_spec, kv_spec, whole(sq.shape), whole(k_new.shape), whole(v_new.shape),
                  hbm, hbm],
        out_specs=[q_spec, whole(sq.shape)],
        out_shape=[jax.ShapeDtypeStruct(q.shape, BF16), jax.ShapeDtypeStruct(sq.shape, F32)],
        scratch_shapes=[pltpu.VMEM((nb, MOBA_BLOCK, HEAD_DIM), BF16), pltpu.VMEM((nb, V_ROWS, MOBA_BLOCK), BF16),
                        pltpu.VMEM((nb // gt, gt * MOBA_BLOCK, HEAD_DIM), BF16),
                        pltpu.VMEM((nb // gt, V_ROWS, gt * MOBA_BLOCK), BF16),
                        pltpu.VMEM((nb, HEAD_DIM), F32), pltpu.VMEM((nb, MOBA_BLOCK, gw), BF16),
                        pltpu.VMEM((GROUP, nb, nb, MOBA_BLOCK), F32), pltpu.VMEM((GROUP, MOBA_BLOCK, MOBA_BLOCK), F32),
                        pltpu.VMEM((GROUP, V_ROWS, MOBA_BLOCK), F32),
                        pltpu.VMEM((2, pp, page_rows, HEAD_DIM), F32), pltpu.VMEM((2, pp, page_rows, HEAD_DIM), F32),
                        pltpu.SemaphoreType.DMA((2,)),
                        pltpu.VMEM((n_blk * SUBLANES, HEAD_DIM), F32), pltpu.VMEM((nq, 128), F32),
                        pltpu.VMEM((nq, 128), F32), pltpu.VMEM((n_blk, nq, HEAD_DIM), F32),
                        pltpu.VMEM((nq, MOBA_BLOCK * N_KV_HEADS), F32)],
        compiler_params=_params(2),
        name="attention",
    )(slopes, page_ids.reshape(-1), q, k, v, sq, k_new, v_new, cache_k, cache_v)


def _merge_ffn_kernel(x_ref, ya_ref, o_ref_in, sgb_ref, g2_ref, sh_ref, sc_ref, g3_ref, ng_ref,
                      wbb_ref, wout_ref, wg_ref, wu_ref, wd_ref, out_ref, *, chunk):
    yb = _dot(o_ref_in[...].astype(BF16), wbb_ref[...])
    merged = (ya_ref[...] + sgb_ref[...] * yb).astype(BF16)
    x = x_ref[...] + g2_ref[...] * _dot(merged, wout_ref[...])
    h = (_rms(x, ng_ref[...]) * (1.0 + sc_ref[...]) + sh_ref[...]).astype(BF16)
    out_ref[...] = x + 0.5 * g3_ref[...] * _swiglu_acc(h, wg_ref, wu_ref, wd_ref, chunk)


def _merge_ffn(x, ya, o, sgb, g2, sh, sc, g3, ng, wbb, wout, wg, wu, wd, tm, tiles_per_group):
    t, d = x.shape
    row = pl.BlockSpec((tm, d), lambda i: (i, 0))
    mod = functools.partial(_mod_spec, tm=tm, tiles_per_group=tiles_per_group)
    return pl.pallas_call(
        functools.partial(_merge_ffn_kernel, chunk=_ffn_chunk(wg.shape[1])),
        grid=(t // tm,),
        in_specs=[row, row, pl.BlockSpec((tm, o.shape[1]), lambda i: (i, 0)), row,
                  mod(g2), mod(sh), mod(sc), mod(g3), _resident((1, d)),
                  _resident(wbb.shape), _resident(wout.shape), _resident(wg.shape), _resident(wu.shape),
                  _resident(wd.shape)],
        out_specs=row,
        out_shape=jax.ShapeDtypeStruct((t, d), F32),
        compiler_params=_params(1),
        name="merge_ffn",
    )(x, ya, o, sgb, g2, sh, sc, g3, ng, wbb, wout, wg, wu, wd)


def _token_tile(t):
    return 512 if t % 512 == 0 else t


def _layer_weights(p):
    w = dict(p)
    for name in ("ffn1_gate", "ffn1_up", "ffn1_down", "w_in", "w_branch_a", "w_branch_b", "w_out",
                 "ffn2_gate", "ffn2_up", "ffn2_down"):
        w[name] = p[name].astype(BF16)
    w["rg_w"] = jnp.concatenate([p["rg_wa"], p["rg_wx"]], axis=-1).astype(BF16)
    for name in ("norm1_g", "norm2_g", "norm3_g", "conv_b", "rg_ba", "rg_bx", "rg_lambda", "q_norm_g", "k_norm_g"):
        w[name] = p[name].reshape(1, -1)
    return w


def _mix_inputs_prompt(x, mod, w):
    b, s, d = x.shape
    t = b * s
    tm = _token_tile(s)
    tpg = s // tm
    mods = [m[:, None, :] for m in jnp.split(mod, 3 * N_SUBLAYERS, axis=-1)]
    sh1, sc1, g1, sh2, sc2 = mods[:5]
    x1 = _ffn(x.reshape(t, d), sh1, sc1, g1, w["norm1_g"], w["ffn1_gate"], w["ffn1_up"], w["ffn1_down"], tm, tpg)
    tm_in = min(tm, 512)
    xr, ug, q, k, v, sga, sgb = _inproj(x1, sh2, sc2, w["norm2_g"], w["w_in"], w["q_norm_g"], w["k_norm_g"],
                                        tm_in, s // tm_in)
    d_rnn = xr.shape[1]
    ts = min(s, 256)
    ya, new_conv, h_last = _rnn_prompt(
        xr.reshape(b, s, d_rnn), ug.reshape(b, s, d_rnn), sga.reshape(b, s, d),
        jnp.zeros((b, CONV_WIDTH - 1, d_rnn), F32), jnp.zeros((b, 1, d_rnn), F32),
        w["conv_w"], w["conv_b"], w["rg_w"], w["rg_ba"], w["rg_bx"], w["rg_lambda"], w["w_branch_a"], ts)
    return dict(x1=x1, ya=ya.reshape(t, d), sgb=sgb, q=q, k=k, v=v, mods=mods[5:], tm=tm, tpg=tpg,
                conv=new_conv, h=h_last.reshape(b, d_rnn))


def _mix_inputs_sample(x, mod, conv_buf, h0, w):
    b, s, d = x.shape
    t = b * s
    rep = lambda m: jnp.repeat(m, s, axis=0)[None]
    mods = [rep(m) for m in jnp.split(mod, 3 * N_SUBLAYERS, axis=-1)]
    sh1, sc1, g1, sh2, sc2 = mods[:5]
    x1 = _ffn(x.reshape(t, d), sh1, sc1, g1, w["norm1_g"], w["ffn1_gate"], w["ffn1_up"], w["ffn1_down"], t, 1)
    xr, ug, q, k, v, sga, sgb = _inproj(x1, sh2, sc2, w["norm2_g"], w["w_in"], w["q_norm_g"], w["k_norm_g"], t, 1)
    d_rnn = xr.shape[1]
    xp = jnp.concatenate([conv_buf, xr.reshape(b, s, d_rnn)], axis=1)
    taps = jnp.stack([xp[:, kk:kk + s].reshape(t, d_rnn) for kk in range(CONV_WIDTH)])
    ya, h_all = _rnn_sample(taps, ug, sga, jnp.repeat(h0, s, axis=0), w["conv_w"], w["conv_b"], w["rg_w"],
                            w["rg_ba"], w["rg_bx"], w["rg_lambda"], w["w_branch_a"], s)
    return dict(x1=x1, ya=ya, sgb=sgb, q=q, k=k, v=v, mods=mods[5:], tm=t, tpg=1,
                conv=xp[:, -(CONV_WIDTH - 1):], h=h_all.reshape(b, s, d_rnn)[:, -1])


def _mix_outputs(st, o, w):
    g2, sh3, sc3, g3 = st["mods"]
    return _merge_ffn(st["x1"], st["ya"], o, st["sgb"], g2, sh3, sc3, g3, w["norm3_g"], w["w_branch_b"], w["w_out"],
                      w["ffn2_gate"], w["ffn2_up"], w["ffn2_down"], st["tm"], st["tpg"])


def _layer(xp, xs, mod_p, mod_s, conv_buf, h0, cache_k, cache_v, page_ids, w):
    bp, sp, d = xp.shape
    bs, ss, _ = xs.shape
    pr = _mix_inputs_prompt(xp, mod_p, w)
    sa = _mix_inputs_sample(xs, mod_s, conv_buf, h0, w)
    o_p, o_s = _attention(pr["q"].reshape(bp, sp, -1), pr["k"].reshape(bp, sp, -1), pr["v"].reshape(bp, sp, -1),
                          sa["q"].reshape(bs, ss * N_HEADS, HEAD_DIM), sa["k"].reshape(bs, ss * N_KV_HEADS, HEAD_DIM),
                          sa["v"].reshape(bs, ss * N_KV_HEADS, HEAD_DIM), cache_k, cache_v, page_ids, ss)
    yp = _mix_outputs(pr, o_p.reshape(bp * sp, -1), w).reshape(bp, sp, d)
    ys = _mix_outputs(sa, o_s.reshape(bs * ss, -1), w).reshape(bs, ss, d)
    heads = lambda a, b_, s_: a.reshape(b_, s_, N_KV_HEADS, HEAD_DIM)
    return (yp, ys, heads(pr["k"], bp, sp), heads(pr["v"], bp, sp), pr["conv"], pr["h"],
            heads(sa["k"], bs, ss), heads(sa["v"], bs, ss), sa["conv"], sa["h"])


def kernel(x_prompt, x_sample, c_prompt, c_sample, cache_k, cache_v, state_conv, state_rglru, page_table, w_ada, b_ada, norm1_g, ffn1_gate, ffn1_up, ffn1_down, norm2_g, w_in, conv_w, conv_b, rg_wa, rg_ba, rg_wx, rg_bx, rg_lambda, q_norm_g, k_norm_g, w_branch_a, w_branch_b, w_out, norm3_g, ffn2_gate, ffn2_up, ffn2_down):
    depth, n_pool = cache_k.shape[0], cache_k.shape[1]
    bp = x_prompt.shape[0]
    assert cache_k.shape[2:] == (PAGE_SIZE, N_KV_HEADS, HEAD_DIM)
    ck = cache_k.reshape(depth * n_pool, PAGE_SIZE * N_KV_HEADS, HEAD_DIM)
    cv = cache_v.reshape(depth * n_pool, PAGE_SIZE * N_KV_HEADS, HEAD_DIM)
    c_all = jnp.concatenate([c_prompt, c_sample], axis=0)
    pad = -c_all.shape[0] % SUBLANES
    c_all = jnp.pad(c_all, ((0, pad), (0, 0)))
    yp, ys = x_prompt, x_sample
    outs = [[] for _ in range(8)]
    for l in range(depth):
        p = dict(norm1_g=norm1_g[l], ffn1_gate=ffn1_gate[l], ffn1_up=ffn1_up[l], ffn1_down=ffn1_down[l],
                 norm2_g=norm2_g[l], w_in=w_in[l], conv_w=conv_w[l], conv_b=conv_b[l], rg_wa=rg_wa[l],
                 rg_ba=rg_ba[l], rg_wx=rg_wx[l], rg_bx=rg_bx[l], rg_lambda=rg_lambda[l], q_norm_g=q_norm_g[l],
                 k_norm_g=k_norm_g[l], w_branch_a=w_branch_a[l], w_branch_b=w_branch_b[l], w_out=w_out[l],
                 norm3_g=norm3_g[l], ffn2_gate=ffn2_gate[l], ffn2_up=ffn2_up[l], ffn2_down=ffn2_down[l])
        w = _layer_weights(p)
        mod = _ada(c_all, w_ada[l], b_ada[l])
        yp, ys, *states = _layer(yp, ys, mod[:bp], mod[bp:bp + x_sample.shape[0]], state_conv[l], state_rglru[l],
                                 ck, cv, page_table + l * n_pool, w)
        for lst, val in zip(outs, states):
            lst.append(val)
    return (yp, ys) + tuple(jnp.stack(o) for o in outs)
```

```python
import functools

import jax
import jax.numpy as jnp
from jax import lax
from jax.experimental import pallas as pl
from jax.experimental.pallas import tpu as pltpu

F32 = jnp.float32
BF16 = jnp.bfloat16

N_HEADS = 8
N_KV_HEADS = 4
HEAD_DIM = 128
GROUP = N_HEADS // N_KV_HEADS
MOBA_BLOCK = 256
MOBA_TOPK = 3
PAGE_SIZE = 128
N_RNN_BLOCKS = 8
RNN_BLOCK = 128
CONV_WIDTH = 4
RG_C = 8.0
N_SUBLAYERS = 3
EPS = 1e-6
ALIBI_SLOPES = tuple(2.0 ** (-8.0 * (h + 1) / N_HEADS) for h in range(N_HEADS))
ATTN_SCALE = HEAD_DIM ** -0.5
NEG = -1e30
LOG2E = 1.4426950408889634
SUBLANES = 8
VMEM_LIMIT = 56 * 1024 * 1024
PAGES_PER_STEP = 32
KV_BLOCKS_PER_STEP = 4


def _params(n_axes):
    return pltpu.CompilerParams(dimension_semantics=("arbitrary",) * n_axes, vmem_limit_bytes=VMEM_LIMIT)


def _resident(shape):
    nd = len(shape)
    return pl.BlockSpec(shape, lambda *_: (0,) * nd, pipeline_mode=pl.Buffered(1))


def _dot(a, b):
    return jnp.dot(a, b, preferred_element_type=F32)


def _dot_nt(a, b, precision=None):
    return lax.dot_general(a, b, (((1,), (1,)), ((), ())), precision=precision, preferred_element_type=F32)


def _rms(x, g):
    return x * lax.rsqrt(jnp.mean(x * x, axis=-1, keepdims=True) + EPS) * g


def _sigmoid(x):
    return 0.5 * jnp.tanh(0.5 * x) + 0.5


def _silu(x):
    return x * _sigmoid(x)


def _ada_kernel(c_ref, w_ref, b_ref, o_ref):
    a = _silu(c_ref[...]).astype(BF16)
    o_ref[...] = _dot(a, w_ref[...].astype(BF16)) + b_ref[...]


def _ada(c, w, b):
    m, d = c.shape
    n = w.shape[1]
    tn = n // 8
    return pl.pallas_call(
        _ada_kernel,
        grid=(n // tn,),
        in_specs=[pl.BlockSpec((m, d), lambda i: (0, 0)),
                  pl.BlockSpec((d, tn), lambda i: (0, i)),
                  pl.BlockSpec((1, tn), lambda i: (0, i))],
        out_specs=pl.BlockSpec((m, tn), lambda i: (0, i)),
        out_shape=jax.ShapeDtypeStruct((m, n), F32),
        compiler_params=_params(1),
        name="ada",
    )(c, w, b.reshape(1, n))


def _swiglu_acc(h, wg_ref, wu_ref, wd_ref, chunk):
    d_ff = wg_ref.shape[1]
    acc = None
    for c in range(d_ff // chunk):
        sl = slice(c * chunk, (c + 1) * chunk)
        act = (_silu(_dot(h, wg_ref[:, sl])) * _dot(h, wu_ref[:, sl])).astype(BF16)
        part = _dot(act, wd_ref[sl, :])
        acc = part if acc is None else acc + part
    return acc


def _ffn_kernel(x_ref, sh_ref, sc_ref, g_ref, ng_ref, wg_ref, wu_ref, wd_ref, o_ref, *, chunk):
    x = x_ref[...]
    h = (_rms(x, ng_ref[...]) * (1.0 + sc_ref[...]) + sh_ref[...]).astype(BF16)
    o_ref[...] = x + 0.5 * g_ref[...] * _swiglu_acc(h, wg_ref, wu_ref, wd_ref, chunk)


def _mod_spec(mod, tm, tiles_per_group):
    r, d = mod.shape[1], mod.shape[2]
    return pl.BlockSpec((None, r, d), lambda i: (i // tiles_per_group, 0, 0))


def _ffn_chunk(d_ff):
    return 256 if d_ff % 256 == 0 else 128


def _ffn(x, sh, sc, g, ng, wg, wu, wd, tm, tiles_per_group):
    t, d = x.shape
    d_ff = wg.shape[1]
    row = pl.BlockSpec((tm, d), lambda i: (i, 0))
    return pl.pallas_call(
        functools.partial(_ffn_kernel, chunk=_ffn_chunk(d_ff)),
        grid=(t // tm,),
        in_specs=[row, _mod_spec(sh, tm, tiles_per_group), _mod_spec(sc, tm, tiles_per_group),
                  _mod_spec(g, tm, tiles_per_group), _resident((1, d)),
                  _resident(wg.shape), _resident(wu.shape), _resident(wd.shape)],
        out_specs=row,
        out_shape=jax.ShapeDtypeStruct((t, d), F32),
        compiler_params=_params(1),
        name="ffn",
    )(x, sh, sc, g, ng, wg, wu, wd)


def _inproj_kernel(x_ref, sh_ref, sc_ref, ng_ref, w_ref, qg_ref, kg_ref,
                   xr_ref, ug_ref, q_ref, k_ref, v_ref, sga_ref, sgb_ref):
    d_rnn = xr_ref.shape[1]
    dq = q_ref.shape[1]
    dk = k_ref.shape[1]
    x = x_ref[...]
    h = (_rms(x, ng_ref[...]) * (1.0 + sc_ref[...]) + sh_ref[...]).astype(BF16)
    o = 0
    xr_ref[...] = _dot(h, w_ref[:, o:o + d_rnn])
    o += d_rnn
    ug_ref[...] = jax.nn.gelu(_dot(h, w_ref[:, o:o + d_rnn]))
    o += d_rnn
    for hd in range(dq // HEAD_DIM):
        sl = slice(hd * HEAD_DIM, (hd + 1) * HEAD_DIM)
        q_ref[:, sl] = _rms(_dot(h, w_ref[:, o + hd * HEAD_DIM:o + (hd + 1) * HEAD_DIM]), qg_ref[...])
    o += dq
    for hd in range(dk // HEAD_DIM):
        sl = slice(hd * HEAD_DIM, (hd + 1) * HEAD_DIM)
        k_ref[:, sl] = _rms(_dot(h, w_ref[:, o + hd * HEAD_DIM:o + (hd + 1) * HEAD_DIM]), kg_ref[...])
    o += dk
    v_ref[...] = _dot(h, w_ref[:, o:o + dk])
    o += dk
    d = sga_ref.shape[1]
    sga_ref[...] = _sigmoid(_dot(h, w_ref[:, o:o + d]))
    o += d
    sgb_ref[...] = _sigmoid(_dot(h, w_ref[:, o:o + d]))


def _inproj(x, sh, sc, ng, w_in, qg, kg, tm, tiles_per_group):
    t, d = x.shape
    dq, dk = N_HEADS * HEAD_DIM, N_KV_HEADS * HEAD_DIM
    d_rnn = (w_in.shape[1] - dq - 2 * dk - 2 * d) // 2

    def row(w):
        return pl.BlockSpec((tm, w), lambda i: (i, 0))

    widths = (d_rnn, d_rnn, dq, dk, dk, d, d)
    return pl.pallas_call(
        _inproj_kernel,
        grid=(t // tm,),
        in_specs=[row(d), _mod_spec(sh, tm, tiles_per_group), _mod_spec(sc, tm, tiles_per_group),
                  _resident((1, d)), _resident(w_in.shape), _resident((1, HEAD_DIM)), _resident((1, HEAD_DIM))],
        out_specs=[row(w) for w in widths],
        out_shape=[jax.ShapeDtypeStruct((t, w), F32) for w in widths],
        compiler_params=_params(1),
        name="inproj",
    )(x, sh, sc, ng, w_in, qg, kg)


def _rg_gates(xc, wg_ref, bra_ref, brx_ref, lam_ref, a_ref, b_ref):
    sp = jax.nn.softplus(-lam_ref[...])
    for n in range(N_RNN_BLOCKS):
        sl = slice(n * RNN_BLOCK, (n + 1) * RNN_BLOCK)
        xcn = xc[:, sl]
        z = _dot(xcn.astype(BF16), wg_ref[n])
        r = _sigmoid(z[:, :RNN_BLOCK] + bra_ref[:, sl])
        i = _sigmoid(z[:, RNN_BLOCK:] + brx_ref[:, sl])
        log_a = -RG_C * r * sp[:, sl]
        a = jnp.exp(log_a)
        a_ref[:, sl] = a
        b_ref[:, sl] = jnp.sqrt(1.0 - a * a) * (i * xcn)


def _scan_rows(a, b, row_in_seg, steps):
    for d in steps:
        keep = row_in_seg >= d
        a_prev = jnp.where(keep, pltpu.roll(a, d, 0), 1.0)
        b_prev = jnp.where(keep, pltpu.roll(b, d, 0), 0.0)
        b = a * b_prev + b
        a = a * a_prev
    return a, b


def _mixin_prompt_kernel(x_ref, sh_ref, sc_ref, ng_ref, w_ref, qg_ref, kg_ref, conv0_ref, h0_ref, cw_ref, cb_ref,
                         wg_ref, bra_ref, brx_ref, lam_ref, wba_ref,
                         q_ref, k_ref, v_ref, sgb_ref, ya_ref, nconv_ref, hlast_ref, xbuf, a_s, b_s, hc_s, *, ts):
    s = pl.program_id(1)
    w = xbuf.shape[1]
    dq = q_ref.shape[1]
    dk = k_ref.shape[1]
    d = sgb_ref.shape[1]
    tail = CONV_WIDTH - 1
    lo = SUBLANES - tail

    @pl.when(s == 0)
    def _():
        xbuf[lo:SUBLANES, :] = conv0_ref[...]
        hc_s[...] = jnp.broadcast_to(h0_ref[...], hc_s.shape)

    x = x_ref[...]
    h = (_rms(x, ng_ref[...]) * (1.0 + sc_ref[...]) + sh_ref[...]).astype(BF16)
    o_gate, o_q = w, 2 * w
    o_k = o_q + dq
    o_v = o_k + dk
    o_ga = o_v + dk
    o_gb = o_ga + d

    xbuf[SUBLANES:SUBLANES + ts, :] = _dot(h, w_ref[:, 0:w])
    xc = cb_ref[...] + cw_ref[0:1, :] * xbuf[lo:lo + ts, :]
    for k in range(1, CONV_WIDTH):
        xc = xc + cw_ref[k:k + 1, :] * xbuf[lo + k:lo + k + ts, :]
    new_tail = xbuf[ts + lo:ts + SUBLANES, :]
    nconv_ref[...] = new_tail
    xbuf[lo:SUBLANES, :] = new_tail
    _rg_gates(xc, wg_ref, bra_ref, brx_ref, lam_ref, a_s, b_s)

    for hd in range(dq // HEAD_DIM):
        sl = slice(hd * HEAD_DIM, (hd + 1) * HEAD_DIM)
        q_ref[:, sl] = _rms(_dot(h, w_ref[:, o_q + hd * HEAD_DIM:o_q + (hd + 1) * HEAD_DIM]), qg_ref[...])
    for hd in range(dk // HEAD_DIM):
        sl = slice(hd * HEAD_DIM, (hd + 1) * HEAD_DIM)
        k_ref[:, sl] = _rms(_dot(h, w_ref[:, o_k + hd * HEAD_DIM:o_k + (hd + 1) * HEAD_DIM]), kg_ref[...])
    v_ref[...] = _dot(h, w_ref[:, o_v:o_v + dk])
    sgb_ref[...] = _sigmoid(_dot(h, w_ref[:, o_gb:o_gb + d]))

    row = lax.broadcasted_iota(jnp.int32, (SUBLANES, w), 0)

    def body(c, hc):
        off = pl.multiple_of(c * SUBLANES, SUBLANES)
        a_cum, b_cum = _scan_rows(a_s[pl.ds(off, SUBLANES), :], b_s[pl.ds(off, SUBLANES), :], row, (1, 2, 4))
        hs = a_cum * hc + b_cum
        a_s[pl.ds(off, SUBLANES), :] = hs
        return jnp.broadcast_to(hs[SUBLANES - 1:SUBLANES, :], (SUBLANES, w))

    hc = lax.fori_loop(0, ts // SUBLANES, body, hc_s[...])
    hc_s[...] = hc
    hlast_ref[...] = hc[0:1, :]
    u = (a_s[...] * jax.nn.gelu(_dot(h, w_ref[:, o_gate:o_gate + w]))).astype(BF16)
    ya_ref[...] = _sigmoid(_dot(h, w_ref[:, o_ga:o_ga + d])) * _dot(u, wba_ref[...])


def _mixin_prompt(x, sh, sc, ng, w_in, qg, kg, conv0, h0, cw, cb, wg, bra, brx, lam, wba, ts):
    b, s, d = x.shape
    w = wba.shape[0]
    dq, dk = N_HEADS * HEAD_DIM, N_KV_HEADS * HEAD_DIM
    tail = CONV_WIDTH - 1

    def seq(width):
        return pl.BlockSpec((None, ts, width), lambda i, j: (i, j, 0))

    def per_seq(rows, width):
        return pl.BlockSpec((None, rows, width), lambda i, j: (i, 0, 0))

    widths = (dq, dk, dk, d, d)
    return pl.pallas_call(
        functools.partial(_mixin_prompt_kernel, ts=ts),
        grid=(b, s // ts),
        in_specs=[seq(d), per_seq(1, d), per_seq(1, d), _resident((1, d)), _resident(w_in.shape),
                  _resident((1, HEAD_DIM)), _resident((1, HEAD_DIM)), per_seq(tail, w), per_seq(1, w),
                  _resident(cw.shape), _resident((1, w)), _resident(wg.shape), _resident((1, w)), _resident((1, w)),
                  _resident((1, w)), _resident(wba.shape)],
        out_specs=[seq(wd) for wd in widths] + [per_seq(tail, w), per_seq(1, w)],
        out_shape=[jax.ShapeDtypeStruct((b, s, wd), F32) for wd in widths]
        + [jax.ShapeDtypeStruct((b, tail, w), F32), jax.ShapeDtypeStruct((b, 1, w), F32)],
        scratch_shapes=[pltpu.VMEM((ts + SUBLANES, w), F32), pltpu.VMEM((ts, w), F32), pltpu.VMEM((ts, w), F32),
                        pltpu.VMEM((SUBLANES, w), F32)],
        compiler_params=_params(2),
        name="mixin_prompt",
    )(x, sh, sc, ng, w_in, qg, kg, conv0, h0, cw, cb, wg, bra, brx, lam, wba)


def _rnn_sample_kernel(xs_ref, ug_ref, sga_ref, h0_ref, cw_ref, cb_ref, wg_ref, bra_ref, brx_ref, lam_ref, wba_ref,
                       ya_ref, h_ref, a_s, b_s, *, seg):
    m, w = ug_ref.shape
    xc = cb_ref[...] + cw_ref[0:1, :] * xs_ref[0]
    for k in range(1, CONV_WIDTH):
        xc = xc + cw_ref[k:k + 1, :] * xs_ref[k]
    _rg_gates(xc, wg_ref, bra_ref, brx_ref, lam_ref, a_s, b_s)
    row_in_seg = lax.broadcasted_iota(jnp.int32, (SUBLANES, w), 0) % seg
    steps = tuple(d for d in (1, 2, 4) if d < seg)
    for c in range(m // SUBLANES):
        rows = slice(c * SUBLANES, (c + 1) * SUBLANES)
        a_cum, b_cum = _scan_rows(a_s[rows, :], b_s[rows, :], row_in_seg, steps)
        h_ref[rows, :] = a_cum * h0_ref[rows, :] + b_cum
    u = (h_ref[...] * ug_ref[...]).astype(BF16)
    ya_ref[...] = sga_ref[...] * _dot(u, wba_ref[...])


def _rnn_sample(xs, ug, sga, h0_rows, cw, cb, wg, bra, brx, lam, wba, seg):
    m, w = ug.shape
    d = wba.shape[1]
    return pl.pallas_call(
        functools.partial(_rnn_sample_kernel, seg=seg),
        out_shape=[jax.ShapeDtypeStruct((m, d), F32), jax.ShapeDtypeStruct((m, w), F32)],
        scratch_shapes=[pltpu.VMEM((m, w), F32), pltpu.VMEM((m, w), F32)],
        compiler_params=pltpu.CompilerParams(vmem_limit_bytes=VMEM_LIMIT),
        name="rnn_sample",
    )(xs, ug, sga, h0_rows, cw, cb, wg, bra, brx, lam, wba)


V_ROWS = HEAD_DIM + 16


def _attn_prompt_kernel(slope_ref, q_ref, k_ref, v_ref, o_ref, kb_s, vt_s, kbg_s, vtg_s, km_s, q2_s, bias_s, colb_s,
                        acc_s, *, nb, gt):
    g = pl.program_id(1)
    blk = MOBA_BLOCK
    s_len = nb * blk
    q_scale = ATTN_SCALE * LOG2E

    ones_rows = (lax.broadcasted_iota(jnp.int32, (V_ROWS - HEAD_DIM, blk), 0) == 0).astype(BF16)
    for n in range(nb):
        rows = slice(n * blk, (n + 1) * blk)
        grp, sub = n // gt, slice((n % gt) * blk, (n % gt + 1) * blk)
        kn = k_ref[rows, :]
        kb = kn.astype(BF16)
        vt = v_ref[rows, :].T.astype(BF16)
        kb_s[n] = kb
        kbg_s[grp, sub, :] = kb
        vt_s[n, 0:HEAD_DIM, :] = vt
        vt_s[n, HEAD_DIM:V_ROWS, :] = ones_rows
        vtg_s[grp, 0:HEAD_DIM, sub] = vt
        vtg_s[grp, HEAD_DIM:V_ROWS, sub] = ones_rows
        km_s[n:n + 1, :] = jnp.sum(kn, axis=0, keepdims=True) * (1.0 / blk)
        q2_s[n] = (q_ref[rows, :] * q_scale).astype(BF16)

    blk_id = lax.broadcasted_iota(jnp.int32, (nb, s_len), 0)
    q_blk = lax.broadcasted_iota(jnp.int32, (nb, s_len), 1) // blk
    fully_past = blk_id < q_blk
    blocks_ahead = ((blk_id - q_blk) * blk).astype(F32)
    key_off = lax.broadcasted_iota(jnp.int32, (blk, blk), 0)
    causal = key_off <= lax.broadcasted_iota(jnp.int32, (blk, blk), 1)
    km = km_s[...]
    for hh in range(GROUP):
        slope2 = slope_ref[g * GROUP + hh] * LOG2E
        cols = slice(hh * HEAD_DIM, (hh + 1) * HEAD_DIM)
        gs = jnp.where(fully_past, _dot_nt(km, q_ref[:, cols], precision=lax.Precision.HIGHEST), -jnp.inf)
        rank = jnp.zeros((nb, s_len), jnp.int32)
        for m in range(nb):
            gm = gs[m:m + 1, :]
            tie = (blk_id > m).astype(jnp.int32)
            rank = rank + jnp.where(gm > gs, 1, jnp.where(gm == gs, tie, 0))
        chosen = jnp.logical_and(fully_past, rank < MOBA_TOPK)
        bias = jnp.where(chosen, slope2 * blocks_ahead, NEG)
        for jq in range(nb):
            bias_s[hh, jq] = bias[:, jq * blk:(jq + 1) * blk]
        colb_s[hh] = slope2 * key_off.astype(F32)

    heads = [slice(hh * HEAD_DIM, (hh + 1) * HEAD_DIM) for hh in range(GROUP)]

    def group_scores(gi, j, q2):
        kg = kbg_s[gi]
        s_grp = [_dot_nt(kg, q2[:, cols]) for cols in heads]
        parts, tops = [], []
        for hh in range(GROUP):
            ps = [s_grp[hh][t * blk:(t + 1) * blk, :] + colb_s[hh] + bias_s[hh, j, pl.ds(gi * gt + t, 1), :]
                  for t in range(gt)]
            top = jnp.max(ps[0], axis=0, keepdims=True)
            for part in ps[1:]:
                top = jnp.maximum(top, jnp.max(part, axis=0, keepdims=True))
            parts.append(ps)
            tops.append(top)
        return parts, tops

    def q_block(j, _):
        q2 = q2_s[j]
        kd = kb_s[j]
        vd = vt_s[j]
        s_own = [_dot_nt(kd, q2[:, cols]) for cols in heads]
        parts0, tops0 = group_scores(0, j, q2)
        m_run, probs = [], []
        for hh in range(GROUP):
            s = jnp.where(causal, s_own[hh] + colb_s[hh], NEG)
            m1 = jnp.maximum(jnp.max(s, axis=0, keepdims=True), tops0[hh])
            p_own = jnp.exp2(s - m1).astype(BF16)
            p_grp = jnp.concatenate([jnp.exp2(part - m1).astype(BF16) for part in parts0[hh]], axis=0)
            probs.append((p_own, p_grp))
            m_run.append(m1)
        vg0 = vtg_s[0]
        for hh in range(GROUP):
            acc_s[hh] = _dot(vd, probs[hh][0]) + _dot(vg0, probs[hh][1])

        def kv_group(gi, m_run):
            vg = vtg_s[gi]
            parts, tops = group_scores(gi, j, q2)
            out, probs = [], []
            for hh in range(GROUP):
                m_new = jnp.maximum(m_run[hh], tops[hh])
                probs.append(jnp.concatenate([jnp.exp2(part - m_new).astype(BF16) for part in parts[hh]], axis=0))
                out.append(m_new)
            pv = [_dot(vg, p) for p in probs]
            for hh in range(GROUP):
                acc_s[hh] = jnp.exp2(m_run[hh] - out[hh]) * acc_s[hh] + pv[hh]
            return tuple(out)

        lax.fori_loop(1, (j + gt - 1) // gt, kv_group, tuple(m_run))
        rows = pl.ds(pl.multiple_of(j * blk, blk), blk)
        for hh in range(GROUP):
            acc = acc_s[hh]
            o = acc[0:HEAD_DIM, :] / acc[HEAD_DIM:HEAD_DIM + 1, :]
            o_ref[rows, heads[hh]] = o.T.astype(o_ref.dtype)
        return 0

    lax.fori_loop(0, nb, q_block, 0)


def _attn_prompt(q, k, v):
    b, s, _ = q.shape
    assert s % MOBA_BLOCK == 0
    nb = s // MOBA_BLOCK
    gt = KV_BLOCKS_PER_STEP if nb % KV_BLOCKS_PER_STEP == 0 else 1
    gw = GROUP * HEAD_DIM
    slopes = jnp.asarray(ALIBI_SLOPES, F32)
    kv_spec = pl.BlockSpec((None, s, HEAD_DIM), lambda i, g: (i, 0, g))
    q_spec = pl.BlockSpec((None, s, gw), lambda i, g: (i, 0, g))
    return pl.pallas_call(
        functools.partial(_attn_prompt_kernel, nb=nb, gt=gt),
        grid=(b, N_KV_HEADS),
        in_specs=[pl.BlockSpec(memory_space=pltpu.SMEM), q_spec, kv_spec, kv_spec],
        out_specs=q_spec,
        out_shape=jax.ShapeDtypeStruct(q.shape, BF16),
        scratch_shapes=[pltpu.VMEM((nb, MOBA_BLOCK, HEAD_DIM), BF16), pltpu.VMEM((nb, V_ROWS, MOBA_BLOCK), BF16),
                        pltpu.VMEM((nb // gt, gt * MOBA_BLOCK, HEAD_DIM), BF16),
                        pltpu.VMEM((nb // gt, V_ROWS, gt * MOBA_BLOCK), BF16),
                        pltpu.VMEM((nb, HEAD_DIM), F32), pltpu.VMEM((nb, MOBA_BLOCK, gw), BF16),
                        pltpu.VMEM((GROUP, nb, nb, MOBA_BLOCK), F32), pltpu.VMEM((GROUP, MOBA_BLOCK, MOBA_BLOCK), F32),
                        pltpu.VMEM((GROUP, V_ROWS, MOBA_BLOCK), F32)],
        compiler_params=_params(2),
        name="attn_prompt",
    )(slopes, q, k, v)


def _row_slopes(head):
    out = jnp.zeros(head.shape, F32)
    for h, sl in enumerate(ALIBI_SLOPES):
        out = jnp.where(head == h, sl, out)
    return out


def _attn_sample_kernel(pt_ref, q_ref, kn_ref, vn_ref, *rest, pp, n_blk, past, n_new):
    k_pages, v_pages = rest[:pp], rest[pp:2 * pp]
    o_ref = rest[2 * pp]
    km_s, m_s, l_s, op_s, bias_s = rest[2 * pp + 1:]
    del pt_ref
    s_id = pl.program_id(1)
    nq = q_ref.shape[0]
    kv = N_KV_HEADS
    cols = MOBA_BLOCK * kv
    q = q_ref[...]
    qb = (q * (ATTN_SCALE * LOG2E)).astype(BF16)

    row = lax.broadcasted_iota(jnp.int32, (nq, cols), 0)
    col = lax.broadcasted_iota(jnp.int32, (nq, cols), 1)
    head = row % N_HEADS
    step = row // N_HEADS

    @pl.when(s_id == 0)
    def _():
        same_kv = (col % kv) == head // GROUP
        back = (step + MOBA_BLOCK - col // kv).astype(F32)
        bias_s[...] = jnp.where(same_kv, -(_row_slopes(head) * LOG2E) * back, NEG)
        m_s[...] = jnp.zeros_like(m_s)
        l_s[...] = jnp.zeros_like(l_s)

    slope_col = _row_slopes(head[:, 0:1]) * LOG2E
    lane = lax.broadcasted_iota(jnp.int32, (nq, 128), 1)
    per_step = pp // 2
    first_blk = s_id * per_step
    ksums, scores, probs, partials = [], [], [], []
    for i in range(per_step):
        k0, k1 = k_pages[2 * i][...], k_pages[2 * i + 1][...]
        ksums.append(jnp.sum(k0.reshape(-1, SUBLANES, HEAD_DIM), axis=0)
                     + jnp.sum(k1.reshape(-1, SUBLANES, HEAD_DIM), axis=0))
        kb = jnp.concatenate([k0, k1], axis=0).astype(BF16)
        scores.append(_dot_nt(qb, kb) + bias_s[...])
    for s in scores:
        m = jnp.max(s, axis=-1, keepdims=True)
        p = jnp.exp2(s - m)
        probs.append((m, jnp.sum(p, axis=-1, keepdims=True), p.astype(BF16)))
    for i, (m, l, p) in enumerate(probs):
        vb = jnp.concatenate([v_pages[2 * i][...], v_pages[2 * i + 1][...]], axis=0).astype(BF16)
        partials.append((m, l, _dot(p, vb)))
    m_new, l_new = m_s[...], l_s[...]
    for i, (m, l, o_part) in enumerate(partials):
        n = first_blk + i
        km_s[pl.ds(pl.multiple_of(n * SUBLANES, SUBLANES), SUBLANES), :] = ksums[i] * (1.0 / MOBA_BLOCK)
        op_s[n] = o_part
        m_new = jnp.where(lane == n, m + slope_col * ((n + 1) * MOBA_BLOCK - past).astype(F32), m_new)
        l_new = jnp.where(lane == n, l, l_new)
    m_s[...] = m_new
    l_s[...] = l_new

    @pl.when(s_id == pl.num_programs(1) - 1)
    def _():
        gcols = n_blk * SUBLANES
        gfull = _dot_nt(q, km_s[...], precision=lax.Precision.HIGHEST)
        gsum = gfull + pltpu.roll(gfull, gcols - kv, 1)
        grow = lax.broadcasted_iota(jnp.int32, (nq, gcols), 0)
        gcol = lax.broadcasted_iota(jnp.int32, (nq, gcols), 1)
        gs = jnp.where((gcol % SUBLANES) == (grow % N_HEADS) // GROUP, gsum, -jnp.inf)
        gcol_f = gcol.astype(F32)
        chosen = jnp.zeros((nq, 128), F32)
        for _ in range(MOBA_TOPK):
            best = jnp.max(gs, axis=-1, keepdims=True)
            first = jnp.min(jnp.where(gs == best, gcol_f, float(gcols)), axis=-1, keepdims=True)
            chosen = jnp.where(lane == (first.astype(jnp.int32) // SUBLANES), 1.0, chosen)
            gs = jnp.where(gcol_f == first, -jnp.inf, gs)
        is_chosen = chosen > 0.0

        nk = n_new * kv
        orow = lax.broadcasted_iota(jnp.int32, (nq, nk), 0)
        ocol = lax.broadcasted_iota(jnp.int32, (nq, nk), 1)
        ohead, ostep = orow % N_HEADS, orow // N_HEADS
        ok = jnp.logical_and((ocol % kv) == ohead // GROUP, ocol // kv <= ostep)
        so = _dot_nt(qb, kn_ref[...].astype(BF16)) - (_row_slopes(ohead) * LOG2E) * (ostep - ocol // kv).astype(F32)
        so = jnp.where(ok, so, NEG)

        m_all = jnp.maximum(jnp.max(so, axis=-1, keepdims=True),
                            jnp.max(jnp.where(is_chosen, m_s[...], NEG), axis=-1, keepdims=True))
        wgt = jnp.where(is_chosen, jnp.exp2(m_s[...] - m_all), 0.0)
        po = jnp.exp2(so - m_all)
        l_all = jnp.sum(wgt * l_s[...], axis=-1, keepdims=True) + jnp.sum(po, axis=-1, keepdims=True)
        acc = _dot(po.astype(BF16), vn_ref[...].astype(BF16))
        for n in range(n_blk):
            acc = acc + wgt[:, n:n + 1] * op_s[n]
        o_ref[...] = acc / l_all


def _attn_sample(q, k_new, v_new, cache_k, cache_v, page_ids, n_new):
    b, nq, _ = q.shape
    n_pages = page_ids.shape[1]
    past = n_pages * PAGE_SIZE
    pp = PAGES_PER_STEP
    assert MOBA_BLOCK == 2 * PAGE_SIZE and past % MOBA_BLOCK == 0 and n_pages % pp == 0
    n_blk = past // MOBA_BLOCK
    assert MOBA_TOPK <= n_blk <= 128 and n_new <= MOBA_BLOCK
    rows = PAGE_SIZE * N_KV_HEADS

    def page(i):
        return pl.BlockSpec((None, rows, HEAD_DIM), lambda bi, si, pt: (pt[bi * n_pages + si * pp + i], 0, 0))

    def per_seq(r):
        return pl.BlockSpec((None, r, HEAD_DIM), lambda bi, si, pt: (bi, 0, 0))

    return pl.pallas_call(
        functools.partial(_attn_sample_kernel, pp=pp, n_blk=n_blk, past=past, n_new=n_new),
        grid_spec=pltpu.PrefetchScalarGridSpec(
            num_scalar_prefetch=1,
            grid=(b, n_pages // pp),
            in_specs=[per_seq(nq), per_seq(n_new * N_KV_HEADS), per_seq(n_new * N_KV_HEADS)]
            + [page(i) for i in range(pp)] * 2,
            out_specs=per_seq(nq),
            scratch_shapes=[pltpu.VMEM((n_blk * SUBLANES, HEAD_DIM), F32), pltpu.VMEM((nq, 128), F32),
                            pltpu.VMEM((nq, 128), F32), pltpu.VMEM((n_blk, nq, HEAD_DIM), F32),
                            pltpu.VMEM((nq, MOBA_BLOCK * N_KV_HEADS), F32)]),
        out_shape=jax.ShapeDtypeStruct((b, nq, HEAD_DIM), F32),
        compiler_params=_params(2),
        name="attn_sample",
    )(page_ids.reshape(-1), q, k_new, v_new, *([cache_k] * pp), *([cache_v] * pp))


def _merge_ffn_kernel(x_ref, ya_ref, o_ref_in, sgb_ref, g2_ref, sh_ref, sc_ref, g3_ref, ng_ref,
                      wbb_ref, wout_ref, wg_ref, wu_ref, wd_ref, out_ref, *, chunk):
    yb = _dot(o_ref_in[...].astype(BF16), wbb_ref[...])
    merged = (ya_ref[...] + sgb_ref[...] * yb).astype(BF16)
    x = x_ref[...] + g2_ref[...] * _dot(merged, wout_ref[...])
    h = (_rms(x, ng_ref[...]) * (1.0 + sc_ref[...]) + sh_ref[...]).astype(BF16)
    out_ref[...] = x + 0.5 * g3_ref[...] * _swiglu_acc(h, wg_ref, wu_ref, wd_ref, chunk)


def _merge_ffn(x, ya, o, sgb, g2, sh, sc, g3, ng, wbb, wout, wg, wu, wd, tm, tiles_per_group):
    t, d = x.shape
    row = pl.BlockSpec((tm, d), lambda i: (i, 0))
    mod = functools.partial(_mod_spec, tm=tm, tiles_per_group=tiles_per_group)
    return pl.pallas_call(
        functools.partial(_merge_ffn_kernel, chunk=_ffn_chunk(wg.shape[1])),
        grid=(t // tm,),
        in_specs=[row, row, pl.BlockSpec((tm, o.shape[1]), lambda i: (i, 0)), row,
                  mod(g2), mod(sh), mod(sc), mod(g3), _resident((1, d)),
                  _resident(wbb.shape), _resident(wout.shape), _resident(wg.shape), _resident(wu.shape),
                  _resident(wd.shape)],
        out_specs=row,
        out_shape=jax.ShapeDtypeStruct((t, d), F32),
        compiler_params=_params(1),
        name="merge_ffn",
    )(x, ya, o, sgb, g2, sh, sc, g3, ng, wbb, wout, wg, wu, wd)


def _token_tile(t):
    return 512 if t % 512 == 0 else t


def _layer_weights(p):
    w = dict(p)
    for name in ("ffn1_gate", "ffn1_up", "ffn1_down", "w_in", "w_branch_a", "w_branch_b", "w_out",
                 "ffn2_gate", "ffn2_up", "ffn2_down"):
        w[name] = p[name].astype(BF16)
    w["rg_w"] = jnp.concatenate([p["rg_wa"], p["rg_wx"]], axis=-1).astype(BF16)
    for name in ("norm1_g", "norm2_g", "norm3_g", "conv_b", "rg_ba", "rg_bx", "rg_lambda", "q_norm_g", "k_norm_g"):
        w[name] = p[name].reshape(1, -1)
    return w


def _prompt_layer(x, mod, w):
    b, s, d = x.shape
    t = b * s
    tm = _token_tile(s)
    tpg = s // tm
    sh1, sc1, g1, sh2, sc2, g2, sh3, sc3, g3 = [m[:, None, :] for m in jnp.split(mod, 3 * N_SUBLAYERS, axis=-1)]
    x1 = _ffn(x.reshape(t, d), sh1, sc1, g1, w["norm1_g"], w["ffn1_gate"], w["ffn1_up"], w["ffn1_down"], tm, tpg)
    d_rnn = w["w_branch_a"].shape[0]
    q, k, v, sgb, ya, new_conv, h_last = _mixin_prompt(
        x1.reshape(b, s, d), sh2, sc2, w["norm2_g"], w["w_in"], w["q_norm_g"], w["k_norm_g"],
        jnp.zeros((b, CONV_WIDTH - 1, d_rnn), F32), jnp.zeros((b, 1, d_rnn), F32),
        w["conv_w"], w["conv_b"], w["rg_w"], w["rg_ba"], w["rg_bx"], w["rg_lambda"], w["w_branch_a"], tm)
    sgb = sgb.reshape(t, d)
    o = _attn_prompt(q.reshape(b, s, -1), k.reshape(b, s, -1), v.reshape(b, s, -1))
    y = _merge_ffn(x1, ya.reshape(t, d), o.reshape(t, -1), sgb, g2, sh3, sc3, g3, w["norm3_g"],
                   w["w_branch_b"], w["w_out"], w["ffn2_gate"], w["ffn2_up"], w["ffn2_down"], tm, tpg)
    return (y.reshape(b, s, d), k.reshape(b, s, N_KV_HEADS, HEAD_DIM), v.reshape(b, s, N_KV_HEADS, HEAD_DIM),
            new_conv, h_last.reshape(b, d_rnn))


def _sample_layer(x, mod, conv_buf, h0, cache_k, cache_v, page_ids, w):
    b, s, d = x.shape
    t = b * s
    rep = lambda m: jnp.repeat(m, s, axis=0)[None]
    sh1, sc1, g1, sh2, sc2, g2, sh3, sc3, g3 = [rep(m) for m in jnp.split(mod, 3 * N_SUBLAYERS, axis=-1)]
    x1 = _ffn(x.reshape(t, d), sh1, sc1, g1, w["norm1_g"], w["ffn1_gate"], w["ffn1_up"], w["ffn1_down"], t, 1)
    xr, ug, q, k, v, sga, sgb = _inproj(x1, sh2, sc2, w["norm2_g"], w["w_in"], w["q_norm_g"], w["k_norm_g"], t, 1)
    d_rnn = xr.shape[1]
    xp = jnp.concatenate([conv_buf, xr.reshape(b, s, d_rnn)], axis=1)
    taps = jnp.stack([xp[:, kk:kk + s].reshape(t, d_rnn) for kk in range(CONV_WIDTH)])
    ya, h_all = _rnn_sample(taps, ug, sga, jnp.repeat(h0, s, axis=0), w["conv_w"], w["conv_b"], w["rg_w"],
                            w["rg_ba"], w["rg_bx"], w["rg_lambda"], w["w_branch_a"], s)
    o = _attn_sample(q.reshape(b, s * N_HEADS, HEAD_DIM), k.reshape(b, s * N_KV_HEADS, HEAD_DIM),
                     v.reshape(b, s * N_KV_HEADS, HEAD_DIM), cache_k, cache_v, page_ids, s)
    y = _merge_ffn(x1, ya, o.reshape(t, -1), sgb, g2, sh3, sc3, g3, w["norm3_g"],
                   w["w_branch_b"], w["w_out"], w["ffn2_gate"], w["ffn2_up"], w["ffn2_down"], t, 1)
    return (y.reshape(b, s, d), k.reshape(b, s, N_KV_HEADS, HEAD_DIM), v.reshape(b, s, N_KV_HEADS, HEAD_DIM),
            xp[:, -(CONV_WIDTH - 1):], h_all.reshape(b, s, d_rnn)[:, -1])


def kernel(x_prompt, x_sample, c_prompt, c_sample, cache_k, cache_v, state_conv, state_rglru, page_table, w_ada, b_ada, norm1_g, ffn1_gate, ffn1_up, ffn1_down, norm2_g, w_in, conv_w, conv_b, rg_wa, rg_ba, rg_wx, rg_bx, rg_lambda, q_norm_g, k_norm_g, w_branch_a, w_branch_b, w_out, norm3_g, ffn2_gate, ffn2_up, ffn2_down):
    depth, n_pool = cache_k.shape[0], cache_k.shape[1]
    bp = x_prompt.shape[0]
    assert cache_k.shape[2:] == (PAGE_SIZE, N_KV_HEADS, HEAD_DIM)
    ck = cache_k.reshape(depth * n_pool, PAGE_SIZE * N_KV_HEADS, HEAD_DIM)
    cv = cache_v.reshape(depth * n_pool, PAGE_SIZE * N_KV_HEADS, HEAD_DIM)
    c_all = jnp.concatenate([c_prompt, c_sample], axis=0)
    pad = -c_all.shape[0] % SUBLANES
    c_all = jnp.pad(c_all, ((0, pad), (0, 0)))
    yp, ys = x_prompt, x_sample
    outs = [[] for _ in range(8)]
    for l in range(depth):
        p = dict(norm1_g=norm1_g[l], ffn1_gate=ffn1_gate[l], ffn1_up=ffn1_up[l], ffn1_down=ffn1_down[l],
                 norm2_g=norm2_g[l], w_in=w_in[l], conv_w=conv_w[l], conv_b=conv_b[l], rg_wa=rg_wa[l],
                 rg_ba=rg_ba[l], rg_wx=rg_wx[l], rg_bx=rg_bx[l], rg_lambda=rg_lambda[l], q_norm_g=q_norm_g[l],
                 k_norm_g=k_norm_g[l], w_branch_a=w_branch_a[l], w_branch_b=w_branch_b[l], w_out=w_out[l],
                 norm3_g=norm3_g[l], ffn2_gate=ffn2_gate[l], ffn2_up=ffn2_up[l], ffn2_down=ffn2_down[l])
        w = _layer_weights(p)
        mod = _ada(c_all, w_ada[l], b_ada[l])
        yp, kp, vp, cp, hp = _prompt_layer(yp, mod[:bp], w)
        ys, ks, vs, cs, hs = _sample_layer(ys, mod[bp:bp + x_sample.shape[0]], state_conv[l], state_rglru[l],
                                           ck, cv, page_table + l * n_pool, w)
        for lst, val in zip(outs, (kp, vp, cp, hp, ks, vs, cs, hs)):
            lst.append(val)
    return (yp, ys) + tuple(jnp.stack(o) for o in outs)
```

```python
import functools

import jax
import jax.numpy as jnp
from jax import lax
from jax.experimental import pallas as pl
from jax.experimental.pallas import tpu as pltpu

F32 = jnp.float32
BF16 = jnp.bfloat16

N_HEADS = 8
N_KV_HEADS = 4
HEAD_DIM = 128
GROUP = N_HEADS // N_KV_HEADS
MOBA_BLOCK = 256
MOBA_TOPK = 3
PAGE_SIZE = 128
N_RNN_BLOCKS = 8
RNN_BLOCK = 128
CONV_WIDTH = 4
RG_C = 8.0
N_SUBLAYERS = 3
EPS = 1e-6
ALIBI_SLOPES = tuple(2.0 ** (-8.0 * (h + 1) / N_HEADS) for h in range(N_HEADS))
ATTN_SCALE = HEAD_DIM ** -0.5
NEG = -1e30
LOG2E = 1.4426950408889634
SUBLANES = 8
VMEM_LIMIT = 56 * 1024 * 1024
PAGES_PER_STEP = 32
KV_BLOCKS_PER_STEP = 4
GATE_COST = 40
MXU_COLS = 256


def _params(n_axes):
    return pltpu.CompilerParams(dimension_semantics=("arbitrary",) * n_axes, vmem_limit_bytes=VMEM_LIMIT)


def _resident(shape):
    nd = len(shape)
    return pl.BlockSpec(shape, lambda *_: (0,) * nd, pipeline_mode=pl.Buffered(1))


def _dot(a, b):
    return jnp.dot(a, b, preferred_element_type=F32)


def _dot_nt(a, b, precision=None):
    return lax.dot_general(a, b, (((1,), (1,)), ((), ())), precision=precision, preferred_element_type=F32)


def _rms(x, g):
    return x * lax.rsqrt(jnp.mean(x * x, axis=-1, keepdims=True) + EPS) * g


def _sigmoid(x):
    return 0.5 * jnp.tanh(0.5 * x) + 0.5


def _silu(x):
    return x * _sigmoid(x)


def _ada_kernel(c_ref, w_ref, b_ref, o_ref):
    a = _silu(c_ref[...]).astype(BF16)
    o_ref[...] = _dot(a, w_ref[...].astype(BF16)) + b_ref[...]


def _ada(c, w, b):
    m, d = c.shape
    n = w.shape[1]
    tn = n // 8
    return pl.pallas_call(
        _ada_kernel,
        grid=(n // tn,),
        in_specs=[pl.BlockSpec((m, d), lambda i: (0, 0)),
                  pl.BlockSpec((d, tn), lambda i: (0, i)),
                  pl.BlockSpec((1, tn), lambda i: (0, i))],
        out_specs=pl.BlockSpec((m, tn), lambda i: (0, i)),
        out_shape=jax.ShapeDtypeStruct((m, n), F32),
        compiler_params=_params(1),
        name="ada",
    )(c, w, b.reshape(1, n))


def _swiglu_acc(h, wg_ref, wu_ref, wd_ref, chunk):
    d_ff = wg_ref.shape[1]
    acc = None
    for c in range(d_ff // chunk):
        sl = slice(c * chunk, (c + 1) * chunk)
        act = (_silu(_dot(h, wg_ref[:, sl])) * _dot(h, wu_ref[:, sl])).astype(BF16)
        part = _dot(act, wd_ref[sl, :])
        acc = part if acc is None else acc + part
    return acc


def _ffn_kernel(x_ref, sh_ref, sc_ref, g_ref, ng_ref, wg_ref, wu_ref, wd_ref, o_ref, *, chunk):
    x = x_ref[...]
    h = (_rms(x, ng_ref[...]) * (1.0 + sc_ref[...]) + sh_ref[...]).astype(BF16)
    o_ref[...] = x + 0.5 * g_ref[...] * _swiglu_acc(h, wg_ref, wu_ref, wd_ref, chunk)


def _mod_spec(mod, tm, tiles_per_group):
    r, d = mod.shape[1], mod.shape[2]
    return pl.BlockSpec((None, r, d), lambda i: (i // tiles_per_group, 0, 0))


def _ffn_chunk(d_ff):
    return 256 if d_ff % 256 == 0 else 128


def _ffn(x, sh, sc, g, ng, wg, wu, wd, tm, tiles_per_group):
    t, d = x.shape
    d_ff = wg.shape[1]
    row = pl.BlockSpec((tm, d), lambda i: (i, 0))
    return pl.pallas_call(
        functools.partial(_ffn_kernel, chunk=_ffn_chunk(d_ff)),
        grid=(t // tm,),
        in_specs=[row, _mod_spec(sh, tm, tiles_per_group), _mod_spec(sc, tm, tiles_per_group),
                  _mod_spec(g, tm, tiles_per_group), _resident((1, d)),
                  _resident(wg.shape), _resident(wu.shape), _resident(wd.shape)],
        out_specs=row,
        out_shape=jax.ShapeDtypeStruct((t, d), F32),
        compiler_params=_params(1),
        name="ffn",
    )(x, sh, sc, g, ng, wg, wu, wd)


def _inproj_kernel(x_ref, sh_ref, sc_ref, ng_ref, w_ref, qg_ref, kg_ref,
                   xr_ref, ug_ref, q_ref, k_ref, v_ref, sga_ref, sgb_ref):
    d_rnn = xr_ref.shape[1]
    dq = q_ref.shape[1]
    dk = k_ref.shape[1]
    x = x_ref[...]
    h = (_rms(x, ng_ref[...]) * (1.0 + sc_ref[...]) + sh_ref[...]).astype(BF16)
    o = 0
    xr_ref[...] = _dot(h, w_ref[:, o:o + d_rnn])
    o += d_rnn
    ug_ref[...] = jax.nn.gelu(_dot(h, w_ref[:, o:o + d_rnn]))
    o += d_rnn
    for hd in range(dq // HEAD_DIM):
        sl = slice(hd * HEAD_DIM, (hd + 1) * HEAD_DIM)
        q_ref[:, sl] = _rms(_dot(h, w_ref[:, o + hd * HEAD_DIM:o + (hd + 1) * HEAD_DIM]), qg_ref[...])
    o += dq
    for hd in range(dk // HEAD_DIM):
        sl = slice(hd * HEAD_DIM, (hd + 1) * HEAD_DIM)
        k_ref[:, sl] = _rms(_dot(h, w_ref[:, o + hd * HEAD_DIM:o + (hd + 1) * HEAD_DIM]), kg_ref[...])
    o += dk
    v_ref[...] = _dot(h, w_ref[:, o:o + dk])
    o += dk
    d = sga_ref.shape[1]
    sga_ref[...] = _sigmoid(_dot(h, w_ref[:, o:o + d]))
    o += d
    sgb_ref[...] = _sigmoid(_dot(h, w_ref[:, o:o + d]))


def _inproj(x, sh, sc, ng, w_in, qg, kg, tm, tiles_per_group):
    t, d = x.shape
    dq, dk = N_HEADS * HEAD_DIM, N_KV_HEADS * HEAD_DIM
    d_rnn = (w_in.shape[1] - dq - 2 * dk - 2 * d) // 2

    def row(w):
        return pl.BlockSpec((tm, w), lambda i: (i, 0))

    widths = (d_rnn, d_rnn, dq, dk, dk, d, d)
    return pl.pallas_call(
        _inproj_kernel,
        grid=(t // tm,),
        in_specs=[row(d), _mod_spec(sh, tm, tiles_per_group), _mod_spec(sc, tm, tiles_per_group),
                  _resident((1, d)), _resident(w_in.shape), _resident((1, HEAD_DIM)), _resident((1, HEAD_DIM))],
        out_specs=[row(w) for w in widths],
        out_shape=[jax.ShapeDtypeStruct((t, w), F32) for w in widths],
        compiler_params=_params(1),
        name="inproj",
    )(x, sh, sc, ng, w_in, qg, kg)


def _rg_gate_block(xcn, n, wg_ref, bra_ref, brx_ref, sp, a_ref, b_ref):
    sl = slice(n * RNN_BLOCK, (n + 1) * RNN_BLOCK)
    z = _dot(xcn.astype(BF16), wg_ref[n])
    r = _sigmoid(z[:, :RNN_BLOCK] + bra_ref[:, sl])
    i = _sigmoid(z[:, RNN_BLOCK:] + brx_ref[:, sl])
    log_a = -RG_C * r * sp[:, sl]
    a = jnp.exp(log_a)
    a_ref[:, sl] = a
    b_ref[:, sl] = jnp.sqrt(1.0 - a * a) * (i * xcn)


def _rg_gates(xc, wg_ref, bra_ref, brx_ref, lam_ref, a_ref, b_ref):
    sp = jax.nn.softplus(-lam_ref[...])
    for n in range(N_RNN_BLOCKS):
        _rg_gate_block(xc[:, n * RNN_BLOCK:(n + 1) * RNN_BLOCK], n, wg_ref, bra_ref, brx_ref, sp, a_ref, b_ref)


def _scan_rows(a, b, row_in_seg, steps):
    for d in steps:
        keep = row_in_seg >= d
        a_prev = jnp.where(keep, pltpu.roll(a, d, 0), 1.0)
        b_prev = jnp.where(keep, pltpu.roll(b, d, 0), 0.0)
        b = a * b_prev + b
        a = a * a_prev
    return a, b


def _mixin_prompt_kernel(x_ref, sh_ref, sc_ref, ng_ref, w_ref, qg_ref, kg_ref, conv0_ref, h0_ref, cw_ref, cb_ref,
                         wg_ref, bra_ref, brx_ref, lam_ref, wba_ref,
                         q_ref, k_ref, v_ref, sgb_ref, ya_ref, nconv_ref, hlast_ref, xbuf, a_s, b_s, hc_s, ug_s, sga_s, *, ts):
    s = pl.program_id(1)
    w = xbuf.shape[1]
    dq = q_ref.shape[1]
    dk = k_ref.shape[1]
    d = sgb_ref.shape[1]
    tail = CONV_WIDTH - 1
    lo = SUBLANES - tail

    @pl.when(s == 0)
    def _():
        xbuf[lo:SUBLANES, :] = conv0_ref[...]
        hc_s[...] = jnp.broadcast_to(h0_ref[...], hc_s.shape)

    x = x_ref[...]
    h = (_rms(x, ng_ref[...]) * (1.0 + sc_ref[...]) + sh_ref[...]).astype(BF16)
    o_gate, o_q = w, 2 * w
    o_k = o_q + dq
    o_v = o_k + dk
    o_ga = o_v + dk
    o_gb = o_ga + d
    cw_ = MXU_COLS

    def proj(lo):
        return _dot(h, w_ref[:, lo:lo + cw_])

    def normed_heads(ref, gain_ref, base, c):
        z = proj(base + c * cw_)
        for i in range(cw_ // HEAD_DIM):
            col = c * cw_ + i * HEAD_DIM
            ref[:, col:col + HEAD_DIM] = _rms(z[:, i * HEAD_DIM:(i + 1) * HEAD_DIM], gain_ref[...])

    def store(ref, base, c, fn):
        ref[:, c * cw_:(c + 1) * cw_] = fn(proj(base + c * cw_))

    jobs = [functools.partial(normed_heads, q_ref, qg_ref, o_q, c) for c in range(dq // cw_)]
    jobs += [functools.partial(normed_heads, k_ref, kg_ref, o_k, c) for c in range(dk // cw_)]
    jobs += [functools.partial(store, v_ref, o_v, c, lambda z: z) for c in range(dk // cw_)]
    jobs += [functools.partial(store, sgb_ref, o_gb, c, _sigmoid) for c in range(d // cw_)]
    jobs += [functools.partial(store, ug_s, o_gate, c, jax.nn.gelu) for c in range(w // cw_)]
    jobs += [functools.partial(store, sga_s, o_ga, c, _sigmoid) for c in range(d // cw_)]
    cost = [GATE_COST] * N_RNN_BLOCKS + [1] * (ts // SUBLANES)
    due, acc_cost = [], 0
    for c_ in cost:
        acc_cost += c_
        due.append((acc_cost * len(jobs)) // sum(cost))
    done = 0

    for c in range(w // cw_):
        xbuf[SUBLANES:SUBLANES + ts, c * cw_:(c + 1) * cw_] = proj(c * cw_)
    sp = jax.nn.softplus(-lam_ref[...])
    for n in range(N_RNN_BLOCKS):
        sl = slice(n * RNN_BLOCK, (n + 1) * RNN_BLOCK)
        xcn = cb_ref[:, sl] + cw_ref[0:1, sl] * xbuf[lo:lo + ts, sl]
        for k in range(1, CONV_WIDTH):
            xcn = xcn + cw_ref[k:k + 1, sl] * xbuf[lo + k:lo + k + ts, sl]
        _rg_gate_block(xcn, n, wg_ref, bra_ref, brx_ref, sp, a_s, b_s)
        while done < due[n]:
            jobs[done]()
            done += 1
    new_tail = xbuf[ts + lo:ts + SUBLANES, :]
    nconv_ref[...] = new_tail
    xbuf[lo:SUBLANES, :] = new_tail

    row = lax.broadcasted_iota(jnp.int32, (SUBLANES, w), 0)
    hc = hc_s[...]
    for c in range(ts // SUBLANES):
        rows = slice(c * SUBLANES, (c + 1) * SUBLANES)
        a_cum, b_cum = _scan_rows(a_s[rows, :], b_s[rows, :], row, (1, 2, 4))
        hs = a_cum * hc + b_cum
        a_s[rows, :] = hs
        hc = jnp.broadcast_to(hs[SUBLANES - 1:SUBLANES, :], (SUBLANES, w))
        while done < due[N_RNN_BLOCKS + c]:
            jobs[done]()
            done += 1
    hc_s[...] = hc
    hlast_ref[...] = hc[0:1, :]
    u = (a_s[...] * ug_s[...]).astype(BF16)
    ya_ref[...] = sga_s[...] * _dot(u, wba_ref[...])


def _mixin_prompt(x, sh, sc, ng, w_in, qg, kg, conv0, h0, cw, cb, wg, bra, brx, lam, wba, ts):
    b, s, d = x.shape
    w = wba.shape[0]
    dq, dk = N_HEADS * HEAD_DIM, N_KV_HEADS * HEAD_DIM
    tail = CONV_WIDTH - 1

    def seq(width):
        return pl.BlockSpec((None, ts, width), lambda i, j: (i, j, 0))

    def per_seq(rows, width):
        return pl.BlockSpec((None, rows, width), lambda i, j: (i, 0, 0))

    widths = (dq, dk, dk, d, d)
    return pl.pallas_call(
        functools.partial(_mixin_prompt_kernel, ts=ts),
        grid=(b, s // ts),
        in_specs=[seq(d), per_seq(1, d), per_seq(1, d), _resident((1, d)), _resident(w_in.shape),
                  _resident((1, HEAD_DIM)), _resident((1, HEAD_DIM)), per_seq(tail, w), per_seq(1, w),
                  _resident(cw.shape), _resident((1, w)), _resident(wg.shape), _resident((1, w)), _resident((1, w)),
                  _resident((1, w)), _resident(wba.shape)],
        out_specs=[seq(wd) for wd in widths] + [per_seq(tail, w), per_seq(1, w)],
        out_shape=[jax.ShapeDtypeStruct((b, s, wd), F32) for wd in widths]
        + [jax.ShapeDtypeStruct((b, tail, w), F32), jax.ShapeDtypeStruct((b, 1, w), F32)],
        scratch_shapes=[pltpu.VMEM((ts + SUBLANES, w), F32), pltpu.VMEM((ts, w), F32), pltpu.VMEM((ts, w), F32),
                        pltpu.VMEM((SUBLANES, w), F32), pltpu.VMEM((ts, w), F32), pltpu.VMEM((ts, d), F32)],
        compiler_params=_params(2),
        name="mixin_prompt",
    )(x, sh, sc, ng, w_in, qg, kg, conv0, h0, cw, cb, wg, bra, brx, lam, wba)


def _rnn_sample_kernel(xs_ref, ug_ref, sga_ref, h0_ref, cw_ref, cb_ref, wg_ref, bra_ref, brx_ref, lam_ref, wba_ref,
                       ya_ref, h_ref, a_s, b_s, *, seg):
    m, w = ug_ref.shape
    xc = cb_ref[...] + cw_ref[0:1, :] * xs_ref[0]
    for k in range(1, CONV_WIDTH):
        xc = xc + cw_ref[k:k + 1, :] * xs_ref[k]
    _rg_gates(xc, wg_ref, bra_ref, brx_ref, lam_ref, a_s, b_s)
    row_in_seg = lax.broadcasted_iota(jnp.int32, (SUBLANES, w), 0) % seg
    steps = tuple(d for d in (1, 2, 4) if d < seg)
    for c in range(m // SUBLANES):
        rows = slice(c * SUBLANES, (c + 1) * SUBLANES)
        a_cum, b_cum = _scan_rows(a_s[rows, :], b_s[rows, :], row_in_seg, steps)
        h_ref[rows, :] = a_cum * h0_ref[rows, :] + b_cum
    u = (h_ref[...] * ug_ref[...]).astype(BF16)
    ya_ref[...] = sga_ref[...] * _dot(u, wba_ref[...])


def _rnn_sample(xs, ug, sga, h0_rows, cw, cb, wg, bra, brx, lam, wba, seg):
    m, w = ug.shape
    d = wba.shape[1]
    return pl.pallas_call(
        functools.partial(_rnn_sample_kernel, seg=seg),
        out_shape=[jax.ShapeDtypeStruct((m, d), F32), jax.ShapeDtypeStruct((m, w), F32)],
        scratch_shapes=[pltpu.VMEM((m, w), F32), pltpu.VMEM((m, w), F32)],
        compiler_params=pltpu.CompilerParams(vmem_limit_bytes=VMEM_LIMIT),
        name="rnn_sample",
    )(xs, ug, sga, h0_rows, cw, cb, wg, bra, brx, lam, wba)


V_ROWS = HEAD_DIM + 16


def _attn_prompt_kernel(slope_ref, q_ref, k_ref, v_ref, o_ref, kb_s, vt_s, kbg_s, vtg_s, km_s, q2_s, bias_s, colb_s,
                        acc_s, *, nb, gt):
    g = pl.program_id(1)
    blk = MOBA_BLOCK
    s_len = nb * blk
    q_scale = ATTN_SCALE * LOG2E

    ones_rows = (lax.broadcasted_iota(jnp.int32, (V_ROWS - HEAD_DIM, blk), 0) == 0).astype(BF16)
    for n in range(nb):
        rows = slice(n * blk, (n + 1) * blk)
        grp, sub = n // gt, slice((n % gt) * blk, (n % gt + 1) * blk)
        kn = k_ref[rows, :]
        kb = kn.astype(BF16)
        vt = v_ref[rows, :].T.astype(BF16)
        kb_s[n] = kb
        kbg_s[grp, sub, :] = kb
        vt_s[n, 0:HEAD_DIM, :] = vt
        vt_s[n, HEAD_DIM:V_ROWS, :] = ones_rows
        vtg_s[grp, 0:HEAD_DIM, sub] = vt
        vtg_s[grp, HEAD_DIM:V_ROWS, sub] = ones_rows
        km_s[n:n + 1, :] = jnp.sum(kn, axis=0, keepdims=True) * (1.0 / blk)
        q2_s[n] = (q_ref[rows, :] * q_scale).astype(BF16)

    blk_id = lax.broadcasted_iota(jnp.int32, (nb, s_len), 0)
    q_blk = lax.broadcasted_iota(jnp.int32, (nb, s_len), 1) // blk
    fully_past = blk_id < q_blk
    blocks_ahead = ((blk_id - q_blk) * blk).astype(F32)
    key_off = lax.broadcasted_iota(jnp.int32, (blk, blk), 0)
    causal = key_off <= lax.broadcasted_iota(jnp.int32, (blk, blk), 1)
    km = km_s[...]
    for hh in range(GROUP):
        slope2 = slope_ref[g * GROUP + hh] * LOG2E
        cols = slice(hh * HEAD_DIM, (hh + 1) * HEAD_DIM)
        gs = jnp.where(fully_past, _dot_nt(km, q_ref[:, cols], precision=lax.Precision.HIGHEST), -jnp.inf)
        rank = jnp.zeros((nb, s_len), jnp.int32)
        for m in range(nb):
            gm = gs[m:m + 1, :]
            tie = (blk_id > m).astype(jnp.int32)
            rank = rank + jnp.where(gm > gs, 1, jnp.where(gm == gs, tie, 0))
        chosen = jnp.logical_and(fully_past, rank < MOBA_TOPK)
        bias = jnp.where(chosen, slope2 * blocks_ahead, NEG)
        for jq in range(nb):
            bias_s[hh, jq] = bias[:, jq * blk:(jq + 1) * blk]
        colb_s[hh] = slope2 * key_off.astype(F32)

    heads = [slice(hh * HEAD_DIM, (hh + 1) * HEAD_DIM) for hh in range(GROUP)]

    def group_scores(gi, j, q2):
        kg = kbg_s[gi]
        s_grp = [_dot_nt(kg, q2[:, cols]) for cols in heads]
        parts, tops = [], []
        for hh in range(GROUP):
            ps = [s_grp[hh][t * blk:(t + 1) * blk, :] + colb_s[hh] + bias_s[hh, j, pl.ds(gi * gt + t, 1), :]
                  for t in range(gt)]
            top = jnp.max(ps[0], axis=0, keepdims=True)
            for part in ps[1:]:
                top = jnp.maximum(top, jnp.max(part, axis=0, keepdims=True))
            parts.append(ps)
            tops.append(top)
        return parts, tops

    def q_block(j, _):
        q2 = q2_s[j]
        kd = kb_s[j]
        vd = vt_s[j]
        s_own = [_dot_nt(kd, q2[:, cols]) for cols in heads]
        parts0, tops0 = group_scores(0, j, q2)
        m_run, probs = [], []
        for hh in range(GROUP):
            s = jnp.where(causal, s_own[hh] + colb_s[hh], NEG)
            m1 = jnp.maximum(jnp.max(s, axis=0, keepdims=True), tops0[hh])
            p_own = jnp.exp2(s - m1).astype(BF16)
            p_grp = jnp.concatenate([jnp.exp2(part - m1).astype(BF16) for part in parts0[hh]], axis=0)
            probs.append((p_own, p_grp))
            m_run.append(m1)
        vg0 = vtg_s[0]
        for hh in range(GROUP):
            acc_s[hh] = _dot(vd, probs[hh][0]) + _dot(vg0, probs[hh][1])

        def kv_group(gi, m_run):
            vg = vtg_s[gi]
            parts, tops = group_scores(gi, j, q2)
            out, probs = [], []
            for hh in range(GROUP):
                m_new = jnp.maximum(m_run[hh], tops[hh])
                probs.append(jnp.concatenate([jnp.exp2(part - m_new).astype(BF16) for part in parts[hh]], axis=0))
                out.append(m_new)
            pv = [_dot(vg, p) for p in probs]
            for hh in range(GROUP):
                acc_s[hh] = jnp.exp2(m_run[hh] - out[hh]) * acc_s[hh] + pv[hh]
            return tuple(out)

        lax.fori_loop(1, (j + gt - 1) // gt, kv_group, tuple(m_run))
        rows = pl.ds(pl.multiple_of(j * blk, blk), blk)
        for hh in range(GROUP):
            acc = acc_s[hh]
            o = acc[0:HEAD_DIM, :] / acc[HEAD_DIM:HEAD_DIM + 1, :]
            o_ref[rows, heads[hh]] = o.T.astype(o_ref.dtype)
        return 0

    lax.fori_loop(0, nb, q_block, 0)


def _attn_prompt(q, k, v):
    b, s, _ = q.shape
    assert s % MOBA_BLOCK == 0
    nb = s // MOBA_BLOCK
    gt = KV_BLOCKS_PER_STEP if nb % KV_BLOCKS_PER_STEP == 0 else 1
    gw = GROUP * HEAD_DIM
    slopes = jnp.asarray(ALIBI_SLOPES, F32)
    kv_spec = pl.BlockSpec((None, s, HEAD_DIM), lambda i, g: (i, 0, g))
    q_spec = pl.BlockSpec((None, s, gw), lambda i, g: (i, 0, g))
    return pl.pallas_call(
        functools.partial(_attn_prompt_kernel, nb=nb, gt=gt),
        grid=(b, N_KV_HEADS),
        in_specs=[pl.BlockSpec(memory_space=pltpu.SMEM), q_spec, kv_spec, kv_spec],
        out_specs=q_spec,
        out_shape=jax.ShapeDtypeStruct(q.shape, BF16),
        scratch_shapes=[pltpu.VMEM((nb, MOBA_BLOCK, HEAD_DIM), BF16), pltpu.VMEM((nb, V_ROWS, MOBA_BLOCK), BF16),
                        pltpu.VMEM((nb // gt, gt * MOBA_BLOCK, HEAD_DIM), BF16),
                        pltpu.VMEM((nb // gt, V_ROWS, gt * MOBA_BLOCK), BF16),
                        pltpu.VMEM((nb, HEAD_DIM), F32), pltpu.VMEM((nb, MOBA_BLOCK, gw), BF16),
                        pltpu.VMEM((GROUP, nb, nb, MOBA_BLOCK), F32), pltpu.VMEM((GROUP, MOBA_BLOCK, MOBA_BLOCK), F32),
                        pltpu.VMEM((GROUP, V_ROWS, MOBA_BLOCK), F32)],
        compiler_params=_params(2),
        name="attn_prompt",
    )(slopes, q, k, v)


def _row_slopes(head):
    out = jnp.zeros(head.shape, F32)
    for h, sl in enumerate(ALIBI_SLOPES):
        out = jnp.where(head == h, sl, out)
    return out


def _attn_sample_kernel(pt_ref, q_ref, kn_ref, vn_ref, *rest, pp, n_blk, past, n_new):
    k_pages, v_pages = rest[:pp], rest[pp:2 * pp]
    o_ref = rest[2 * pp]
    km_s, m_s, l_s, op_s, bias_s = rest[2 * pp + 1:]
    del pt_ref
    s_id = pl.program_id(1)
    nq = q_ref.shape[0]
    kv = N_KV_HEADS
    cols = MOBA_BLOCK * kv
    q = q_ref[...]
    qb = (q * (ATTN_SCALE * LOG2E)).astype(BF16)

    row = lax.broadcasted_iota(jnp.int32, (nq, cols), 0)
    col = lax.broadcasted_iota(jnp.int32, (nq, cols), 1)
    head = row % N_HEADS
    step = row // N_HEADS

    @pl.when(s_id == 0)
    def _():
        same_kv = (col % kv) == head // GROUP
        back = (step + MOBA_BLOCK - col // kv).astype(F32)
        bias_s[...] = jnp.where(same_kv, -(_row_slopes(head) * LOG2E) * back, NEG)
        m_s[...] = jnp.zeros_like(m_s)
        l_s[...] = jnp.zeros_like(l_s)

    slope_col = _row_slopes(head[:, 0:1]) * LOG2E
    lane = lax.broadcasted_iota(jnp.int32, (nq, 128), 1)
    per_step = pp // 2
    first_blk = s_id * per_step
    ksums, scores, probs, partials = [], [], [], []
    for i in range(per_step):
        k0, k1 = k_pages[2 * i][...], k_pages[2 * i + 1][...]
        ksums.append(jnp.sum(k0.reshape(-1, SUBLANES, HEAD_DIM), axis=0)
                     + jnp.sum(k1.reshape(-1, SUBLANES, HEAD_DIM), axis=0))
        kb = jnp.concatenate([k0, k1], axis=0).astype(BF16)
        scores.append(_dot_nt(qb, kb) + bias_s[...])
    for s in scores:
        m = jnp.max(s, axis=-1, keepdims=True)
        p = jnp.exp2(s - m)
        probs.append((m, jnp.sum(p, axis=-1, keepdims=True), p.astype(BF16)))
    for i, (m, l, p) in enumerate(probs):
        vb = jnp.concatenate([v_pages[2 * i][...], v_pages[2 * i + 1][...]], axis=0).astype(BF16)
        partials.append((m, l, _dot(p, vb)))
    m_new, l_new = m_s[...], l_s[...]
    for i, (m, l, o_part) in enumerate(partials):
        n = first_blk + i
        km_s[pl.ds(pl.multiple_of(n * SUBLANES, SUBLANES), SUBLANES), :] = ksums[i] * (1.0 / MOBA_BLOCK)
        op_s[n] = o_part
        m_new = jnp.where(lane == n, m + slope_col * ((n + 1) * MOBA_BLOCK - past).astype(F32), m_new)
        l_new = jnp.where(lane == n, l, l_new)
    m_s[...] = m_new
    l_s[...] = l_new

    @pl.when(s_id == pl.num_programs(1) - 1)
    def _():
        gcols = n_blk * SUBLANES
        gfull = _dot_nt(q, km_s[...], precision=lax.Precision.HIGHEST)
        gsum = gfull + pltpu.roll(gfull, gcols - kv, 1)
        grow = lax.broadcasted_iota(jnp.int32, (nq, gcols), 0)
        gcol = lax.broadcasted_iota(jnp.int32, (nq, gcols), 1)
        gs = jnp.where((gcol % SUBLANES) == (grow % N_HEADS) // GROUP, gsum, -jnp.inf)
        gcol_f = gcol.astype(F32)
        chosen = jnp.zeros((nq, 128), F32)
        for _ in range(MOBA_TOPK):
            best = jnp.max(gs, axis=-1, keepdims=True)
            first = jnp.min(jnp.where(gs == best, gcol_f, float(gcols)), axis=-1, keepdims=True)
            chosen = jnp.where(lane == (first.astype(jnp.int32) // SUBLANES), 1.0, chosen)
            gs = jnp.where(gcol_f == first, -jnp.inf, gs)
        is_chosen = chosen > 0.0

        nk = n_new * kv
        orow = lax.broadcasted_iota(jnp.int32, (nq, nk), 0)
        ocol = lax.broadcasted_iota(jnp.int32, (nq, nk), 1)
        ohead, ostep = orow % N_HEADS, orow // N_HEADS
        ok = jnp.logical_and((ocol % kv) == ohead // GROUP, ocol // kv <= ostep)
        so = _dot_nt(qb, kn_ref[...].astype(BF16)) - (_row_slopes(ohead) * LOG2E) * (ostep - ocol // kv).astype(F32)
        so = jnp.where(ok, so, NEG)

        m_all = jnp.maximum(jnp.max(so, axis=-1, keepdims=True),
                            jnp.max(jnp.where(is_chosen, m_s[...], NEG), axis=-1, keepdims=True))
        wgt = jnp.where(is_chosen, jnp.exp2(m_s[...] - m_all), 0.0)
        po = jnp.exp2(so - m_all)
        l_all = jnp.sum(wgt * l_s[...], axis=-1, keepdims=True) + jnp.sum(po, axis=-1, keepdims=True)
        acc = _dot(po.astype(BF16), vn_ref[...].astype(BF16))
        for n in range(n_blk):
            acc = acc + wgt[:, n:n + 1] * op_s[n]
        o_ref[...] = acc / l_all


def _attn_sample(q, k_new, v_new, cache_k, cache_v, page_ids, n_new):
    b, nq, _ = q.shape
    n_pages = page_ids.shape[1]
    past = n_pages * PAGE_SIZE
    pp = PAGES_PER_STEP
    assert MOBA_BLOCK == 2 * PAGE_SIZE and past % MOBA_BLOCK == 0 and n_pages % pp == 0
    n_blk = past // MOBA_BLOCK
    assert MOBA_TOPK <= n_blk <= 128 and n_new <= MOBA_BLOCK
    rows = PAGE_SIZE * N_KV_HEADS

    def page(i):
        return pl.BlockSpec((None, rows, HEAD_DIM), lambda bi, si, pt: (pt[bi * n_pages + si * pp + i], 0, 0))

    def per_seq(r):
        return pl.BlockSpec((None, r, HEAD_DIM), lambda bi, si, pt: (bi, 0, 0))

    return pl.pallas_call(
        functools.partial(_attn_sample_kernel, pp=pp, n_blk=n_blk, past=past, n_new=n_new),
        grid_spec=pltpu.PrefetchScalarGridSpec(
            num_scalar_prefetch=1,
            grid=(b, n_pages // pp),
            in_specs=[per_seq(nq), per_seq(n_new * N_KV_HEADS), per_seq(n_new * N_KV_HEADS)]
            + [page(i) for i in range(pp)] * 2,
            out_specs=per_seq(nq),
            scratch_shapes=[pltpu.VMEM((n_blk * SUBLANES, HEAD_DIM), F32), pltpu.VMEM((nq, 128), F32),
                            pltpu.VMEM((nq, 128), F32), pltpu.VMEM((n_blk, nq, HEAD_DIM), F32),
                            pltpu.VMEM((nq, MOBA_BLOCK * N_KV_HEADS), F32)]),
        out_shape=jax.ShapeDtypeStruct((b, nq, HEAD_DIM), F32),
        compiler_params=_params(2),
        name="attn_sample",
    )(page_ids.reshape(-1), q, k_new, v_new, *([cache_k] * pp), *([cache_v] * pp))


def _merge_ffn_kernel(x_ref, ya_ref, o_ref_in, sgb_ref, g2_ref, sh_ref, sc_ref, g3_ref, ng_ref,
                      wbb_ref, wout_ref, wg_ref, wu_ref, wd_ref, out_ref, *, chunk):
    yb = _dot(o_ref_in[...].astype(BF16), wbb_ref[...])
    merged = (ya_ref[...] + sgb_ref[...] * yb).astype(BF16)
    x = x_ref[...] + g2_ref[...] * _dot(merged, wout_ref[...])
    h = (_rms(x, ng_ref[...]) * (1.0 + sc_ref[...]) + sh_ref[...]).astype(BF16)
    out_ref[...] = x + 0.5 * g3_ref[...] * _swiglu_acc(h, wg_ref, wu_ref, wd_ref, chunk)


def _merge_ffn(x, ya, o, sgb, g2, sh, sc, g3, ng, wbb, wout, wg, wu, wd, tm, tiles_per_group):
    t, d = x.shape
    row = pl.BlockSpec((tm, d), lambda i: (i, 0))
    mod = functools.partial(_mod_spec, tm=tm, tiles_per_group=tiles_per_group)
    return pl.pallas_call(
        functools.partial(_merge_ffn_kernel, chunk=_ffn_chunk(wg.shape[1])),
        grid=(t // tm,),
        in_specs=[row, row, pl.BlockSpec((tm, o.shape[1]), lambda i: (i, 0)), row,
                  mod(g2), mod(sh), mod(sc), mod(g3), _resident((1, d)),
                  _resident(wbb.shape), _resident(wout.shape), _resident(wg.shape), _resident(wu.shape),
                  _resident(wd.shape)],
        out_specs=row,
        out_shape=jax.ShapeDtypeStruct((t, d), F32),
        compiler_params=_params(1),
        name="merge_ffn",
    )(x, ya, o, sgb, g2, sh, sc, g3, ng, wbb, wout, wg, wu, wd)


def _token_tile(t):
    return 512 if t % 512 == 0 else t


def _layer_weights(p):
    w = dict(p)
    for name in ("ffn1_gate", "ffn1_up", "ffn1_down", "w_in", "w_branch_a", "w_branch_b", "w_out",
                 "ffn2_gate", "ffn2_up", "ffn2_down"):
        w[name] = p[name].astype(BF16)
    w["rg_w"] = jnp.concatenate([p["rg_wa"], p["rg_wx"]], axis=-1).astype(BF16)
    for name in ("norm1_g", "norm2_g", "norm3_g", "conv_b", "rg_ba", "rg_bx", "rg_lambda", "q_norm_g", "k_norm_g"):
        w[name] = p[name].reshape(1, -1)
    return w


def _prompt_layer(x, mod, w):
    b, s, d = x.shape
    t = b * s
    tm = _token_tile(s)
    tpg = s // tm
    sh1, sc1, g1, sh2, sc2, g2, sh3, sc3, g3 = [m[:, None, :] for m in jnp.split(mod, 3 * N_SUBLAYERS, axis=-1)]
    x1 = _ffn(x.reshape(t, d), sh1, sc1, g1, w["norm1_g"], w["ffn1_gate"], w["ffn1_up"], w["ffn1_down"], tm, tpg)
    d_rnn = w["w_branch_a"].shape[0]
    q, k, v, sgb, ya, new_conv, h_last = _mixin_prompt(
        x1.reshape(b, s, d), sh2, sc2, w["norm2_g"], w["w_in"], w["q_norm_g"], w["k_norm_g"],
        jnp.zeros((b, CONV_WIDTH - 1, d_rnn), F32), jnp.zeros((b, 1, d_rnn), F32),
        w["conv_w"], w["conv_b"], w["rg_w"], w["rg_ba"], w["rg_bx"], w["rg_lambda"], w["w_branch_a"], tm)
    sgb = sgb.reshape(t, d)
    o = _attn_prompt(q.reshape(b, s, -1), k.reshape(b, s, -1), v.reshape(b, s, -1))
    y = _merge_ffn(x1, ya.reshape(t, d), o.reshape(t, -1), sgb, g2, sh3, sc3, g3, w["norm3_g"],
                   w["w_branch_b"], w["w_out"], w["ffn2_gate"], w["ffn2_up"], w["ffn2_down"], tm, tpg)
    return (y.reshape(b, s, d), k.reshape(b, s, N_KV_HEADS, HEAD_DIM), v.reshape(b, s, N_KV_HEADS, HEAD_DIM),
            new_conv, h_last.reshape(b, d_rnn))


def _sample_layer(x, mod, conv_buf, h0, cache_k, cache_v, page_ids, w):
    b, s, d = x.shape
    t = b * s
    rep = lambda m: jnp.repeat(m, s, axis=0)[None]
    sh1, sc1, g1, sh2, sc2, g2, sh3, sc3, g3 = [rep(m) for m in jnp.split(mod, 3 * N_SUBLAYERS, axis=-1)]
    x1 = _ffn(x.reshape(t, d), sh1, sc1, g1, w["norm1_g"], w["ffn1_gate"], w["ffn1_up"], w["ffn1_down"], t, 1)
    xr, ug, q, k, v, sga, sgb = _inproj(x1, sh2, sc2, w["norm2_g"], w["w_in"], w["q_norm_g"], w["k_norm_g"], t, 1)
    d_rnn = xr.shape[1]
    xp = jnp.concatenate([conv_buf, xr.reshape(b, s, d_rnn)], axis=1)
    taps = jnp.stack([xp[:, kk:kk + s].reshape(t, d_rnn) for kk in range(CONV_WIDTH)])
    ya, h_all = _rnn_sample(taps, ug, sga, jnp.repeat(h0, s, axis=0), w["conv_w"], w["conv_b"], w["rg_w"],
                            w["rg_ba"], w["rg_bx"], w["rg_lambda"], w["w_branch_a"], s)
    o = _attn_sample(q.reshape(b, s * N_HEADS, HEAD_DIM), k.reshape(b, s * N_KV_HEADS, HEAD_DIM),
                     v.reshape(b, s * N_KV_HEADS, HEAD_DIM), cache_k, cache_v, page_ids, s)
    y = _merge_ffn(x1, ya, o.reshape(t, -1), sgb, g2, sh3, sc3, g3, w["norm3_g"],
                   w["w_branch_b"], w["w_out"], w["ffn2_gate"], w["ffn2_up"], w["ffn2_down"], t, 1)
    return (y.reshape(b, s, d), k.reshape(b, s, N_KV_HEADS, HEAD_DIM), v.reshape(b, s, N_KV_HEADS, HEAD_DIM),
            xp[:, -(CONV_WIDTH - 1):], h_all.reshape(b, s, d_rnn)[:, -1])


def kernel(x_prompt, x_sample, c_prompt, c_sample, cache_k, cache_v, state_conv, state_rglru, page_table, w_ada, b_ada, norm1_g, ffn1_gate, ffn1_up, ffn1_down, norm2_g, w_in, conv_w, conv_b, rg_wa, rg_ba, rg_wx, rg_bx, rg_lambda, q_norm_g, k_norm_g, w_branch_a, w_branch_b, w_out, norm3_g, ffn2_gate, ffn2_up, ffn2_down):
    depth, n_pool = cache_k.shape[0], cache_k.shape[1]
    bp = x_prompt.shape[0]
    assert cache_k.shape[2:] == (PAGE_SIZE, N_KV_HEADS, HEAD_DIM)
    ck = cache_k.reshape(depth * n_pool, PAGE_SIZE * N_KV_HEADS, HEAD_DIM)
    cv = cache_v.reshape(depth * n_pool, PAGE_SIZE * N_KV_HEADS, HEAD_DIM)
    c_all = jnp.concatenate([c_prompt, c_sample], axis=0)
    pad = -c_all.shape[0] % SUBLANES
    c_all = jnp.pad(c_all, ((0, pad), (0, 0)))
    yp, ys = x_prompt, x_sample
    outs = [[] for _ in range(8)]
    for l in range(depth):
        p = dict(norm1_g=norm1_g[l], ffn1_gate=ffn1_gate[l], ffn1_up=ffn1_up[l], ffn1_down=ffn1_down[l],
                 norm2_g=norm2_g[l], w_in=w_in[l], conv_w=conv_w[l], conv_b=conv_b[l], rg_wa=rg_wa[l],
                 rg_ba=rg_ba[l], rg_wx=rg_wx[l], rg_bx=rg_bx[l], rg_lambda=rg_lambda[l], q_norm_g=q_norm_g[l],
                 k_norm_g=k_norm_g[l], w_branch_a=w_branch_a[l], w_branch_b=w_branch_b[l], w_out=w_out[l],
                 norm3_g=norm3_g[l], ffn2_gate=ffn2_gate[l], ffn2_up=ffn2_up[l], ffn2_down=ffn2_down[l])
        w = _layer_weights(p)
        mod = _ada(c_all, w_ada[l], b_ada[l])
        yp, kp, vp, cp, hp = _prompt_layer(yp, mod[:bp], w)
        ys, ks, vs, cs, hs = _sample_layer(ys, mod[bp:bp + x_sample.shape[0]], state_conv[l], state_rglru[l],
                                           ck, cv, page_table + l * n_pool, w)
        for lst, val in zip(outs, (kp, vp, cp, hp, ks, vs, cs, hs)):
            lst.append(val)
    return (yp, ys) + tuple(jnp.stack(o) for o in outs)
```

```python
import functools

import jax
import jax.numpy as jnp
from jax import lax
from jax.experimental import pallas as pl
from jax.experimental.pallas import tpu as pltpu

F32 = jnp.float32
BF16 = jnp.bfloat16

N_HEADS = 8
N_KV_HEADS = 4
HEAD_DIM = 128
GROUP = N_HEADS // N_KV_HEADS
MOBA_BLOCK = 256
MOBA_TOPK = 3
PAGE_SIZE = 128
N_RNN_BLOCKS = 8
RNN_BLOCK = 128
CONV_WIDTH = 4
RG_C = 8.0
N_SUBLAYERS = 3
EPS = 1e-6
ALIBI_SLOPES = tuple(2.0 ** (-8.0 * (h + 1) / N_HEADS) for h in range(N_HEADS))
ATTN_SCALE = HEAD_DIM ** -0.5
NEG = -1e30
LOG2E = 1.4426950408889634
SUBLANES = 8
VMEM_LIMIT = 56 * 1024 * 1024
PAGES_PER_STEP = 32
KV_BLOCKS_PER_STEP = 4
GATE_COST = 40
MXU_COLS = 256


def _params(n_axes):
    return pltpu.CompilerParams(dimension_semantics=("arbitrary",) * n_axes, vmem_limit_bytes=VMEM_LIMIT)


def _resident(shape):
    nd = len(shape)
    return pl.BlockSpec(shape, lambda *_: (0,) * nd, pipeline_mode=pl.Buffered(1))


def _dot(a, b):
    return jnp.dot(a, b, preferred_element_type=F32)


def _dot_nt(a, b, precision=None):
    return lax.dot_general(a, b, (((1,), (1,)), ((), ())), precision=precision, preferred_element_type=F32)


def _rms(x, g):
    return x * lax.rsqrt(jnp.mean(x * x, axis=-1, keepdims=True) + EPS) * g


def _sigmoid(x):
    return 0.5 * jnp.tanh(0.5 * x) + 0.5


def _silu(x):
    return x * _sigmoid(x)


def _ada_kernel(c_ref, w_ref, b_ref, o_ref):
    a = _silu(c_ref[...]).astype(BF16)
    o_ref[...] = _dot(a, w_ref[...].astype(BF16)) + b_ref[...]


def _ada(c, w, b):
    m, d = c.shape
    n = w.shape[1]
    tn = n // 8
    return pl.pallas_call(
        _ada_kernel,
        grid=(n // tn,),
        in_specs=[pl.BlockSpec((m, d), lambda i: (0, 0)),
                  pl.BlockSpec((d, tn), lambda i: (0, i)),
                  pl.BlockSpec((1, tn), lambda i: (0, i))],
        out_specs=pl.BlockSpec((m, tn), lambda i: (0, i)),
        out_shape=jax.ShapeDtypeStruct((m, n), F32),
        compiler_params=_params(1),
        name="ada",
    )(c, w, b.reshape(1, n))


def _swiglu_acc(h, wg_ref, wu_ref, wd_ref, chunk):
    d_ff = wg_ref.shape[1]
    acc = None
    for c in range(d_ff // chunk):
        sl = slice(c * chunk, (c + 1) * chunk)
        act = (_silu(_dot(h, wg_ref[:, sl])) * _dot(h, wu_ref[:, sl])).astype(BF16)
        part = _dot(act, wd_ref[sl, :])
        acc = part if acc is None else acc + part
    return acc


def _ffn_kernel(x_ref, sh_ref, sc_ref, g_ref, ng_ref, wg_ref, wu_ref, wd_ref, o_ref, *, chunk):
    x = x_ref[...]
    h = (_rms(x, ng_ref[...]) * (1.0 + sc_ref[...]) + sh_ref[...]).astype(BF16)
    o_ref[...] = x + 0.5 * g_ref[...] * _swiglu_acc(h, wg_ref, wu_ref, wd_ref, chunk)


def _mod_spec(mod, tm, tiles_per_group):
    arr, chunk = mod
    r, d = arr.shape[1], arr.shape[2] // (3 * N_SUBLAYERS)
    return pl.BlockSpec((None, r, d), lambda i: (i // tiles_per_group, 0, chunk))


def _ffn_chunk(d_ff):
    return 256 if d_ff % 256 == 0 else 128


def _ffn(x, sh, sc, g, ng, wg, wu, wd, tm, tiles_per_group):
    t, d = x.shape
    d_ff = wg.shape[1]
    row = pl.BlockSpec((tm, d), lambda i: (i, 0))
    return pl.pallas_call(
        functools.partial(_ffn_kernel, chunk=_ffn_chunk(d_ff)),
        grid=(t // tm,),
        in_specs=[row, _mod_spec(sh, tm, tiles_per_group), _mod_spec(sc, tm, tiles_per_group),
                  _mod_spec(g, tm, tiles_per_group), _resident((1, d)),
                  _resident(wg.shape), _resident(wu.shape), _resident(wd.shape)],
        out_specs=row,
        out_shape=jax.ShapeDtypeStruct((t, d), F32),
        compiler_params=_params(1),
        name="ffn",
    )(x, sh[0], sc[0], g[0], ng, wg, wu, wd)


def _inproj_kernel(x_ref, sh_ref, sc_ref, ng_ref, w_ref, qg_ref, kg_ref,
                   xr_ref, ug_ref, q_ref, k_ref, v_ref, sga_ref, sgb_ref):
    d_rnn = xr_ref.shape[1]
    dq = q_ref.shape[1]
    dk = k_ref.shape[1]
    x = x_ref[...]
    h = (_rms(x, ng_ref[...]) * (1.0 + sc_ref[...]) + sh_ref[...]).astype(BF16)
    o = 0
    xr_ref[...] = _dot(h, w_ref[:, o:o + d_rnn])
    o += d_rnn
    ug_ref[...] = jax.nn.gelu(_dot(h, w_ref[:, o:o + d_rnn]))
    o += d_rnn
    for hd in range(dq // HEAD_DIM):
        sl = slice(hd * HEAD_DIM, (hd + 1) * HEAD_DIM)
        q_ref[:, sl] = _rms(_dot(h, w_ref[:, o + hd * HEAD_DIM:o + (hd + 1) * HEAD_DIM]), qg_ref[...])
    o += dq
    for hd in range(dk // HEAD_DIM):
        sl = slice(hd * HEAD_DIM, (hd + 1) * HEAD_DIM)
        k_ref[:, sl] = _rms(_dot(h, w_ref[:, o + hd * HEAD_DIM:o + (hd + 1) * HEAD_DIM]), kg_ref[...])
    o += dk
    v_ref[...] = _dot(h, w_ref[:, o:o + dk])
    o += dk
    d = sga_ref.shape[1]
    sga_ref[...] = _sigmoid(_dot(h, w_ref[:, o:o + d]))
    o += d
    sgb_ref[...] = _sigmoid(_dot(h, w_ref[:, o:o + d]))


def _inproj(x, sh, sc, ng, w_in, qg, kg, tm, tiles_per_group):
    t, d = x.shape
    dq, dk = N_HEADS * HEAD_DIM, N_KV_HEADS * HEAD_DIM
    d_rnn = (w_in.shape[1] - dq - 2 * dk - 2 * d) // 2

    def row(w):
        return pl.BlockSpec((tm, w), lambda i: (i, 0))

    widths = (d_rnn, d_rnn, dq, dk, dk, d, d)
    return pl.pallas_call(
        _inproj_kernel,
        grid=(t // tm,),
        in_specs=[row(d), _mod_spec(sh, tm, tiles_per_group), _mod_spec(sc, tm, tiles_per_group),
                  _resident((1, d)), _resident(w_in.shape), _resident((1, HEAD_DIM)), _resident((1, HEAD_DIM))],
        out_specs=[row(w) for w in widths],
        out_shape=[jax.ShapeDtypeStruct((t, w), F32) for w in widths],
        compiler_params=_params(1),
        name="inproj",
    )(x, sh[0], sc[0], ng, w_in, qg, kg)


def _rg_gate_block(xcn, n, wg_ref, bra_ref, brx_ref, sp, a_ref, b_ref):
    sl = slice(n * RNN_BLOCK, (n + 1) * RNN_BLOCK)
    z = _dot(xcn.astype(BF16), wg_ref[n])
    r = _sigmoid(z[:, :RNN_BLOCK] + bra_ref[:, sl])
    i = _sigmoid(z[:, RNN_BLOCK:] + brx_ref[:, sl])
    log_a = -RG_C * r * sp[:, sl]
    a = jnp.exp(log_a)
    a_ref[:, sl] = a
    b_ref[:, sl] = jnp.sqrt(1.0 - a * a) * (i * xcn)


def _rg_gates(xc, wg_ref, bra_ref, brx_ref, lam_ref, a_ref, b_ref):
    sp = jax.nn.softplus(-lam_ref[...])
    for n in range(N_RNN_BLOCKS):
        _rg_gate_block(xc[:, n * RNN_BLOCK:(n + 1) * RNN_BLOCK], n, wg_ref, bra_ref, brx_ref, sp, a_ref, b_ref)


def _scan_rows(a, b, row_in_seg, steps):
    for d in steps:
        keep = row_in_seg >= d
        a_prev = jnp.where(keep, pltpu.roll(a, d, 0), 1.0)
        b_prev = jnp.where(keep, pltpu.roll(b, d, 0), 0.0)
        b = a * b_prev + b
        a = a * a_prev
    return a, b


def _mixin_prompt_kernel(x_ref, sh_ref, sc_ref, ng_ref, w_ref, qg_ref, kg_ref, conv0_ref, h0_ref, cw_ref, cb_ref,
                         wg_ref, bra_ref, brx_ref, lam_ref, wba_ref,
                         q_ref, k_ref, v_ref, sgb_ref, ya_ref, k4_ref, v4_ref, nconv_ref, hlast_ref,
                         xbuf, a_s, b_s, hc_s, ug_s, sga_s, *, ts):
    s = pl.program_id(1)
    w = xbuf.shape[1]
    dq = q_ref.shape[1]
    dk = k_ref.shape[1]
    d = sgb_ref.shape[1]
    tail = CONV_WIDTH - 1
    lo = SUBLANES - tail

    @pl.when(s == 0)
    def _():
        xbuf[lo:SUBLANES, :] = conv0_ref[...]
        hc_s[...] = jnp.broadcast_to(h0_ref[...], hc_s.shape)

    x = x_ref[...]
    h = (_rms(x, ng_ref[...]) * (1.0 + sc_ref[...]) + sh_ref[...]).astype(BF16)
    o_gate, o_q = w, 2 * w
    o_k = o_q + dq
    o_v = o_k + dk
    o_ga = o_v + dk
    o_gb = o_ga + d
    cw_ = MXU_COLS

    def proj(lo):
        return _dot(h, w_ref[:, lo:lo + cw_])

    def normed_heads(ref, gain_ref, base, c):
        z = proj(base + c * cw_)
        for i in range(cw_ // HEAD_DIM):
            col = c * cw_ + i * HEAD_DIM
            ref[:, col:col + HEAD_DIM] = _rms(z[:, i * HEAD_DIM:(i + 1) * HEAD_DIM], gain_ref[...])

    def store(ref, base, c, fn):
        ref[:, c * cw_:(c + 1) * cw_] = fn(proj(base + c * cw_))

    jobs = [functools.partial(normed_heads, q_ref, qg_ref, o_q, c) for c in range(dq // cw_)]
    jobs += [functools.partial(normed_heads, k_ref, kg_ref, o_k, c) for c in range(dk // cw_)]
    jobs += [functools.partial(store, v_ref, o_v, c, lambda z: z) for c in range(dk // cw_)]
    jobs += [functools.partial(store, sgb_ref, o_gb, c, _sigmoid) for c in range(d // cw_)]
    jobs += [functools.partial(store, ug_s, o_gate, c, jax.nn.gelu) for c in range(w // cw_)]
    jobs += [functools.partial(store, sga_s, o_ga, c, _sigmoid) for c in range(d // cw_)]

    def by_head(src_ref, dst_ref, hd):
        dst_ref[:, hd, :] = src_ref[:, hd * HEAD_DIM:(hd + 1) * HEAD_DIM]

    jobs += [functools.partial(by_head, k_ref, k4_ref, hd) for hd in range(dk // HEAD_DIM)]
    jobs += [functools.partial(by_head, v_ref, v4_ref, hd) for hd in range(dk // HEAD_DIM)]
    cost = [GATE_COST] * N_RNN_BLOCKS + [1] * (ts // SUBLANES)
    due, acc_cost = [], 0
    for c_ in cost:
        acc_cost += c_
        due.append((acc_cost * len(jobs)) // sum(cost))
    done = 0

    for c in range(w // cw_):
        xbuf[SUBLANES:SUBLANES + ts, c * cw_:(c + 1) * cw_] = proj(c * cw_)
    sp = jax.nn.softplus(-lam_ref[...])
    for n in range(N_RNN_BLOCKS):
        sl = slice(n * RNN_BLOCK, (n + 1) * RNN_BLOCK)
        xcn = cb_ref[:, sl] + cw_ref[0:1, sl] * xbuf[lo:lo + ts, sl]
        for k in range(1, CONV_WIDTH):
            xcn = xcn + cw_ref[k:k + 1, sl] * xbuf[lo + k:lo + k + ts, sl]
        _rg_gate_block(xcn, n, wg_ref, bra_ref, brx_ref, sp, a_s, b_s)
        while done < due[n]:
            jobs[done]()
            done += 1
    new_tail = xbuf[ts + lo:ts + SUBLANES, :]
    nconv_ref[...] = new_tail
    xbuf[lo:SUBLANES, :] = new_tail

    row = lax.broadcasted_iota(jnp.int32, (SUBLANES, w), 0)
    hc = hc_s[...]
    for c in range(ts // SUBLANES):
        rows = slice(c * SUBLANES, (c + 1) * SUBLANES)
        a_cum, b_cum = _scan_rows(a_s[rows, :], b_s[rows, :], row, (1, 2, 4))
        hs = a_cum * hc + b_cum
        a_s[rows, :] = hs
        hc = jnp.broadcast_to(hs[SUBLANES - 1:SUBLANES, :], (SUBLANES, w))
        while done < due[N_RNN_BLOCKS + c]:
            jobs[done]()
            done += 1
    hc_s[...] = hc
    hlast_ref[...] = hc[0:1, :]
    u = (a_s[...] * ug_s[...]).astype(BF16)
    ya_ref[...] = sga_s[...] * _dot(u, wba_ref[...])


def _mixin_prompt(x, sh, sc, ng, w_in, qg, kg, conv0, h0, cw, cb, wg, bra, brx, lam, wba, ts):
    b, s, d = x.shape
    w = wba.shape[0]
    dq, dk = N_HEADS * HEAD_DIM, N_KV_HEADS * HEAD_DIM
    tail = CONV_WIDTH - 1

    def seq(width):
        return pl.BlockSpec((None, ts, width), lambda i, j: (i, j, 0))

    def per_seq(rows, width):
        return pl.BlockSpec((None, rows, width), lambda i, j: (i, 0, 0))

    def mod_row(mod):
        return pl.BlockSpec((None, 1, d), lambda i, j: (i, 0, mod[1]))

    widths = (dq, dk, dk, d, d)
    by_head = pl.BlockSpec((None, ts, N_KV_HEADS, HEAD_DIM), lambda i, j: (i, j, 0, 0))
    return pl.pallas_call(
        functools.partial(_mixin_prompt_kernel, ts=ts),
        grid=(b, s // ts),
        in_specs=[seq(d), mod_row(sh), mod_row(sc), _resident((1, d)), _resident(w_in.shape),
                  _resident((1, HEAD_DIM)), _resident((1, HEAD_DIM)), per_seq(tail, w), per_seq(1, w),
                  _resident(cw.shape), _resident((1, w)), _resident(wg.shape), _resident((1, w)), _resident((1, w)),
                  _resident((1, w)), _resident(wba.shape)],
        out_specs=[seq(wd) for wd in widths] + [by_head, by_head, per_seq(tail, w), per_seq(1, w)],
        out_shape=[jax.ShapeDtypeStruct((b, s, wd), F32) for wd in widths]
        + [jax.ShapeDtypeStruct((b, s, N_KV_HEADS, HEAD_DIM), F32)] * 2
        + [jax.ShapeDtypeStruct((b, tail, w), F32), jax.ShapeDtypeStruct((b, 1, w), F32)],
        scratch_shapes=[pltpu.VMEM((ts + SUBLANES, w), F32), pltpu.VMEM((ts, w), F32), pltpu.VMEM((ts, w), F32),
                        pltpu.VMEM((SUBLANES, w), F32), pltpu.VMEM((ts, w), F32), pltpu.VMEM((ts, d), F32)],
        compiler_params=_params(2),
        name="mixin_prompt",
    )(x, sh[0], sc[0], ng, w_in, qg, kg, conv0, h0, cw, cb, wg, bra, brx, lam, wba)


def _rnn_sample_kernel(xs_ref, ug_ref, sga_ref, h0_ref, cw_ref, cb_ref, wg_ref, bra_ref, brx_ref, lam_ref, wba_ref,
                       ya_ref, h_ref, a_s, b_s, *, seg):
    m, w = ug_ref.shape
    xc = cb_ref[...] + cw_ref[0:1, :] * xs_ref[0]
    for k in range(1, CONV_WIDTH):
        xc = xc + cw_ref[k:k + 1, :] * xs_ref[k]
    _rg_gates(xc, wg_ref, bra_ref, brx_ref, lam_ref, a_s, b_s)
    row_in_seg = lax.broadcasted_iota(jnp.int32, (SUBLANES, w), 0) % seg
    steps = tuple(d for d in (1, 2, 4) if d < seg)
    for c in range(m // SUBLANES):
        rows = slice(c * SUBLANES, (c + 1) * SUBLANES)
        a_cum, b_cum = _scan_rows(a_s[rows, :], b_s[rows, :], row_in_seg, steps)
        h_ref[rows, :] = a_cum * h0_ref[rows, :] + b_cum
    u = (h_ref[...] * ug_ref[...]).astype(BF16)
    ya_ref[...] = sga_ref[...] * _dot(u, wba_ref[...])


def _rnn_sample(xs, ug, sga, h0_rows, cw, cb, wg, bra, brx, lam, wba, seg):
    m, w = ug.shape
    d = wba.shape[1]
    return pl.pallas_call(
        functools.partial(_rnn_sample_kernel, seg=seg),
        out_shape=[jax.ShapeDtypeStruct((m, d), F32), jax.ShapeDtypeStruct((m, w), F32)],
        scratch_shapes=[pltpu.VMEM((m, w), F32), pltpu.VMEM((m, w), F32)],
        compiler_params=pltpu.CompilerParams(vmem_limit_bytes=VMEM_LIMIT),
        name="rnn_sample",
    )(xs, ug, sga, h0_rows, cw, cb, wg, bra, brx, lam, wba)


V_ROWS = HEAD_DIM + 16


def _attn_prompt_kernel(slope_ref, q_ref, k_ref, v_ref, o_ref, kb_s, vt_s, kbg_s, vtg_s, km_s, q2_s, bias_s, colb_s,
                        acc_s, *, nb, gt):
    g = pl.program_id(1)
    blk = MOBA_BLOCK
    s_len = nb * blk
    q_scale = ATTN_SCALE * LOG2E

    ones_rows = (lax.broadcasted_iota(jnp.int32, (V_ROWS - HEAD_DIM, blk), 0) == 0).astype(BF16)
    for n in range(nb):
        rows = slice(n * blk, (n + 1) * blk)
        grp, sub = n // gt, slice((n % gt) * blk, (n % gt + 1) * blk)
        kn = k_ref[rows, :]
        kb = kn.astype(BF16)
        vt = v_ref[rows, :].T.astype(BF16)
        kb_s[n] = kb
        kbg_s[grp, sub, :] = kb
        vt_s[n, 0:HEAD_DIM, :] = vt
        vt_s[n, HEAD_DIM:V_ROWS, :] = ones_rows
        vtg_s[grp, 0:HEAD_DIM, sub] = vt
        vtg_s[grp, HEAD_DIM:V_ROWS, sub] = ones_rows
        km_s[n:n + 1, :] = jnp.sum(kn, axis=0, keepdims=True) * (1.0 / blk)
        q2_s[n] = (q_ref[rows, :] * q_scale).astype(BF16)

    blk_id = lax.broadcasted_iota(jnp.int32, (nb, s_len), 0)
    q_blk = lax.broadcasted_iota(jnp.int32, (nb, s_len), 1) // blk
    fully_past = blk_id < q_blk
    blocks_ahead = ((blk_id - q_blk) * blk).astype(F32)
    key_off = lax.broadcasted_iota(jnp.int32, (blk, blk), 0)
    causal = key_off <= lax.broadcasted_iota(jnp.int32, (blk, blk), 1)
    km = km_s[...]
    for hh in range(GROUP):
        slope2 = slope_ref[g * GROUP + hh] * LOG2E
        cols = slice(hh * HEAD_DIM, (hh + 1) * HEAD_DIM)
        gs = jnp.where(fully_past, _dot_nt(km, q_ref[:, cols], precision=lax.Precision.HIGHEST), -jnp.inf)
        rank = jnp.zeros((nb, s_len), jnp.int32)
        for m in range(nb):
            gm = gs[m:m + 1, :]
            tie = (blk_id > m).astype(jnp.int32)
            rank = rank + jnp.where(gm > gs, 1, jnp.where(gm == gs, tie, 0))
        chosen = jnp.logical_and(fully_past, rank < MOBA_TOPK)
        bias = jnp.where(chosen, slope2 * blocks_ahead, NEG)
        for jq in range(nb):
            bias_s[hh, jq] = bias[:, jq * blk:(jq + 1) * blk]
        colb_s[hh] = slope2 * key_off.astype(F32)

    heads = [slice(hh * HEAD_DIM, (hh + 1) * HEAD_DIM) for hh in range(GROUP)]

    def group_scores(gi, j, q2):
        kg = kbg_s[gi]
        s_grp = [_dot_nt(kg, q2[:, cols]) for cols in heads]
        parts, tops = [], []
        for hh in range(GROUP):
            ps = [s_grp[hh][t * blk:(t + 1) * blk, :] + colb_s[hh] + bias_s[hh, j, pl.ds(gi * gt + t, 1), :]
                  for t in range(gt)]
            top = jnp.max(ps[0], axis=0, keepdims=True)
            for part in ps[1:]:
                top = jnp.maximum(top, jnp.max(part, axis=0, keepdims=True))
            parts.append(ps)
            tops.append(top)
        return parts, tops

    def q_block(j, _):
        q2 = q2_s[j]
        kd = kb_s[j]
        vd = vt_s[j]
        s_own = [_dot_nt(kd, q2[:, cols]) for cols in heads]
        parts0, tops0 = group_scores(0, j, q2)
        m_run, probs = [], []
        for hh in range(GROUP):
            s = jnp.where(causal, s_own[hh] + colb_s[hh], NEG)
            m1 = jnp.maximum(jnp.max(s, axis=0, keepdims=True), tops0[hh])
            p_own = jnp.exp2(s - m1).astype(BF16)
            p_grp = jnp.concatenate([jnp.exp2(part - m1).astype(BF16) for part in parts0[hh]], axis=0)
            probs.append((p_own, p_grp))
            m_run.append(m1)
        vg0 = vtg_s[0]
        for hh in range(GROUP):
            acc_s[hh] = _dot(vd, probs[hh][0]) + _dot(vg0, probs[hh][1])

        def kv_group(gi, m_run):
            vg = vtg_s[gi]
            parts, tops = group_scores(gi, j, q2)
            out, probs = [], []
            for hh in range(GROUP):
                m_new = jnp.maximum(m_run[hh], tops[hh])
                probs.append(jnp.concatenate([jnp.exp2(part - m_new).astype(BF16) for part in parts[hh]], axis=0))
                out.append(m_new)
            pv = [_dot(vg, p) for p in probs]
            for hh in range(GROUP):
                acc_s[hh] = jnp.exp2(m_run[hh] - out[hh]) * acc_s[hh] + pv[hh]
            return tuple(out)

        lax.fori_loop(1, (j + gt - 1) // gt, kv_group, tuple(m_run))
        rows = pl.ds(pl.multiple_of(j * blk, blk), blk)
        for hh in range(GROUP):
            acc = acc_s[hh]
            o = acc[0:HEAD_DIM, :] / acc[HEAD_DIM:HEAD_DIM + 1, :]
            o_ref[rows, heads[hh]] = o.T.astype(o_ref.dtype)
        return 0

    lax.fori_loop(0, nb, q_block, 0)


def _attn_prompt(q, k, v):
    b, s, _ = q.shape
    assert s % MOBA_BLOCK == 0
    nb = s // MOBA_BLOCK
    gt = KV_BLOCKS_PER_STEP if nb % KV_BLOCKS_PER_STEP == 0 else 1
    gw = GROUP * HEAD_DIM
    slopes = jnp.asarray(ALIBI_SLOPES, F32)
    kv_spec = pl.BlockSpec((None, s, HEAD_DIM), lambda i, g: (i, 0, g))
    q_spec = pl.BlockSpec((None, s, gw), lambda i, g: (i, 0, g))
    return pl.pallas_call(
        functools.partial(_attn_prompt_kernel, nb=nb, gt=gt),
        grid=(b, N_KV_HEADS),
        in_specs=[pl.BlockSpec(memory_space=pltpu.SMEM), q_spec, kv_spec, kv_spec],
        out_specs=q_spec,
        out_shape=jax.ShapeDtypeStruct(q.shape, BF16),
        scratch_shapes=[pltpu.VMEM((nb, MOBA_BLOCK, HEAD_DIM), BF16), pltpu.VMEM((nb, V_ROWS, MOBA_BLOCK), BF16),
                        pltpu.VMEM((nb // gt, gt * MOBA_BLOCK, HEAD_DIM), BF16),
                        pltpu.VMEM((nb // gt, V_ROWS, gt * MOBA_BLOCK), BF16),
                        pltpu.VMEM((nb, HEAD_DIM), F32), pltpu.VMEM((nb, MOBA_BLOCK, gw), BF16),
                        pltpu.VMEM((GROUP, nb, nb, MOBA_BLOCK), F32), pltpu.VMEM((GROUP, MOBA_BLOCK, MOBA_BLOCK), F32),
                        pltpu.VMEM((GROUP, V_ROWS, MOBA_BLOCK), F32)],
        compiler_params=_params(2),
        name="attn_prompt",
    )(slopes, q, k, v)


def _row_slopes(head):
    out = jnp.zeros(head.shape, F32)
    for h, sl in enumerate(ALIBI_SLOPES):
        out = jnp.where(head == h, sl, out)
    return out


def _attn_sample_kernel(pt_ref, q_ref, kn_ref, vn_ref, *rest, pp, n_blk, past, n_new):
    k_pages, v_pages = rest[:pp], rest[pp:2 * pp]
    o_ref = rest[2 * pp]
    km_s, m_s, l_s, op_s, bias_s = rest[2 * pp + 1:]
    del pt_ref
    s_id = pl.program_id(1)
    nq = q_ref.shape[0]
    kv = N_KV_HEADS
    cols = MOBA_BLOCK * kv
    q = q_ref[...]
    qb = (q * (ATTN_SCALE * LOG2E)).astype(BF16)

    row = lax.broadcasted_iota(jnp.int32, (nq, cols), 0)
    col = lax.broadcasted_iota(jnp.int32, (nq, cols), 1)
    head = row % N_HEADS
    step = row // N_HEADS

    @pl.when(s_id == 0)
    def _():
        same_kv = (col % kv) == head // GROUP
        back = (step + MOBA_BLOCK - col // kv).astype(F32)
        bias_s[...] = jnp.where(same_kv, -(_row_slopes(head) * LOG2E) * back, NEG)
        m_s[...] = jnp.zeros_like(m_s)
        l_s[...] = jnp.zeros_like(l_s)

    slope_col = _row_slopes(head[:, 0:1]) * LOG2E
    lane = lax.broadcasted_iota(jnp.int32, (nq, 128), 1)
    per_step = pp // 2
    first_blk = s_id * per_step
    ksums, scores, probs, partials = [], [], [], []
    for i in range(per_step):
        k0, k1 = k_pages[2 * i][...], k_pages[2 * i + 1][...]
        ksums.append(jnp.sum(k0.reshape(-1, SUBLANES, HEAD_DIM), axis=0)
                     + jnp.sum(k1.reshape(-1, SUBLANES, HEAD_DIM), axis=0))
        kb = jnp.concatenate([k0, k1], axis=0).astype(BF16)
        scores.append(_dot_nt(qb, kb) + bias_s[...])
    for s in scores:
        m = jnp.max(s, axis=-1, keepdims=True)
        p = jnp.exp2(s - m)
        probs.append((m, jnp.sum(p, axis=-1, keepdims=True), p.astype(BF16)))
    for i, (m, l, p) in enumerate(probs):
        vb = jnp.concatenate([v_pages[2 * i][...], v_pages[2 * i + 1][...]], axis=0).astype(BF16)
        partials.append((m, l, _dot(p, vb)))
    m_new, l_new = m_s[...], l_s[...]
    for i, (m, l, o_part) in enumerate(partials):
        n = first_blk + i
        km_s[pl.ds(pl.multiple_of(n * SUBLANES, SUBLANES), SUBLANES), :] = ksums[i] * (1.0 / MOBA_BLOCK)
        op_s[n] = o_part
        m_new = jnp.where(lane == n, m + slope_col * ((n + 1) * MOBA_BLOCK - past).astype(F32), m_new)
        l_new = jnp.where(lane == n, l, l_new)
    m_s[...] = m_new
    l_s[...] = l_new

    @pl.when(s_id == pl.num_programs(1) - 1)
    def _():
        gcols = n_blk * SUBLANES
        gfull = _dot_nt(q, km_s[...], precision=lax.Precision.HIGHEST)
        gsum = gfull + pltpu.roll(gfull, gcols - kv, 1)
        grow = lax.broadcasted_iota(jnp.int32, (nq, gcols), 0)
        gcol = lax.broadcasted_iota(jnp.int32, (nq, gcols), 1)
        gs = jnp.where((gcol % SUBLANES) == (grow % N_HEADS) // GROUP, gsum, -jnp.inf)
        gcol_f = gcol.astype(F32)
        chosen = jnp.zeros((nq, 128), F32)
        for _ in range(MOBA_TOPK):
            best = jnp.max(gs, axis=-1, keepdims=True)
            first = jnp.min(jnp.where(gs == best, gcol_f, float(gcols)), axis=-1, keepdims=True)
            chosen = jnp.where(lane == (first.astype(jnp.int32) // SUBLANES), 1.0, chosen)
            gs = jnp.where(gcol_f == first, -jnp.inf, gs)
        is_chosen = chosen > 0.0

        nk = n_new * kv
        orow = lax.broadcasted_iota(jnp.int32, (nq, nk), 0)
        ocol = lax.broadcasted_iota(jnp.int32, (nq, nk), 1)
        ohead, ostep = orow % N_HEADS, orow // N_HEADS
        ok = jnp.logical_and((ocol % kv) == ohead // GROUP, ocol // kv <= ostep)
        so = _dot_nt(qb, kn_ref[...].astype(BF16)) - (_row_slopes(ohead) * LOG2E) * (ostep - ocol // kv).astype(F32)
        so = jnp.where(ok, so, NEG)

        m_all = jnp.maximum(jnp.max(so, axis=-1, keepdims=True),
                            jnp.max(jnp.where(is_chosen, m_s[...], NEG), axis=-1, keepdims=True))
        wgt = jnp.where(is_chosen, jnp.exp2(m_s[...] - m_all), 0.0)
        po = jnp.exp2(so - m_all)
        l_all = jnp.sum(wgt * l_s[...], axis=-1, keepdims=True) + jnp.sum(po, axis=-1, keepdims=True)
        acc = _dot(po.astype(BF16), vn_ref[...].astype(BF16))
        for n in range(n_blk):
            acc = acc + wgt[:, n:n + 1] * op_s[n]
        o_ref[...] = acc / l_all


def _attn_sample(q, k_new, v_new, cache_k, cache_v, page_ids, n_new):
    b, nq, _ = q.shape
    n_pages = page_ids.shape[1]
    past = n_pages * PAGE_SIZE
    pp = PAGES_PER_STEP
    assert MOBA_BLOCK == 2 * PAGE_SIZE and past % MOBA_BLOCK == 0 and n_pages % pp == 0
    n_blk = past // MOBA_BLOCK
    assert MOBA_TOPK <= n_blk <= 128 and n_new <= MOBA_BLOCK
    rows = PAGE_SIZE * N_KV_HEADS

    def page(i):
        return pl.BlockSpec((None, rows, HEAD_DIM), lambda bi, si, pt: (pt[bi * n_pages + si * pp + i], 0, 0))

    def per_seq(r):
        return pl.BlockSpec((None, r, HEAD_DIM), lambda bi, si, pt: (bi, 0, 0))

    return pl.pallas_call(
        functools.partial(_attn_sample_kernel, pp=pp, n_blk=n_blk, past=past, n_new=n_new),
        grid_spec=pltpu.PrefetchScalarGridSpec(
            num_scalar_prefetch=1,
            grid=(b, n_pages // pp),
            in_specs=[per_seq(nq), per_seq(n_new * N_KV_HEADS), per_seq(n_new * N_KV_HEADS)]
            + [page(i) for i in range(pp)] * 2,
            out_specs=per_seq(nq),
            scratch_shapes=[pltpu.VMEM((n_blk * SUBLANES, HEAD_DIM), F32), pltpu.VMEM((nq, 128), F32),
                            pltpu.VMEM((nq, 128), F32), pltpu.VMEM((n_blk, nq, HEAD_DIM), F32),
                            pltpu.VMEM((nq, MOBA_BLOCK * N_KV_HEADS), F32)]),
        out_shape=jax.ShapeDtypeStruct((b, nq, HEAD_DIM), F32),
        compiler_params=_params(2),
        name="attn_sample",
    )(page_ids.reshape(-1), q, k_new, v_new, *([cache_k] * pp), *([cache_v] * pp))


def _merge_ffn_kernel(x_ref, ya_ref, o_ref_in, sgb_ref, g2_ref, sh_ref, sc_ref, g3_ref, ng_ref,
                      wbb_ref, wout_ref, wg_ref, wu_ref, wd_ref, out_ref, *, chunk):
    yb = _dot(o_ref_in[...].astype(BF16), wbb_ref[...])
    merged = (ya_ref[...] + sgb_ref[...] * yb).astype(BF16)
    x = x_ref[...] + g2_ref[...] * _dot(merged, wout_ref[...])
    h = (_rms(x, ng_ref[...]) * (1.0 + sc_ref[...]) + sh_ref[...]).astype(BF16)
    out_ref[...] = x + 0.5 * g3_ref[...] * _swiglu_acc(h, wg_ref, wu_ref, wd_ref, chunk)


def _merge_ffn(x, ya, o, sgb, g2, sh, sc, g3, ng, wbb, wout, wg, wu, wd, tm, tiles_per_group):
    t, d = x.shape
    row = pl.BlockSpec((tm, d), lambda i: (i, 0))
    mod = functools.partial(_mod_spec, tm=tm, tiles_per_group=tiles_per_group)
    return pl.pallas_call(
        functools.partial(_merge_ffn_kernel, chunk=_ffn_chunk(wg.shape[1])),
        grid=(t // tm,),
        in_specs=[row, row, pl.BlockSpec((tm, o.shape[1]), lambda i: (i, 0)), row,
                  mod(g2), mod(sh), mod(sc), mod(g3), _resident((1, d)),
                  _resident(wbb.shape), _resident(wout.shape), _resident(wg.shape), _resident(wu.shape),
                  _resident(wd.shape)],
        out_specs=row,
        out_shape=jax.ShapeDtypeStruct((t, d), F32),
        compiler_params=_params(1),
        name="merge_ffn",
    )(x, ya, o, sgb, g2[0], sh[0], sc[0], g3[0], ng, wbb, wout, wg, wu, wd)


def _token_tile(t):
    return 512 if t % 512 == 0 else t


def _layer_weights(p):
    w = dict(p)
    for name in ("ffn1_gate", "ffn1_up", "ffn1_down", "w_in", "w_branch_a", "w_branch_b", "w_out",
                 "ffn2_gate", "ffn2_up", "ffn2_down"):
        w[name] = p[name].astype(BF16)
    w["rg_w"] = jnp.concatenate([p["rg_wa"], p["rg_wx"]], axis=-1).astype(BF16)
    for name in ("norm1_g", "norm2_g", "norm3_g", "conv_b", "rg_ba", "rg_bx", "rg_lambda", "q_norm_g", "k_norm_g"):
        w[name] = p[name].reshape(1, -1)
    return w


def _prompt_layer(x, mod, w):
    b, s, d = x.shape
    t = b * s
    tm = _token_tile(s)
    tpg = s // tm
    sh1, sc1, g1, sh2, sc2, g2, sh3, sc3, g3 = [(mod[:, None, :], c) for c in range(3 * N_SUBLAYERS)]
    x1 = _ffn(x.reshape(t, d), sh1, sc1, g1, w["norm1_g"], w["ffn1_gate"], w["ffn1_up"], w["ffn1_down"], tm, tpg)
    d_rnn = w["w_branch_a"].shape[0]
    q, k, v, sgb, ya, k_heads, v_heads, new_conv, h_last = _mixin_prompt(
        x1.reshape(b, s, d), sh2, sc2, w["norm2_g"], w["w_in"], w["q_norm_g"], w["k_norm_g"],
        jnp.zeros((b, CONV_WIDTH - 1, d_rnn), F32), jnp.zeros((b, 1, d_rnn), F32),
        w["conv_w"], w["conv_b"], w["rg_w"], w["rg_ba"], w["rg_bx"], w["rg_lambda"], w["w_branch_a"], tm)
    sgb = sgb.reshape(t, d)
    o = _attn_prompt(q.reshape(b, s, -1), k.reshape(b, s, -1), v.reshape(b, s, -1))
    y = _merge_ffn(x1, ya.reshape(t, d), o.reshape(t, -1), sgb, g2, sh3, sc3, g3, w["norm3_g"],
                   w["w_branch_b"], w["w_out"], w["ffn2_gate"], w["ffn2_up"], w["ffn2_down"], tm, tpg)
    return y.reshape(b, s, d), k_heads, v_heads, new_conv, h_last.reshape(b, d_rnn)


def _sample_layer(x, mod, conv_buf, h0, cache_k, cache_v, page_ids, w):
    b, s, d = x.shape
    t = b * s
    per_token = jnp.repeat(mod, s, axis=0)[None]
    sh1, sc1, g1, sh2, sc2, g2, sh3, sc3, g3 = [(per_token, c) for c in range(3 * N_SUBLAYERS)]
    x1 = _ffn(x.reshape(t, d), sh1, sc1, g1, w["norm1_g"], w["ffn1_gate"], w["ffn1_up"], w["ffn1_down"], t, 1)
    xr, ug, q, k, v, sga, sgb = _inproj(x1, sh2, sc2, w["norm2_g"], w["w_in"], w["q_norm_g"], w["k_norm_g"], t, 1)
    d_rnn = xr.shape[1]
    xp = jnp.concatenate([conv_buf, xr.reshape(b, s, d_rnn)], axis=1)
    taps = jnp.stack([xp[:, kk:kk + s].reshape(t, d_rnn) for kk in range(CONV_WIDTH)])
    ya, h_all = _rnn_sample(taps, ug, sga, jnp.repeat(h0, s, axis=0), w["conv_w"], w["conv_b"], w["rg_w"],
                            w["rg_ba"], w["rg_bx"], w["rg_lambda"], w["w_branch_a"], s)
    o = _attn_sample(q.reshape(b, s * N_HEADS, HEAD_DIM), k.reshape(b, s * N_KV_HEADS, HEAD_DIM),
                     v.reshape(b, s * N_KV_HEADS, HEAD_DIM), cache_k, cache_v, page_ids, s)
    y = _merge_ffn(x1, ya, o.reshape(t, -1), sgb, g2, sh3, sc3, g3, w["norm3_g"],
                   w["w_branch_b"], w["w_out"], w["ffn2_gate"], w["ffn2_up"], w["ffn2_down"], t, 1)
    return (y.reshape(b, s, d), k.reshape(b, s, N_KV_HEADS, HEAD_DIM), v.reshape(b, s, N_KV_HEADS, HEAD_DIM),
            xp[:, -(CONV_WIDTH - 1):], h_all.reshape(b, s, d_rnn)[:, -1])


def kernel(x_prompt, x_sample, c_prompt, c_sample, cache_k, cache_v, state_conv, state_rglru, page_table, w_ada, b_ada, norm1_g, ffn1_gate, ffn1_up, ffn1_down, norm2_g, w_in, conv_w, conv_b, rg_wa, rg_ba, rg_wx, rg_bx, rg_lambda, q_norm_g, k_norm_g, w_branch_a, w_branch_b, w_out, norm3_g, ffn2_gate, ffn2_up, ffn2_down):
    depth, n_pool = cache_k.shape[0], cache_k.shape[1]
    bp = x_prompt.shape[0]
    assert cache_k.shape[2:] == (PAGE_SIZE, N_KV_HEADS, HEAD_DIM)
    ck = cache_k.reshape(depth * n_pool, PAGE_SIZE * N_KV_HEADS, HEAD_DIM)
    cv = cache_v.reshape(depth * n_pool, PAGE_SIZE * N_KV_HEADS, HEAD_DIM)
    c_all = jnp.concatenate([c_prompt, c_sample], axis=0)
    pad = -c_all.shape[0] % SUBLANES
    c_all = jnp.pad(c_all, ((0, pad), (0, 0)))
    yp, ys = x_prompt, x_sample
    outs = [[] for _ in range(8)]
    for l in range(depth):
        p = dict(norm1_g=norm1_g[l], ffn1_gate=ffn1_gate[l], ffn1_up=ffn1_up[l], ffn1_down=ffn1_down[l],
                 norm2_g=norm2_g[l], w_in=w_in[l], conv_w=conv_w[l], conv_b=conv_b[l], rg_wa=rg_wa[l],
                 rg_ba=rg_ba[l], rg_wx=rg_wx[l], rg_bx=rg_bx[l], rg_lambda=rg_lambda[l], q_norm_g=q_norm_g[l],
                 k_norm_g=k_norm_g[l], w_branch_a=w_branch_a[l], w_branch_b=w_branch_b[l], w_out=w_out[l],
                 norm3_g=norm3_g[l], ffn2_gate=ffn2_gate[l], ffn2_up=ffn2_up[l], ffn2_down=ffn2_down[l])
        w = _layer_weights(p)
        mod = _ada(c_all, w_ada[l], b_ada[l])
        yp, kp, vp, cp, hp = _prompt_layer(yp, mod[:bp], w)
        ys, ks, vs, cs, hs = _sample_layer(ys, mod[bp:bp + x_sample.shape[0]], state_conv[l], state_rglru[l],
                                           ck, cv, page_table + l * n_pool, w)
        for lst, val in zip(outs, (kp, vp, cp, hp, ks, vs, cs, hs)):
            lst.append(val)
    return (yp, ys) + tuple(jnp.stack(o) for o in outs)
```

```python
import functools

import jax
import jax.numpy as jnp
from jax import lax
from jax.experimental import pallas as pl
from jax.experimental.pallas import tpu as pltpu

F32 = jnp.float32
BF16 = jnp.bfloat16

N_HEADS = 8
N_KV_HEADS = 4
HEAD_DIM = 128
GROUP = N_HEADS // N_KV_HEADS
MOBA_BLOCK = 256
MOBA_TOPK = 3
PAGE_SIZE = 128
N_RNN_BLOCKS = 8
RNN_BLOCK = 128
CONV_WIDTH = 4
RG_C = 8.0
N_SUBLAYERS = 3
EPS = 1e-6
ALIBI_SLOPES = tuple(2.0 ** (-8.0 * (h + 1) / N_HEADS) for h in range(N_HEADS))
ATTN_SCALE = HEAD_DIM ** -0.5
NEG = -1e30
LOG2E = 1.4426950408889634
SUBLANES = 8
VMEM_LIMIT = 56 * 1024 * 1024
PAGES_PER_STEP = 32
KV_BLOCKS_PER_STEP = 4
GATE_COST = 40
MXU_COLS = 256


def _params(n_axes):
    return pltpu.CompilerParams(dimension_semantics=("arbitrary",) * n_axes, vmem_limit_bytes=VMEM_LIMIT)


def _resident(shape):
    nd = len(shape)
    return pl.BlockSpec(shape, lambda *_: (0,) * nd, pipeline_mode=pl.Buffered(1))


def _dot(a, b):
    return jnp.dot(a, b, preferred_element_type=F32)


def _dot_nt(a, b, precision=None):
    return lax.dot_general(a, b, (((1,), (1,)), ((), ())), precision=precision, preferred_element_type=F32)


def _dot_nt_split(a, b):
    a_hi = a.astype(BF16)
    b_hi = b.astype(BF16)
    a_lo = (a - a_hi.astype(F32)).astype(BF16)
    b_lo = (b - b_hi.astype(F32)).astype(BF16)
    return _dot_nt(a_hi, b_hi) + (_dot_nt(a_hi, b_lo) + _dot_nt(a_lo, b_hi))


def _rms(x, g):
    return x * lax.rsqrt(jnp.mean(x * x, axis=-1, keepdims=True) + EPS) * g


def _sigmoid(x):
    return 0.5 * jnp.tanh(0.5 * x) + 0.5


def _silu(x):
    return x * _sigmoid(x)


def _ada_kernel(c_ref, w_ref, b_ref, o_ref):
    a = _silu(c_ref[...]).astype(BF16)
    o_ref[...] = _dot(a, w_ref[...].astype(BF16)) + b_ref[...]


def _ada(c, w, b):
    m, d = c.shape
    n = w.shape[1]
    tn = n // 8
    return pl.pallas_call(
        _ada_kernel,
        grid=(n // tn,),
        in_specs=[pl.BlockSpec((m, d), lambda i: (0, 0)),
                  pl.BlockSpec((d, tn), lambda i: (0, i)),
                  pl.BlockSpec((1, tn), lambda i: (0, i))],
        out_specs=pl.BlockSpec((m, tn), lambda i: (0, i)),
        out_shape=jax.ShapeDtypeStruct((m, n), F32),
        compiler_params=_params(1),
        name="ada",
    )(c, w, b.reshape(1, n))


def _swiglu_acc(h, wg_ref, wu_ref, wd_ref, chunk):
    d_ff = wg_ref.shape[1]
    acc = None
    for c in range(d_ff // chunk):
        sl = slice(c * chunk, (c + 1) * chunk)
        act = (_silu(_dot(h, wg_ref[:, sl])) * _dot(h, wu_ref[:, sl])).astype(BF16)
        part = _dot(act, wd_ref[sl, :])
        acc = part if acc is None else acc + part
    return acc


def _ffn_kernel(x_ref, sh_ref, sc_ref, g_ref, ng_ref, wg_ref, wu_ref, wd_ref, o_ref, *, chunk):
    x = x_ref[...]
    h = (_rms(x, ng_ref[...]) * (1.0 + sc_ref[...]) + sh_ref[...]).astype(BF16)
    o_ref[...] = x + 0.5 * g_ref[...] * _swiglu_acc(h, wg_ref, wu_ref, wd_ref, chunk)


def _mod_spec(mod, tm, tiles_per_group):
    arr, chunk = mod
    r, d = arr.shape[1], arr.shape[2] // (3 * N_SUBLAYERS)
    return pl.BlockSpec((None, r, d), lambda i: (i // tiles_per_group, 0, chunk))


def _ffn_chunk(d_ff):
    return 256 if d_ff % 256 == 0 else 128


def _ffn(x, sh, sc, g, ng, wg, wu, wd, tm, tiles_per_group):
    t, d = x.shape
    d_ff = wg.shape[1]
    row = pl.BlockSpec((tm, d), lambda i: (i, 0))
    return pl.pallas_call(
        functools.partial(_ffn_kernel, chunk=_ffn_chunk(d_ff)),
        grid=(t // tm,),
        in_specs=[row, _mod_spec(sh, tm, tiles_per_group), _mod_spec(sc, tm, tiles_per_group),
                  _mod_spec(g, tm, tiles_per_group), _resident((1, d)),
                  _resident(wg.shape), _resident(wu.shape), _resident(wd.shape)],
        out_specs=row,
        out_shape=jax.ShapeDtypeStruct((t, d), F32),
        compiler_params=_params(1),
        name="ffn",
    )(x, sh[0], sc[0], g[0], ng, wg, wu, wd)


def _inproj_kernel(x_ref, sh_ref, sc_ref, ng_ref, w_ref, qg_ref, kg_ref,
                   xr_ref, ug_ref, q_ref, k_ref, v_ref, sga_ref, sgb_ref):
    d_rnn = xr_ref.shape[1]
    dq = q_ref.shape[1]
    dk = k_ref.shape[1]
    x = x_ref[...]
    h = (_rms(x, ng_ref[...]) * (1.0 + sc_ref[...]) + sh_ref[...]).astype(BF16)
    o = 0
    xr_ref[...] = _dot(h, w_ref[:, o:o + d_rnn])
    o += d_rnn
    ug_ref[...] = jax.nn.gelu(_dot(h, w_ref[:, o:o + d_rnn]))
    o += d_rnn
    for hd in range(dq // HEAD_DIM):
        sl = slice(hd * HEAD_DIM, (hd + 1) * HEAD_DIM)
        q_ref[:, sl] = _rms(_dot(h, w_ref[:, o + hd * HEAD_DIM:o + (hd + 1) * HEAD_DIM]), qg_ref[...])
    o += dq
    for hd in range(dk // HEAD_DIM):
        sl = slice(hd * HEAD_DIM, (hd + 1) * HEAD_DIM)
        k_ref[:, sl] = _rms(_dot(h, w_ref[:, o + hd * HEAD_DIM:o + (hd + 1) * HEAD_DIM]), kg_ref[...])
    o += dk
    v_ref[...] = _dot(h, w_ref[:, o:o + dk])
    o += dk
    d = sga_ref.shape[1]
    sga_ref[...] = _sigmoid(_dot(h, w_ref[:, o:o + d]))
    o += d
    sgb_ref[...] = _sigmoid(_dot(h, w_ref[:, o:o + d]))


def _inproj(x, sh, sc, ng, w_in, qg, kg, tm, tiles_per_group):
    t, d = x.shape
    dq, dk = N_HEADS * HEAD_DIM, N_KV_HEADS * HEAD_DIM
    d_rnn = (w_in.shape[1] - dq - 2 * dk - 2 * d) // 2

    def row(w):
        return pl.BlockSpec((tm, w), lambda i: (i, 0))

    widths = (d_rnn, d_rnn, dq, dk, dk, d, d)
    return pl.pallas_call(
        _inproj_kernel,
        grid=(t // tm,),
        in_specs=[row(d), _mod_spec(sh, tm, tiles_per_group), _mod_spec(sc, tm, tiles_per_group),
                  _resident((1, d)), _resident(w_in.shape), _resident((1, HEAD_DIM)), _resident((1, HEAD_DIM))],
        out_specs=[row(w) for w in widths],
        out_shape=[jax.ShapeDtypeStruct((t, w), F32) for w in widths],
        compiler_params=_params(1),
        name="inproj",
    )(x, sh[0], sc[0], ng, w_in, qg, kg)


def _rg_gate_block(xcn, n, wg_ref, bra_ref, brx_ref, sp, a_ref, b_ref):
    sl = slice(n * RNN_BLOCK, (n + 1) * RNN_BLOCK)
    z = _dot(xcn.astype(BF16), wg_ref[n])
    r = _sigmoid(z[:, :RNN_BLOCK] + bra_ref[:, sl])
    i = _sigmoid(z[:, RNN_BLOCK:] + brx_ref[:, sl])
    log_a = -RG_C * r * sp[:, sl]
    a = jnp.exp(log_a)
    a_ref[:, sl] = a
    b_ref[:, sl] = jnp.sqrt(1.0 - a * a) * (i * xcn)


def _rg_gates(xc, wg_ref, bra_ref, brx_ref, lam_ref, a_ref, b_ref):
    sp = jax.nn.softplus(-lam_ref[...])
    for n in range(N_RNN_BLOCKS):
        _rg_gate_block(xc[:, n * RNN_BLOCK:(n + 1) * RNN_BLOCK], n, wg_ref, bra_ref, brx_ref, sp, a_ref, b_ref)


def _scan_rows(a, b, row_in_seg, steps):
    for d in steps:
        keep = row_in_seg >= d
        a_prev = jnp.where(keep, pltpu.roll(a, d, 0), 1.0)
        b_prev = jnp.where(keep, pltpu.roll(b, d, 0), 0.0)
        b = a * b_prev + b
        a = a * a_prev
    return a, b


def _mixin_prompt_kernel(x_ref, sh_ref, sc_ref, ng_ref, w_ref, qg_ref, kg_ref, conv0_ref, h0_ref, cw_ref, cb_ref,
                         wg_ref, bra_ref, brx_ref, lam_ref, wba_ref,
                         q_ref, k_ref, v_ref, sgb_ref, ya_ref, k4_ref, v4_ref, nconv_ref, hlast_ref,
                         xbuf, a_s, b_s, hc_s, ug_s, sga_s, *, ts):
    s = pl.program_id(1)
    w = xbuf.shape[1]
    dq = q_ref.shape[1]
    dk = k_ref.shape[1]
    d = sgb_ref.shape[1]
    tail = CONV_WIDTH - 1
    lo = SUBLANES - tail

    @pl.when(s == 0)
    def _():
        xbuf[lo:SUBLANES, :] = conv0_ref[...]
        hc_s[...] = jnp.broadcast_to(h0_ref[...], hc_s.shape)

    x = x_ref[...]
    h = (_rms(x, ng_ref[...]) * (1.0 + sc_ref[...]) + sh_ref[...]).astype(BF16)
    o_gate, o_q = w, 2 * w
    o_k = o_q + dq
    o_v = o_k + dk
    o_ga = o_v + dk
    o_gb = o_ga + d
    cw_ = MXU_COLS

    def proj(lo):
        return _dot(h, w_ref[:, lo:lo + cw_])

    def normed_heads(ref, gain_ref, base, c):
        z = proj(base + c * cw_)
        for i in range(cw_ // HEAD_DIM):
            col = c * cw_ + i * HEAD_DIM
            ref[:, col:col + HEAD_DIM] = _rms(z[:, i * HEAD_DIM:(i + 1) * HEAD_DIM], gain_ref[...])

    def store(ref, base, c, fn):
        ref[:, c * cw_:(c + 1) * cw_] = fn(proj(base + c * cw_))

    jobs = [functools.partial(normed_heads, q_ref, qg_ref, o_q, c) for c in range(dq // cw_)]
    jobs += [functools.partial(normed_heads, k_ref, kg_ref, o_k, c) for c in range(dk // cw_)]
    jobs += [functools.partial(store, v_ref, o_v, c, lambda z: z) for c in range(dk // cw_)]
    jobs += [functools.partial(store, sgb_ref, o_gb, c, _sigmoid) for c in range(d // cw_)]
    jobs += [functools.partial(store, ug_s, o_gate, c, jax.nn.gelu) for c in range(w // cw_)]
    jobs += [functools.partial(store, sga_s, o_ga, c, _sigmoid) for c in range(d // cw_)]

    def by_head(src_ref, dst_ref, hd):
        dst_ref[:, hd, :] = src_ref[:, hd * HEAD_DIM:(hd + 1) * HEAD_DIM]
    cost = [GATE_COST] * N_RNN_BLOCKS + [1] * (ts // SUBLANES)
    due, acc_cost = [], 0
    for c_ in cost:
        acc_cost += c_
        due.append((acc_cost * len(jobs)) // sum(cost))
    done = 0

    for c in range(w // cw_):
        xbuf[SUBLANES:SUBLANES + ts, c * cw_:(c + 1) * cw_] = proj(c * cw_)
    sp = jax.nn.softplus(-lam_ref[...])
    for n in range(N_RNN_BLOCKS):
        sl = slice(n * RNN_BLOCK, (n + 1) * RNN_BLOCK)
        xcn = cb_ref[:, sl] + cw_ref[0:1, sl] * xbuf[lo:lo + ts, sl]
        for k in range(1, CONV_WIDTH):
            xcn = xcn + cw_ref[k:k + 1, sl] * xbuf[lo + k:lo + k + ts, sl]
        _rg_gate_block(xcn, n, wg_ref, bra_ref, brx_ref, sp, a_s, b_s)
        while done < due[n]:
            jobs[done]()
            done += 1
    new_tail = xbuf[ts + lo:ts + SUBLANES, :]
    nconv_ref[...] = new_tail
    xbuf[lo:SUBLANES, :] = new_tail

    row = lax.broadcasted_iota(jnp.int32, (SUBLANES, w), 0)
    hc = hc_s[...]
    for c in range(ts // SUBLANES):
        rows = slice(c * SUBLANES, (c + 1) * SUBLANES)
        a_cum, b_cum = _scan_rows(a_s[rows, :], b_s[rows, :], row, (1, 2, 4))
        hs = a_cum * hc + b_cum
        a_s[rows, :] = hs
        hc = jnp.broadcast_to(hs[SUBLANES - 1:SUBLANES, :], (SUBLANES, w))
        while done < due[N_RNN_BLOCKS + c]:
            jobs[done]()
            done += 1
    hc_s[...] = hc
    hlast_ref[...] = hc[0:1, :]
    u = (a_s[...] * ug_s[...]).astype(BF16)
    ya = _dot(u, wba_ref[...])
    for hd in range(dk // HEAD_DIM):
        by_head(k_ref, k4_ref, hd)
        by_head(v_ref, v4_ref, hd)
    ya_ref[...] = sga_s[...] * ya


def _mixin_prompt(x, sh, sc, ng, w_in, qg, kg, conv0, h0, cw, cb, wg, bra, brx, lam, wba, ts):
    b, s, d = x.shape
    w = wba.shape[0]
    dq, dk = N_HEADS * HEAD_DIM, N_KV_HEADS * HEAD_DIM
    tail = CONV_WIDTH - 1

    def seq(width):
        return pl.BlockSpec((None, ts, width), lambda i, j: (i, j, 0))

    def per_seq(rows, width):
        return pl.BlockSpec((None, rows, width), lambda i, j: (i, 0, 0))

    def mod_row(mod):
        return pl.BlockSpec((None, 1, d), lambda i, j: (i, 0, mod[1]))

    widths = (dq, dk, dk, d, d)
    by_head = pl.BlockSpec((None, ts, N_KV_HEADS, HEAD_DIM), lambda i, j: (i, j, 0, 0))
    return pl.pallas_call(
        functools.partial(_mixin_prompt_kernel, ts=ts),
        grid=(b, s // ts),
        in_specs=[seq(d), mod_row(sh), mod_row(sc), _resident((1, d)), _resident(w_in.shape),
                  _resident((1, HEAD_DIM)), _resident((1, HEAD_DIM)), per_seq(tail, w), per_seq(1, w),
                  _resident(cw.shape), _resident((1, w)), _resident(wg.shape), _resident((1, w)), _resident((1, w)),
                  _resident((1, w)), _resident(wba.shape)],
        out_specs=[seq(wd) for wd in widths] + [by_head, by_head, per_seq(tail, w), per_seq(1, w)],
        out_shape=[jax.ShapeDtypeStruct((b, s, wd), F32) for wd in widths]
        + [jax.ShapeDtypeStruct((b, s, N_KV_HEADS, HEAD_DIM), F32)] * 2
        + [jax.ShapeDtypeStruct((b, tail, w), F32), jax.ShapeDtypeStruct((b, 1, w), F32)],
        scratch_shapes=[pltpu.VMEM((ts + SUBLANES, w), F32), pltpu.VMEM((ts, w), F32), pltpu.VMEM((ts, w), F32),
                        pltpu.VMEM((SUBLANES, w), F32), pltpu.VMEM((ts, w), F32), pltpu.VMEM((ts, d), F32)],
        compiler_params=_params(2),
        name="mixin_prompt",
    )(x, sh[0], sc[0], ng, w_in, qg, kg, conv0, h0, cw, cb, wg, bra, brx, lam, wba)


def _rnn_sample_kernel(xs_ref, ug_ref, sga_ref, h0_ref, cw_ref, cb_ref, wg_ref, bra_ref, brx_ref, lam_ref, wba_ref,
                       ya_ref, h_ref, a_s, b_s, *, seg):
    m, w = ug_ref.shape
    xc = cb_ref[...] + cw_ref[0:1, :] * xs_ref[0]
    for k in range(1, CONV_WIDTH):
        xc = xc + cw_ref[k:k + 1, :] * xs_ref[k]
    _rg_gates(xc, wg_ref, bra_ref, brx_ref, lam_ref, a_s, b_s)
    row_in_seg = lax.broadcasted_iota(jnp.int32, (SUBLANES, w), 0) % seg
    steps = tuple(d for d in (1, 2, 4) if d < seg)
    for c in range(m // SUBLANES):
        rows = slice(c * SUBLANES, (c + 1) * SUBLANES)
        a_cum, b_cum = _scan_rows(a_s[rows, :], b_s[rows, :], row_in_seg, steps)
        h_ref[rows, :] = a_cum * h0_ref[rows, :] + b_cum
    u = (h_ref[...] * ug_ref[...]).astype(BF16)
    ya_ref[...] = sga_ref[...] * _dot(u, wba_ref[...])


def _rnn_sample(xs, ug, sga, h0_rows, cw, cb, wg, bra, brx, lam, wba, seg):
    m, w = ug.shape
    d = wba.shape[1]
    return pl.pallas_call(
        functools.partial(_rnn_sample_kernel, seg=seg),
        out_shape=[jax.ShapeDtypeStruct((m, d), F32), jax.ShapeDtypeStruct((m, w), F32)],
        scratch_shapes=[pltpu.VMEM((m, w), F32), pltpu.VMEM((m, w), F32)],
        compiler_params=pltpu.CompilerParams(vmem_limit_bytes=VMEM_LIMIT),
        name="rnn_sample",
    )(xs, ug, sga, h0_rows, cw, cb, wg, bra, brx, lam, wba)


V_ROWS = HEAD_DIM + 16


def _attn_prompt_kernel(slope_ref, q_ref, k_ref, v_ref, o_ref, kb_s, vt_s, kbg_s, vtg_s, km_s, q2_s, bias_s, colb_s,
                        acc_s, *, nb, gt):
    g = pl.program_id(1)
    blk = MOBA_BLOCK
    s_len = nb * blk
    q_scale = ATTN_SCALE * LOG2E

    ones_rows = (lax.broadcasted_iota(jnp.int32, (V_ROWS - HEAD_DIM, blk), 0) == 0).astype(BF16)
    for n in range(nb):
        rows = slice(n * blk, (n + 1) * blk)
        grp, sub = n // gt, slice((n % gt) * blk, (n % gt + 1) * blk)
        kn = k_ref[rows, :]
        kb = kn.astype(BF16)
        vt = v_ref[rows, :].T.astype(BF16)
        kb_s[n] = kb
        kbg_s[grp, sub, :] = kb
        vt_s[n, 0:HEAD_DIM, :] = vt
        vt_s[n, HEAD_DIM:V_ROWS, :] = ones_rows
        vtg_s[grp, 0:HEAD_DIM, sub] = vt
        vtg_s[grp, HEAD_DIM:V_ROWS, sub] = ones_rows
        km_s[n:n + 1, :] = jnp.sum(kn, axis=0, keepdims=True) * (1.0 / blk)
        q2_s[n] = (q_ref[rows, :] * q_scale).astype(BF16)

    blk_id = lax.broadcasted_iota(jnp.int32, (nb, s_len), 0)
    q_blk = lax.broadcasted_iota(jnp.int32, (nb, s_len), 1) // blk
    fully_past = blk_id < q_blk
    blocks_ahead = ((blk_id - q_blk) * blk).astype(F32)
    key_off = lax.broadcasted_iota(jnp.int32, (blk, blk), 0)
    causal = key_off <= lax.broadcasted_iota(jnp.int32, (blk, blk), 1)
    km = km_s[...]
    for hh in range(GROUP):
        slope2 = slope_ref[g * GROUP + hh] * LOG2E
        cols = slice(hh * HEAD_DIM, (hh + 1) * HEAD_DIM)
        gs = jnp.where(fully_past, _dot_nt_split(km, q_ref[:, cols]), -jnp.inf)
        rank = jnp.zeros((nb, s_len), jnp.int32)
        for m in range(nb):
            gm = gs[m:m + 1, :]
            tie = (blk_id > m).astype(jnp.int32)
            rank = rank + jnp.where(gm > gs, 1, jnp.where(gm == gs, tie, 0))
        chosen = jnp.logical_and(fully_past, rank < MOBA_TOPK)
        bias = jnp.where(chosen, slope2 * blocks_ahead, NEG)
        for jq in range(nb):
            bias_s[hh, jq] = bias[:, jq * blk:(jq + 1) * blk]
        colb_s[hh] = slope2 * key_off.astype(F32)

    heads = [slice(hh * HEAD_DIM, (hh + 1) * HEAD_DIM) for hh in range(GROUP)]

    def group_scores(gi, j, q2):
        kg = kbg_s[gi]
        s_grp = [_dot_nt(kg, q2[:, cols]) for cols in heads]
        parts, tops = [], []
        for hh in range(GROUP):
            ps = [s_grp[hh][t * blk:(t + 1) * blk, :] + colb_s[hh] + bias_s[hh, j, pl.ds(gi * gt + t, 1), :]
                  for t in range(gt)]
            top = jnp.max(ps[0], axis=0, keepdims=True)
            for part in ps[1:]:
                top = jnp.maximum(top, jnp.max(part, axis=0, keepdims=True))
            parts.append(ps)
            tops.append(top)
        return parts, tops

    def write_out(jb):
        rows = pl.ds(pl.multiple_of(jb * blk, blk), blk)
        for hh in range(GROUP):
            acc = acc_s[hh]
            o = acc[0:HEAD_DIM, :] / acc[HEAD_DIM:HEAD_DIM + 1, :]
            o_ref[rows, heads[hh]] = o.T.astype(o_ref.dtype)

    acc_s[...] = jnp.ones_like(acc_s)

    def q_block(j, _):
        q2 = q2_s[j]
        kd = kb_s[j]
        vd = vt_s[j]
        s_own = [_dot_nt(kd, q2[:, cols]) for cols in heads]
        parts0, tops0 = group_scores(0, j, q2)
        write_out(jnp.maximum(j - 1, 0))
        m_run, probs = [], []
        for hh in range(GROUP):
            s = jnp.where(causal, s_own[hh] + colb_s[hh], NEG)
            m1 = jnp.maximum(jnp.max(s, axis=0, keepdims=True), tops0[hh])
            p_own = jnp.exp2(s - m1).astype(BF16)
            p_grp = jnp.concatenate([jnp.exp2(part - m1).astype(BF16) for part in parts0[hh]], axis=0)
            probs.append((p_own, p_grp))
            m_run.append(m1)
        vg0 = vtg_s[0]
        for hh in range(GROUP):
            acc_s[hh] = _dot(vd, probs[hh][0]) + _dot(vg0, probs[hh][1])

        def kv_group(gi, m_run):
            vg = vtg_s[gi]
            parts, tops = group_scores(gi, j, q2)
            out, probs = [], []
            for hh in range(GROUP):
                m_new = jnp.maximum(m_run[hh], tops[hh])
                probs.append(jnp.concatenate([jnp.exp2(part - m_new).astype(BF16) for part in parts[hh]], axis=0))
                out.append(m_new)
            pv = [_dot(vg, p) for p in probs]
            for hh in range(GROUP):
                acc_s[hh] = jnp.exp2(m_run[hh] - out[hh]) * acc_s[hh] + pv[hh]
            return tuple(out)

        lax.fori_loop(1, (j + gt - 1) // gt, kv_group, tuple(m_run))
        return 0

    lax.fori_loop(0, nb, q_block, 0)
    write_out(nb - 1)


def _attn_prompt(q, k, v):
    b, s, _ = q.shape
    assert s % MOBA_BLOCK == 0
    nb = s // MOBA_BLOCK
    gt = KV_BLOCKS_PER_STEP if nb % KV_BLOCKS_PER_STEP == 0 else 1
    gw = GROUP * HEAD_DIM
    slopes = jnp.asarray(ALIBI_SLOPES, F32)
    kv_spec = pl.BlockSpec((None, s, HEAD_DIM), lambda i, g: (i, 0, g))
    q_spec = pl.BlockSpec((None, s, gw), lambda i, g: (i, 0, g))
    return pl.pallas_call(
        functools.partial(_attn_prompt_kernel, nb=nb, gt=gt),
        grid=(b, N_KV_HEADS),
        in_specs=[pl.BlockSpec(memory_space=pltpu.SMEM), q_spec, kv_spec, kv_spec],
        out_specs=q_spec,
        out_shape=jax.ShapeDtypeStruct(q.shape, BF16),
        scratch_shapes=[pltpu.VMEM((nb, MOBA_BLOCK, HEAD_DIM), BF16), pltpu.VMEM((nb, V_ROWS, MOBA_BLOCK), BF16),
                        pltpu.VMEM((nb // gt, gt * MOBA_BLOCK, HEAD_DIM), BF16),
                        pltpu.VMEM((nb // gt, V_ROWS, gt * MOBA_BLOCK), BF16),
                        pltpu.VMEM((nb, HEAD_DIM), F32), pltpu.VMEM((nb, MOBA_BLOCK, gw), BF16),
                        pltpu.VMEM((GROUP, nb, nb, MOBA_BLOCK), F32), pltpu.VMEM((GROUP, MOBA_BLOCK, MOBA_BLOCK), F32),
                        pltpu.VMEM((GROUP, V_ROWS, MOBA_BLOCK), F32)],
        compiler_params=_params(2),
        name="attn_prompt",
    )(slopes, q, k, v)


def _row_slopes(head):
    out = jnp.zeros(head.shape, F32)
    for h, sl in enumerate(ALIBI_SLOPES):
        out = jnp.where(head == h, sl, out)
    return out


def _attn_sample_kernel(pt_ref, q_ref, kn_ref, vn_ref, *rest, pp, n_blk, past, n_new):
    k_pages, v_pages = rest[:pp], rest[pp:2 * pp]
    o_ref = rest[2 * pp]
    km_s, m_s, l_s, op_s, bias_s = rest[2 * pp + 1:]
    del pt_ref
    s_id = pl.program_id(1)
    nq = q_ref.shape[0]
    kv = N_KV_HEADS
    cols = MOBA_BLOCK * kv
    q = q_ref[...]
    qb = (q * (ATTN_SCALE * LOG2E)).astype(BF16)

    row = lax.broadcasted_iota(jnp.int32, (nq, cols), 0)
    col = lax.broadcasted_iota(jnp.int32, (nq, cols), 1)
    head = row % N_HEADS
    step = row // N_HEADS

    @pl.when(s_id == 0)
    def _():
        same_kv = (col % kv) == head // GROUP
        back = (step + MOBA_BLOCK - col // kv).astype(F32)
        bias_s[...] = jnp.where(same_kv, -(_row_slopes(head) * LOG2E) * back, NEG)
        m_s[...] = jnp.zeros_like(m_s)
        l_s[...] = jnp.zeros_like(l_s)

    slope_col = _row_slopes(head[:, 0:1]) * LOG2E
    lane = lax.broadcasted_iota(jnp.int32, (nq, 128), 1)
    per_step = pp // 2
    first_blk = s_id * per_step
    ksums, scores, probs, partials = [], [], [], []
    for i in range(per_step):
        k0, k1 = k_pages[2 * i][...], k_pages[2 * i + 1][...]
        ksums.append(jnp.sum(k0.reshape(-1, SUBLANES, HEAD_DIM), axis=0)
                     + jnp.sum(k1.reshape(-1, SUBLANES, HEAD_DIM), axis=0))
        kb = jnp.concatenate([k0, k1], axis=0).astype(BF16)
        scores.append(_dot_nt(qb, kb) + bias_s[...])
    for s in scores:
        m = jnp.max(s, axis=-1, keepdims=True)
        p = jnp.exp2(s - m)
        probs.append((m, jnp.sum(p, axis=-1, keepdims=True), p.astype(BF16)))
    for i, (m, l, p) in enumerate(probs):
        vb = jnp.concatenate([v_pages[2 * i][...], v_pages[2 * i + 1][...]], axis=0).astype(BF16)
        partials.append((m, l, _dot(p, vb)))
    m_new, l_new = m_s[...], l_s[...]
    for i, (m, l, o_part) in enumerate(partials):
        n = first_blk + i
        km_s[pl.ds(pl.multiple_of(n * SUBLANES, SUBLANES), SUBLANES), :] = ksums[i] * (1.0 / MOBA_BLOCK)
        op_s[n] = o_part
        m_new = jnp.where(lane == n, m + slope_col * ((n + 1) * MOBA_BLOCK - past).astype(F32), m_new)
        l_new = jnp.where(lane == n, l, l_new)
    m_s[...] = m_new
    l_s[...] = l_new

    @pl.when(s_id == pl.num_programs(1) - 1)
    def _():
        gcols = n_blk * SUBLANES
        gfull = _dot_nt(q, km_s[...], precision=lax.Precision.HIGHEST)
        gsum = gfull + pltpu.roll(gfull, gcols - kv, 1)
        grow = lax.broadcasted_iota(jnp.int32, (nq, gcols), 0)
        gcol = lax.broadcasted_iota(jnp.int32, (nq, gcols), 1)
        gs = jnp.where((gcol % SUBLANES) == (grow % N_HEADS) // GROUP, gsum, -jnp.inf)
        gcol_f = gcol.astype(F32)
        chosen = jnp.zeros((nq, 128), F32)
        for _ in range(MOBA_TOPK):
            best = jnp.max(gs, axis=-1, keepdims=True)
            first = jnp.min(jnp.where(gs == best, gcol_f, float(gcols)), axis=-1, keepdims=True)
            chosen = jnp.where(lane == (first.astype(jnp.int32) // SUBLANES), 1.0, chosen)
            gs = jnp.where(gcol_f == first, -jnp.inf, gs)
        is_chosen = chosen > 0.0

        nk = n_new * kv
        orow = lax.broadcasted_iota(jnp.int32, (nq, nk), 0)
        ocol = lax.broadcasted_iota(jnp.int32, (nq, nk), 1)
        ohead, ostep = orow % N_HEADS, orow // N_HEADS
        ok = jnp.logical_and((ocol % kv) == ohead // GROUP, ocol // kv <= ostep)
        so = _dot_nt(qb, kn_ref[...].astype(BF16)) - (_row_slopes(ohead) * LOG2E) * (ostep - ocol // kv).astype(F32)
        so = jnp.where(ok, so, NEG)

        m_all = jnp.maximum(jnp.max(so, axis=-1, keepdims=True),
                            jnp.max(jnp.where(is_chosen, m_s[...], NEG), axis=-1, keepdims=True))
        wgt = jnp.where(is_chosen, jnp.exp2(m_s[...] - m_all), 0.0)
        po = jnp.exp2(so - m_all)
        l_all = jnp.sum(wgt * l_s[...], axis=-1, keepdims=True) + jnp.sum(po, axis=-1, keepdims=True)
        acc = _dot(po.astype(BF16), vn_ref[...].astype(BF16))
        for n in range(n_blk):
            acc = acc + wgt[:, n:n + 1] * op_s[n]
        o_ref[...] = acc / l_all


def _attn_sample(q, k_new, v_new, cache_k, cache_v, page_ids, n_new):
    b, nq, _ = q.shape
    n_pages = page_ids.shape[1]
    past = n_pages * PAGE_SIZE
    pp = PAGES_PER_STEP
    assert MOBA_BLOCK == 2 * PAGE_SIZE and past % MOBA_BLOCK == 0 and n_pages % pp == 0
    n_blk = past // MOBA_BLOCK
    assert MOBA_TOPK <= n_blk <= 128 and n_new <= MOBA_BLOCK
    rows = PAGE_SIZE * N_KV_HEADS

    def page(i):
        return pl.BlockSpec((None, rows, HEAD_DIM), lambda bi, si, pt: (pt[bi * n_pages + si * pp + i], 0, 0))

    def per_seq(r):
        return pl.BlockSpec((None, r, HEAD_DIM), lambda bi, si, pt: (bi, 0, 0))

    return pl.pallas_call(
        functools.partial(_attn_sample_kernel, pp=pp, n_blk=n_blk, past=past, n_new=n_new),
        grid_spec=pltpu.PrefetchScalarGridSpec(
            num_scalar_prefetch=1,
            grid=(b, n_pages // pp),
            in_specs=[per_seq(nq), per_seq(n_new * N_KV_HEADS), per_seq(n_new * N_KV_HEADS)]
            + [page(i) for i in range(pp)] * 2,
            out_specs=per_seq(nq),
            scratch_shapes=[pltpu.VMEM((n_blk * SUBLANES, HEAD_DIM), F32), pltpu.VMEM((nq, 128), F32),
                            pltpu.VMEM((nq, 128), F32), pltpu.VMEM((n_blk, nq, HEAD_DIM), F32),
                            pltpu.VMEM((nq, MOBA_BLOCK * N_KV_HEADS), F32)]),
        out_shape=jax.ShapeDtypeStruct((b, nq, HEAD_DIM), F32),
        compiler_params=_params(2),
        name="attn_sample",
    )(page_ids.reshape(-1), q, k_new, v_new, *([cache_k] * pp), *([cache_v] * pp))


def _merge_ffn_kernel(x_ref, ya_ref, o_ref_in, sgb_ref, g2_ref, sh_ref, sc_ref, g3_ref, ng_ref,
                      wbb_ref, wout_ref, wg_ref, wu_ref, wd_ref, out_ref, *, chunk):
    yb = _dot(o_ref_in[...].astype(BF16), wbb_ref[...])
    merged = (ya_ref[...] + sgb_ref[...] * yb).astype(BF16)
    x = x_ref[...] + g2_ref[...] * _dot(merged, wout_ref[...])
    h = (_rms(x, ng_ref[...]) * (1.0 + sc_ref[...]) + sh_ref[...]).astype(BF16)
    out_ref[...] = x + 0.5 * g3_ref[...] * _swiglu_acc(h, wg_ref, wu_ref, wd_ref, chunk)


def _merge_ffn(x, ya, o, sgb, g2, sh, sc, g3, ng, wbb, wout, wg, wu, wd, tm, tiles_per_group):
    t, d = x.shape
    row = pl.BlockSpec((tm, d), lambda i: (i, 0))
    mod = functools.partial(_mod_spec, tm=tm, tiles_per_group=tiles_per_group)
    return pl.pallas_call(
        functools.partial(_merge_ffn_kernel, chunk=_ffn_chunk(wg.shape[1])),
        grid=(t // tm,),
        in_specs=[row, row, pl.BlockSpec((tm, o.shape[1]), lambda i: (i, 0)), row,
                  mod(g2), mod(sh), mod(sc), mod(g3), _resident((1, d)),
                  _resident(wbb.shape), _resident(wout.shape), _resident(wg.shape), _resident(wu.shape),
                  _resident(wd.shape)],
        out_specs=row,
        out_shape=jax.ShapeDtypeStruct((t, d), F32),
        compiler_params=_params(1),
        name="merge_ffn",
    )(x, ya, o, sgb, g2[0], sh[0], sc[0], g3[0], ng, wbb, wout, wg, wu, wd)


def _token_tile(t):
    return 512 if t % 512 == 0 else t


def _layer_weights(p):
    w = dict(p)
    for name in ("ffn1_gate", "ffn1_up", "ffn1_down", "w_in", "w_branch_a", "w_branch_b", "w_out",
                 "ffn2_gate", "ffn2_up", "ffn2_down"):
        w[name] = p[name].astype(BF16)
    w["rg_w"] = jnp.concatenate([p["rg_wa"], p["rg_wx"]], axis=-1).astype(BF16)
    for name in ("norm1_g", "norm2_g", "norm3_g", "conv_b", "rg_ba", "rg_bx", "rg_lambda", "q_norm_g", "k_norm_g"):
        w[name] = p[name].reshape(1, -1)
    return w


def _prompt_layer(x, mod, w):
    b, s, d = x.shape
    t = b * s
    tm = _token_tile(s)
    tpg = s // tm
    sh1, sc1, g1, sh2, sc2, g2, sh3, sc3, g3 = [(mod[:, None, :], c) for c in range(3 * N_SUBLAYERS)]
    x1 = _ffn(x.reshape(t, d), sh1, sc1, g1, w["norm1_g"], w["ffn1_gate"], w["ffn1_up"], w["ffn1_down"], tm, tpg)
    d_rnn = w["w_branch_a"].shape[0]
    q, k, v, sgb, ya, k_heads, v_heads, new_conv, h_last = _mixin_prompt(
        x1.reshape(b, s, d), sh2, sc2, w["norm2_g"], w["w_in"], w["q_norm_g"], w["k_norm_g"],
        jnp.zeros((b, CONV_WIDTH - 1, d_rnn), F32), jnp.zeros((b, 1, d_rnn), F32),
        w["conv_w"], w["conv_b"], w["rg_w"], w["rg_ba"], w["rg_bx"], w["rg_lambda"], w["w_branch_a"], tm)
    sgb = sgb.reshape(t, d)
    o = _attn_prompt(q.reshape(b, s, -1), k.reshape(b, s, -1), v.reshape(b, s, -1))
    y = _merge_ffn(x1, ya.reshape(t, d), o.reshape(t, -1), sgb, g2, sh3, sc3, g3, w["norm3_g"],
                   w["w_branch_b"], w["w_out"], w["ffn2_gate"], w["ffn2_up"], w["ffn2_down"], tm, tpg)
    return y.reshape(b, s, d), k_heads, v_heads, new_conv, h_last.reshape(b, d_rnn)


def _sample_layer(x, mod, conv_buf, h0, cache_k, cache_v, page_ids, w):
    b, s, d = x.shape
    t = b * s
    per_token = jnp.repeat(mod, s, axis=0)[None]
    sh1, sc1, g1, sh2, sc2, g2, sh3, sc3, g3 = [(per_token, c) for c in range(3 * N_SUBLAYERS)]
    x1 = _ffn(x.reshape(t, d), sh1, sc1, g1, w["norm1_g"], w["ffn1_gate"], w["ffn1_up"], w["ffn1_down"], t, 1)
    xr, ug, q, k, v, sga, sgb = _inproj(x1, sh2, sc2, w["norm2_g"], w["w_in"], w["q_norm_g"], w["k_norm_g"], t, 1)
    d_rnn = xr.shape[1]
    xp = jnp.concatenate([conv_buf, xr.reshape(b, s, d_rnn)], axis=1)
    taps = jnp.stack([xp[:, kk:kk + s].reshape(t, d_rnn) for kk in range(CONV_WIDTH)])
    ya, h_all = _rnn_sample(taps, ug, sga, jnp.repeat(h0, s, axis=0), w["conv_w"], w["conv_b"], w["rg_w"],
                            w["rg_ba"], w["rg_bx"], w["rg_lambda"], w["w_branch_a"], s)
    o = _attn_sample(q.reshape(b, s * N_HEADS, HEAD_DIM), k.reshape(b, s * N_KV_HEADS, HEAD_DIM),
                     v.reshape(b, s * N_KV_HEADS, HEAD_DIM), cache_k, cache_v, page_ids, s)
    y = _merge_ffn(x1, ya, o.reshape(t, -1), sgb, g2, sh3, sc3, g3, w["norm3_g"],
                   w["w_branch_b"], w["w_out"], w["ffn2_gate"], w["ffn2_up"], w["ffn2_down"], t, 1)
    return (y.reshape(b, s, d), k.reshape(b, s, N_KV_HEADS, HEAD_DIM), v.reshape(b, s, N_KV_HEADS, HEAD_DIM),
            xp[:, -(CONV_WIDTH - 1):], h_all.reshape(b, s, d_rnn)[:, -1])


def kernel(x_prompt, x_sample, c_prompt, c_sample, cache_k, cache_v, state_conv, state_rglru, page_table, w_ada, b_ada, norm1_g, ffn1_gate, ffn1_up, ffn1_down, norm2_g, w_in, conv_w, conv_b, rg_wa, rg_ba, rg_wx, rg_bx, rg_lambda, q_norm_g, k_norm_g, w_branch_a, w_branch_b, w_out, norm3_g, ffn2_gate, ffn2_up, ffn2_down):
    depth, n_pool = cache_k.shape[0], cache_k.shape[1]
    bp = x_prompt.shape[0]
    assert cache_k.shape[2:] == (PAGE_SIZE, N_KV_HEADS, HEAD_DIM)
    ck = cache_k.reshape(depth * n_pool, PAGE_SIZE * N_KV_HEADS, HEAD_DIM)
    cv = cache_v.reshape(depth * n_pool, PAGE_SIZE * N_KV_HEADS, HEAD_DIM)
    c_all = jnp.concatenate([c_prompt, c_sample], axis=0)
    pad = -c_all.shape[0] % SUBLANES
    c_all = jnp.pad(c_all, ((0, pad), (0, 0)))
    yp, ys = x_prompt, x_sample
    outs = [[] for _ in range(8)]
    for l in range(depth):
        p = dict(norm1_g=norm1_g[l], ffn1_gate=ffn1_gate[l], ffn1_up=ffn1_up[l], ffn1_down=ffn1_down[l],
                 norm2_g=norm2_g[l], w_in=w_in[l], conv_w=conv_w[l], conv_b=conv_b[l], rg_wa=rg_wa[l],
                 rg_ba=rg_ba[l], rg_wx=rg_wx[l], rg_bx=rg_bx[l], rg_lambda=rg_lambda[l], q_norm_g=q_norm_g[l],
                 k_norm_g=k_norm_g[l], w_branch_a=w_branch_a[l], w_branch_b=w_branch_b[l], w_out=w_out[l],
                 norm3_g=norm3_g[l], ffn2_gate=ffn2_gate[l], ffn2_up=ffn2_up[l], ffn2_down=ffn2_down[l])
        w = _layer_weights(p)
        mod = _ada(c_all, w_ada[l], b_ada[l])
        yp, kp, vp, cp, hp = _prompt_layer(yp, mod[:bp], w)
        ys, ks, vs, cs, hs = _sample_layer(ys, mod[bp:bp + x_sample.shape[0]], state_conv[l], state_rglru[l],
                                           ck, cv, page_table + l * n_pool, w)
        for lst, val in zip(outs, (kp, vp, cp, hp, ks, vs, cs, hs)):
            lst.append(val)
    return (yp, ys) + tuple(jnp.stack(o) for o in outs)
```

```python
import functools

import jax
import jax.numpy as jnp
from jax import lax
from jax.experimental import pallas as pl
from jax.experimental.pallas import tpu as pltpu

F32 = jnp.float32
BF16 = jnp.bfloat16

N_HEADS = 8
N_KV_HEADS = 4
HEAD_DIM = 128
GROUP = N_HEADS // N_KV_HEADS
MOBA_BLOCK = 256
MOBA_TOPK = 3
PAGE_SIZE = 128
N_RNN_BLOCKS = 8
RNN_BLOCK = 128
CONV_WIDTH = 4
RG_C = 8.0
N_SUBLAYERS = 3
EPS = 1e-6
ALIBI_SLOPES = tuple(2.0 ** (-8.0 * (h + 1) / N_HEADS) for h in range(N_HEADS))
ATTN_SCALE = HEAD_DIM ** -0.5
NEG = -1e30
LOG2E = 1.4426950408889634
SUBLANES = 8
VMEM_LIMIT = 56 * 1024 * 1024
PAGES_PER_STEP = 32
KV_BLOCKS_PER_STEP = 4
GATE_COST = 40
MXU_COLS = 256


def _params(n_axes):
    return pltpu.CompilerParams(dimension_semantics=("arbitrary",) * n_axes, vmem_limit_bytes=VMEM_LIMIT)


def _resident(shape):
    nd = len(shape)
    return pl.BlockSpec(shape, lambda *_: (0,) * nd, pipeline_mode=pl.Buffered(1))


def _dot(a, b):
    return jnp.dot(a, b, preferred_element_type=F32)


def _cast_chunks(weights, n_steps):
    return all(w.shape[0] % (n_steps * 2 * SUBLANES) == 0 for w in weights)


def _cast_spec(w, n_steps, step_of):
    rows = w.shape[0] // n_steps
    return pl.BlockSpec((rows, w.shape[1]), lambda *idx: (step_of(*idx), 0))


def _cast_rows(refs):
    n = len(refs) // 2
    for src, dst in zip(refs[:n], refs[n:]):
        dst[...] = src[...].astype(BF16)


def _dot_nt(a, b, precision=None):
    return lax.dot_general(a, b, (((1,), (1,)), ((), ())), precision=precision, preferred_element_type=F32)


def _dot_nt_split(a, b):
    a_hi = a.astype(BF16)
    b_hi = b.astype(BF16)
    a_lo = (a - a_hi.astype(F32)).astype(BF16)
    b_lo = (b - b_hi.astype(F32)).astype(BF16)
    return _dot_nt(a_hi, b_hi) + (_dot_nt(a_hi, b_lo) + _dot_nt(a_lo, b_hi))


def _rms(x, g):
    return x * lax.rsqrt(jnp.mean(x * x, axis=-1, keepdims=True) + EPS) * g


def _sigmoid(x):
    return 0.5 * jnp.tanh(0.5 * x) + 0.5


def _silu(x):
    return x * _sigmoid(x)


def _ada_kernel(c_ref, w_ref, b_ref, o_ref):
    a = _silu(c_ref[...]).astype(BF16)
    o_ref[...] = _dot(a, w_ref[...].astype(BF16)) + b_ref[...]


def _ada(c, w, b):
    m, d = c.shape
    n = w.shape[1]
    tn = n // 8
    return pl.pallas_call(
        _ada_kernel,
        grid=(n // tn,),
        in_specs=[pl.BlockSpec((m, d), lambda i: (0, 0)),
                  pl.BlockSpec((d, tn), lambda i: (0, i)),
                  pl.BlockSpec((1, tn), lambda i: (0, i))],
        out_specs=pl.BlockSpec((m, tn), lambda i: (0, i)),
        out_shape=jax.ShapeDtypeStruct((m, n), F32),
        compiler_params=_params(1),
        name="ada",
    )(c, w, b.reshape(1, n))


def _swiglu_acc(h, wg_ref, wu_ref, wd_ref, chunk):
    d_ff = wg_ref.shape[1]
    acc = None
    for c in range(d_ff // chunk):
        sl = slice(c * chunk, (c + 1) * chunk)
        act = (_silu(_dot(h, wg_ref[:, sl])) * _dot(h, wu_ref[:, sl])).astype(BF16)
        part = _dot(act, wd_ref[sl, :])
        acc = part if acc is None else acc + part
    return acc


def _ffn_kernel(x_ref, sh_ref, sc_ref, g_ref, ng_ref, wg_ref, wu_ref, wd_ref, *rest, chunk, n_cast):
    cast_src, o_ref, cast_dst = rest[:n_cast], rest[n_cast], rest[n_cast + 1:]
    x = x_ref[...]
    h = (_rms(x, ng_ref[...]) * (1.0 + sc_ref[...]) + sh_ref[...]).astype(BF16)
    o_ref[...] = x + 0.5 * g_ref[...] * _swiglu_acc(h, wg_ref, wu_ref, wd_ref, chunk)
    _cast_rows(cast_src + cast_dst)


def _mod_spec(mod, tm, tiles_per_group):
    arr, chunk = mod
    r, d = arr.shape[1], arr.shape[2] // (3 * N_SUBLAYERS)
    return pl.BlockSpec((None, r, d), lambda i: (i // tiles_per_group, 0, chunk))


def _ffn_chunk(d_ff):
    return 256 if d_ff % 256 == 0 else 128


def _ffn(x, sh, sc, g, ng, wg, wu, wd, tm, tiles_per_group, casts=()):
    t, d = x.shape
    d_ff = wg.shape[1]
    n_steps = t // tm
    row = pl.BlockSpec((tm, d), lambda i: (i, 0))
    cast_specs = [_cast_spec(w, n_steps, lambda i: i) for w in casts]
    out = pl.pallas_call(
        functools.partial(_ffn_kernel, chunk=_ffn_chunk(d_ff), n_cast=len(casts)),
        grid=(n_steps,),
        in_specs=[row, _mod_spec(sh, tm, tiles_per_group), _mod_spec(sc, tm, tiles_per_group),
                  _mod_spec(g, tm, tiles_per_group), _resident((1, d)),
                  _resident(wg.shape), _resident(wu.shape), _resident(wd.shape)] + cast_specs,
        out_specs=[row] + cast_specs,
        out_shape=[jax.ShapeDtypeStruct((t, d), F32)] + [jax.ShapeDtypeStruct(w.shape, BF16) for w in casts],
        compiler_params=_params(1),
        name="ffn",
    )(x, sh[0], sc[0], g[0], ng, wg, wu, wd, *casts)
    return out[0], tuple(out[1:])


def _inproj_kernel(x_ref, sh_ref, sc_ref, ng_ref, w_ref, qg_ref, kg_ref,
                   xr_ref, ug_ref, q_ref, k_ref, v_ref, sga_ref, sgb_ref):
    d_rnn = xr_ref.shape[1]
    dq = q_ref.shape[1]
    dk = k_ref.shape[1]
    x = x_ref[...]
    h = (_rms(x, ng_ref[...]) * (1.0 + sc_ref[...]) + sh_ref[...]).astype(BF16)
    o = 0
    xr_ref[...] = _dot(h, w_ref[:, o:o + d_rnn])
    o += d_rnn
    ug_ref[...] = jax.nn.gelu(_dot(h, w_ref[:, o:o + d_rnn]))
    o += d_rnn
    for hd in range(dq // HEAD_DIM):
        sl = slice(hd * HEAD_DIM, (hd + 1) * HEAD_DIM)
        q_ref[:, sl] = _rms(_dot(h, w_ref[:, o + hd * HEAD_DIM:o + (hd + 1) * HEAD_DIM]), qg_ref[...])
    o += dq
    for hd in range(dk // HEAD_DIM):
        sl = slice(hd * HEAD_DIM, (hd + 1) * HEAD_DIM)
        k_ref[:, sl] = _rms(_dot(h, w_ref[:, o + hd * HEAD_DIM:o + (hd + 1) * HEAD_DIM]), kg_ref[...])
    o += dk
    v_ref[...] = _dot(h, w_ref[:, o:o + dk])
    o += dk
    d = sga_ref.shape[1]
    sga_ref[...] = _sigmoid(_dot(h, w_ref[:, o:o + d]))
    o += d
    sgb_ref[...] = _sigmoid(_dot(h, w_ref[:, o:o + d]))


def _inproj(x, sh, sc, ng, w_in, qg, kg, tm, tiles_per_group):
    t, d = x.shape
    dq, dk = N_HEADS * HEAD_DIM, N_KV_HEADS * HEAD_DIM
    d_rnn = (w_in.shape[1] - dq - 2 * dk - 2 * d) // 2

    def row(w):
        return pl.BlockSpec((tm, w), lambda i: (i, 0))

    widths = (d_rnn, d_rnn, dq, dk, dk, d, d)
    return pl.pallas_call(
        _inproj_kernel,
        grid=(t // tm,),
        in_specs=[row(d), _mod_spec(sh, tm, tiles_per_group), _mod_spec(sc, tm, tiles_per_group),
                  _resident((1, d)), _resident(w_in.shape), _resident((1, HEAD_DIM)), _resident((1, HEAD_DIM))],
        out_specs=[row(w) for w in widths],
        out_shape=[jax.ShapeDtypeStruct((t, w), F32) for w in widths],
        compiler_params=_params(1),
        name="inproj",
    )(x, sh[0], sc[0], ng, w_in, qg, kg)


def _rg_gate_block(xcn, n, wg_ref, bra_ref, brx_ref, sp, a_ref, b_ref):
    sl = slice(n * RNN_BLOCK, (n + 1) * RNN_BLOCK)
    z = _dot(xcn.astype(BF16), wg_ref[n])
    r = _sigmoid(z[:, :RNN_BLOCK] + bra_ref[:, sl])
    i = _sigmoid(z[:, RNN_BLOCK:] + brx_ref[:, sl])
    log_a = -RG_C * r * sp[:, sl]
    a = jnp.exp(log_a)
    a_ref[:, sl] = a
    b_ref[:, sl] = jnp.sqrt(1.0 - a * a) * (i * xcn)


def _rg_gates(xc, wg_ref, bra_ref, brx_ref, lam_ref, a_ref, b_ref):
    sp = jax.nn.softplus(-lam_ref[...])
    for n in range(N_RNN_BLOCKS):
        _rg_gate_block(xc[:, n * RNN_BLOCK:(n + 1) * RNN_BLOCK], n, wg_ref, bra_ref, brx_ref, sp, a_ref, b_ref)


def _scan_rows(a, b, row_in_seg, steps):
    for d in steps:
        keep = row_in_seg >= d
        a_prev = jnp.where(keep, pltpu.roll(a, d, 0), 1.0)
        b_prev = jnp.where(keep, pltpu.roll(b, d, 0), 0.0)
        b = a * b_prev + b
        a = a * a_prev
    return a, b


def _mixin_prompt_kernel(x_ref, sh_ref, sc_ref, ng_ref, w_ref, qg_ref, kg_ref, conv0_ref, h0_ref, cw_ref, cb_ref,
                         wg_ref, bra_ref, brx_ref, lam_ref, wba_ref,
                         q_ref, k_ref, v_ref, sgb_ref, ya_ref, k4_ref, v4_ref, nconv_ref, hlast_ref,
                         xbuf, a_s, b_s, hc_s, ug_s, sga_s, *, ts):
    s = pl.program_id(1)
    w = xbuf.shape[1]
    dq = q_ref.shape[1]
    dk = k_ref.shape[1]
    d = sgb_ref.shape[1]
    tail = CONV_WIDTH - 1
    lo = SUBLANES - tail

    @pl.when(s == 0)
    def _():
        xbuf[lo:SUBLANES, :] = conv0_ref[...]
        hc_s[...] = jnp.broadcast_to(h0_ref[...], hc_s.shape)

    x = x_ref[...]
    h = (_rms(x, ng_ref[...]) * (1.0 + sc_ref[...]) + sh_ref[...]).astype(BF16)
    o_gate, o_q = w, 2 * w
    o_k = o_q + dq
    o_v = o_k + dk
    o_ga = o_v + dk
    o_gb = o_ga + d
    cw_ = MXU_COLS

    def proj(lo):
        return _dot(h, w_ref[:, lo:lo + cw_])

    def normed_heads(ref, gain_ref, base, c):
        z = proj(base + c * cw_)
        for i in range(cw_ // HEAD_DIM):
            col = c * cw_ + i * HEAD_DIM
            ref[:, col:col + HEAD_DIM] = _rms(z[:, i * HEAD_DIM:(i + 1) * HEAD_DIM], gain_ref[...])

    def store(ref, base, c, fn):
        ref[:, c * cw_:(c + 1) * cw_] = fn(proj(base + c * cw_))

    jobs = [functools.partial(normed_heads, q_ref, qg_ref, o_q, c) for c in range(dq // cw_)]
    jobs += [functools.partial(normed_heads, k_ref, kg_ref, o_k, c) for c in range(dk // cw_)]
    jobs += [functools.partial(store, v_ref, o_v, c, lambda z: z) for c in range(dk // cw_)]
    jobs += [functools.partial(store, sgb_ref, o_gb, c, _sigmoid) for c in range(d // cw_)]
    jobs += [functools.partial(store, ug_s, o_gate, c, jax.nn.gelu) for c in range(w // cw_)]
    jobs += [functools.partial(store, sga_s, o_ga, c, _sigmoid) for c in range(d // cw_)]

    def by_head(src_ref, dst_ref, hd):
        dst_ref[:, hd, :] = src_ref[:, hd * HEAD_DIM:(hd + 1) * HEAD_DIM]
    cost = [GATE_COST] * N_RNN_BLOCKS + [1] * (ts // SUBLANES)
    due, acc_cost = [], 0
    for c_ in cost:
        acc_cost += c_
        due.append((acc_cost * len(jobs)) // sum(cost))
    done = 0

    for c in range(w // cw_):
        xbuf[SUBLANES:SUBLANES + ts, c * cw_:(c + 1) * cw_] = proj(c * cw_)
    sp = jax.nn.softplus(-lam_ref[...])
    for n in range(N_RNN_BLOCKS):
        sl = slice(n * RNN_BLOCK, (n + 1) * RNN_BLOCK)
        xcn = cb_ref[:, sl] + cw_ref[0:1, sl] * xbuf[lo:lo + ts, sl]
        for k in range(1, CONV_WIDTH):
            xcn = xcn + cw_ref[k:k + 1, sl] * xbuf[lo + k:lo + k + ts, sl]
        _rg_gate_block(xcn, n, wg_ref, bra_ref, brx_ref, sp, a_s, b_s)
        while done < due[n]:
            jobs[done]()
            done += 1
    new_tail = xbuf[ts + lo:ts + SUBLANES, :]
    nconv_ref[...] = new_tail
    xbuf[lo:SUBLANES, :] = new_tail

    row = lax.broadcasted_iota(jnp.int32, (SUBLANES, w), 0)
    hc = hc_s[...]
    for c in range(ts // SUBLANES):
        rows = slice(c * SUBLANES, (c + 1) * SUBLANES)
        a_cum, b_cum = _scan_rows(a_s[rows, :], b_s[rows, :], row, (1, 2, 4))
        hs = a_cum * hc + b_cum
        a_s[rows, :] = hs
        hc = jnp.broadcast_to(hs[SUBLANES - 1:SUBLANES, :], (SUBLANES, w))
        while done < due[N_RNN_BLOCKS + c]:
            jobs[done]()
            done += 1
    hc_s[...] = hc
    hlast_ref[...] = hc[0:1, :]
    u = (a_s[...] * ug_s[...]).astype(BF16)
    ya = _dot(u, wba_ref[...])
    for hd in range(dk // HEAD_DIM):
        by_head(k_ref, k4_ref, hd)
        by_head(v_ref, v4_ref, hd)
    ya_ref[...] = sga_s[...] * ya


def _mixin_prompt(x, sh, sc, ng, w_in, qg, kg, conv0, h0, cw, cb, wg, bra, brx, lam, wba, ts):
    b, s, d = x.shape
    w = wba.shape[0]
    dq, dk = N_HEADS * HEAD_DIM, N_KV_HEADS * HEAD_DIM
    tail = CONV_WIDTH - 1

    def seq(width):
        return pl.BlockSpec((None, ts, width), lambda i, j: (i, j, 0))

    def per_seq(rows, width):
        return pl.BlockSpec((None, rows, width), lambda i, j: (i, 0, 0))

    def mod_row(mod):
        return pl.BlockSpec((None, 1, d), lambda i, j: (i, 0, mod[1]))

    widths = (dq, dk, dk, d, d)
    by_head = pl.BlockSpec((None, ts, N_KV_HEADS, HEAD_DIM), lambda i, j: (i, j, 0, 0))
    return pl.pallas_call(
        functools.partial(_mixin_prompt_kernel, ts=ts),
        grid=(b, s // ts),
        in_specs=[seq(d), mod_row(sh), mod_row(sc), _resident((1, d)), _resident(w_in.shape),
                  _resident((1, HEAD_DIM)), _resident((1, HEAD_DIM)), per_seq(tail, w), per_seq(1, w),
                  _resident(cw.shape), _resident((1, w)), _resident(wg.shape), _resident((1, w)), _resident((1, w)),
                  _resident((1, w)), _resident(wba.shape)],
        out_specs=[seq(wd) for wd in widths] + [by_head, by_head, per_seq(tail, w), per_seq(1, w)],
        out_shape=[jax.ShapeDtypeStruct((b, s, wd), F32) for wd in widths]
        + [jax.ShapeDtypeStruct((b, s, N_KV_HEADS, HEAD_DIM), F32)] * 2
        + [jax.ShapeDtypeStruct((b, tail, w), F32), jax.ShapeDtypeStruct((b, 1, w), F32)],
        scratch_shapes=[pltpu.VMEM((ts + SUBLANES, w), F32), pltpu.VMEM((ts, w), F32), pltpu.VMEM((ts, w), F32),
                        pltpu.VMEM((SUBLANES, w), F32), pltpu.VMEM((ts, w), F32), pltpu.VMEM((ts, d), F32)],
        compiler_params=_params(2),
        name="mixin_prompt",
    )(x, sh[0], sc[0], ng, w_in, qg, kg, conv0, h0, cw, cb, wg, bra, brx, lam, wba)


def _rnn_sample_kernel(xs_ref, ug_ref, sga_ref, h0_ref, cw_ref, cb_ref, wg_ref, bra_ref, brx_ref, lam_ref, wba_ref,
                       ya_ref, h_ref, a_s, b_s, *, seg):
    m, w = ug_ref.shape
    xc = cb_ref[...] + cw_ref[0:1, :] * xs_ref[0]
    for k in range(1, CONV_WIDTH):
        xc = xc + cw_ref[k:k + 1, :] * xs_ref[k]
    _rg_gates(xc, wg_ref, bra_ref, brx_ref, lam_ref, a_s, b_s)
    row_in_seg = lax.broadcasted_iota(jnp.int32, (SUBLANES, w), 0) % seg
    steps = tuple(d for d in (1, 2, 4) if d < seg)
    for c in range(m // SUBLANES):
        rows = slice(c * SUBLANES, (c + 1) * SUBLANES)
        a_cum, b_cum = _scan_rows(a_s[rows, :], b_s[rows, :], row_in_seg, steps)
        h_ref[rows, :] = a_cum * h0_ref[rows, :] + b_cum
    u = (h_ref[...] * ug_ref[...]).astype(BF16)
    ya_ref[...] = sga_ref[...] * _dot(u, wba_ref[...])


def _rnn_sample(xs, ug, sga, h0_rows, cw, cb, wg, bra, brx, lam, wba, seg):
    m, w = ug.shape
    d = wba.shape[1]
    return pl.pallas_call(
        functools.partial(_rnn_sample_kernel, seg=seg),
        out_shape=[jax.ShapeDtypeStruct((m, d), F32), jax.ShapeDtypeStruct((m, w), F32)],
        scratch_shapes=[pltpu.VMEM((m, w), F32), pltpu.VMEM((m, w), F32)],
        compiler_params=pltpu.CompilerParams(vmem_limit_bytes=VMEM_LIMIT),
        name="rnn_sample",
    )(xs, ug, sga, h0_rows, cw, cb, wg, bra, brx, lam, wba)


V_ROWS = HEAD_DIM + 16


def _attn_prompt_kernel(slope_ref, q_ref, k_ref, v_ref, *rest, nb, gt, n_cast):
    cast_src, o_ref, cast_dst = rest[:n_cast], rest[n_cast], rest[n_cast + 1:2 * n_cast + 1]
    kb_s, vt_s, kbg_s, vtg_s, km_s, q2_s, bias_s, colb_s, acc_s = rest[2 * n_cast + 1:]
    g = pl.program_id(1)
    blk = MOBA_BLOCK
    s_len = nb * blk
    q_scale = ATTN_SCALE * LOG2E
    _cast_rows(cast_src + cast_dst)

    ones_rows = (lax.broadcasted_iota(jnp.int32, (V_ROWS - HEAD_DIM, blk), 0) == 0).astype(BF16)
    for n in range(nb):
        rows = slice(n * blk, (n + 1) * blk)
        grp, sub = n // gt, slice((n % gt) * blk, (n % gt + 1) * blk)
        kn = k_ref[rows, :]
        kb = kn.astype(BF16)
        vt = v_ref[rows, :].T.astype(BF16)
        kb_s[n] = kb
        kbg_s[grp, sub, :] = kb
        vt_s[n, 0:HEAD_DIM, :] = vt
        vt_s[n, HEAD_DIM:V_ROWS, :] = ones_rows
        vtg_s[grp, 0:HEAD_DIM, sub] = vt
        vtg_s[grp, HEAD_DIM:V_ROWS, sub] = ones_rows
        km_s[n:n + 1, :] = jnp.sum(kn, axis=0, keepdims=True) * (1.0 / blk)
        q2_s[n] = (q_ref[rows, :] * q_scale).astype(BF16)

    blk_id = lax.broadcasted_iota(jnp.int32, (nb, s_len), 0)
    q_blk = lax.broadcasted_iota(jnp.int32, (nb, s_len), 1) // blk
    fully_past = blk_id < q_blk
    blocks_ahead = ((blk_id - q_blk) * blk).astype(F32)
    key_off = lax.broadcasted_iota(jnp.int32, (blk, blk), 0)
    causal = key_off <= lax.broadcasted_iota(jnp.int32, (blk, blk), 1)
    km = km_s[...]
    for hh in range(GROUP):
        slope2 = slope_ref[g * GROUP + hh] * LOG2E
        cols = slice(hh * HEAD_DIM, (hh + 1) * HEAD_DIM)
        gs = jnp.where(fully_past, _dot_nt_split(km, q_ref[:, cols]), -jnp.inf)
        rank = jnp.zeros((nb, s_len), jnp.int32)
        for m in range(nb):
            gm = gs[m:m + 1, :]
            tie = (blk_id > m).astype(jnp.int32)
            rank = rank + jnp.where(gm > gs, 1, jnp.where(gm == gs, tie, 0))
        chosen = jnp.logical_and(fully_past, rank < MOBA_TOPK)
        bias = jnp.where(chosen, slope2 * blocks_ahead, NEG)
        for jq in range(nb):
            bias_s[hh, jq] = bias[:, jq * blk:(jq + 1) * blk]
        colb_s[hh] = slope2 * key_off.astype(F32)

    heads = [slice(hh * HEAD_DIM, (hh + 1) * HEAD_DIM) for hh in range(GROUP)]

    def group_scores(gi, j, q2):
        kg = kbg_s[gi]
        s_grp = [_dot_nt(kg, q2[:, cols]) for cols in heads]
        parts, tops = [], []
        for hh in range(GROUP):
            ps = [s_grp[hh][t * blk:(t + 1) * blk, :] + colb_s[hh] + bias_s[hh, j, pl.ds(gi * gt + t, 1), :]
                  for t in range(gt)]
            top = jnp.max(ps[0], axis=0, keepdims=True)
            for part in ps[1:]:
                top = jnp.maximum(top, jnp.max(part, axis=0, keepdims=True))
            parts.append(ps)
            tops.append(top)
        return parts, tops

    def write_out(jb):
        rows = pl.ds(pl.multiple_of(jb * blk, blk), blk)
        for hh in range(GROUP):
            acc = acc_s[hh]
            o = acc[0:HEAD_DIM, :] / acc[HEAD_DIM:HEAD_DIM + 1, :]
            o_ref[rows, heads[hh]] = o.T.astype(o_ref.dtype)

    acc_s[...] = jnp.ones_like(acc_s)

    def q_block(j, _):
        q2 = q2_s[j]
        kd = kb_s[j]
        vd = vt_s[j]
        s_own = [_dot_nt(kd, q2[:, cols]) for cols in heads]
        parts0, tops0 = group_scores(0, j, q2)
        write_out(jnp.maximum(j - 1, 0))
        m_run, probs = [], []
        for hh in range(GROUP):
            s = jnp.where(causal, s_own[hh] + colb_s[hh], NEG)
            m1 = jnp.maximum(jnp.max(s, axis=0, keepdims=True), tops0[hh])
            p_own = jnp.exp2(s - m1).astype(BF16)
            p_grp = jnp.concatenate([jnp.exp2(part - m1).astype(BF16) for part in parts0[hh]], axis=0)
            probs.append((p_own, p_grp))
            m_run.append(m1)
        vg0 = vtg_s[0]
        for hh in range(GROUP):
            acc_s[hh] = _dot(vd, probs[hh][0]) + _dot(vg0, probs[hh][1])

        def kv_group(gi, m_run):
            vg = vtg_s[gi]
            parts, tops = group_scores(gi, j, q2)
            out, probs = [], []
            for hh in range(GROUP):
                m_new = jnp.maximum(m_run[hh], tops[hh])
                probs.append(jnp.concatenate([jnp.exp2(part - m_new).astype(BF16) for part in parts[hh]], axis=0))
                out.append(m_new)
            pv = [_dot(vg, p) for p in probs]
            for hh in range(GROUP):
                acc_s[hh] = jnp.exp2(m_run[hh] - out[hh]) * acc_s[hh] + pv[hh]
            return tuple(out)

        lax.fori_loop(1, (j + gt - 1) // gt, kv_group, tuple(m_run))
        return 0

    lax.fori_loop(0, nb, q_block, 0)
    write_out(nb - 1)


def _attn_prompt(q, k, v, casts=()):
    b, s, _ = q.shape
    assert s % MOBA_BLOCK == 0
    nb = s // MOBA_BLOCK
    gt = KV_BLOCKS_PER_STEP if nb % KV_BLOCKS_PER_STEP == 0 else 1
    gw = GROUP * HEAD_DIM
    slopes = jnp.asarray(ALIBI_SLOPES, F32)
    kv_spec = pl.BlockSpec((None, s, HEAD_DIM), lambda i, g: (i, 0, g))
    q_spec = pl.BlockSpec((None, s, gw), lambda i, g: (i, 0, g))
    cast_specs = [_cast_spec(w, b * N_KV_HEADS, lambda i, g: i * N_KV_HEADS + g) for w in casts]
    out = pl.pallas_call(
        functools.partial(_attn_prompt_kernel, nb=nb, gt=gt, n_cast=len(casts)),
        grid=(b, N_KV_HEADS),
        in_specs=[pl.BlockSpec(memory_space=pltpu.SMEM), q_spec, kv_spec, kv_spec] + cast_specs,
        out_specs=[q_spec] + cast_specs,
        out_shape=[jax.ShapeDtypeStruct(q.shape, BF16)] + [jax.ShapeDtypeStruct(w.shape, BF16) for w in casts],
        scratch_shapes=[pltpu.VMEM((nb, MOBA_BLOCK, HEAD_DIM), BF16), pltpu.VMEM((nb, V_ROWS, MOBA_BLOCK), BF16),
                        pltpu.VMEM((nb // gt, gt * MOBA_BLOCK, HEAD_DIM), BF16),
                        pltpu.VMEM((nb // gt, V_ROWS, gt * MOBA_BLOCK), BF16),
                        pltpu.VMEM((nb, HEAD_DIM), F32), pltpu.VMEM((nb, MOBA_BLOCK, gw), BF16),
                        pltpu.VMEM((GROUP, nb, nb, MOBA_BLOCK), F32), pltpu.VMEM((GROUP, MOBA_BLOCK, MOBA_BLOCK), F32),
                        pltpu.VMEM((GROUP, V_ROWS, MOBA_BLOCK), F32)],
        compiler_params=_params(2),
        name="attn_prompt",
    )(slopes, q, k, v, *casts)
    return out[0], tuple(out[1:])


def _row_slopes(head):
    out = jnp.zeros(head.shape, F32)
    for h, sl in enumerate(ALIBI_SLOPES):
        out = jnp.where(head == h, sl, out)
    return out


def _attn_sample_kernel(pt_ref, q_ref, kn_ref, vn_ref, *rest, pp, n_blk, past, n_new):
    k_pages, v_pages = rest[:pp], rest[pp:2 * pp]
    o_ref = rest[2 * pp]
    km_s, m_s, l_s, op_s, bias_s = rest[2 * pp + 1:]
    del pt_ref
    s_id = pl.program_id(1)
    nq = q_ref.shape[0]
    kv = N_KV_HEADS
    cols = MOBA_BLOCK * kv
    q = q_ref[...]
    qb = (q * (ATTN_SCALE * LOG2E)).astype(BF16)

    row = lax.broadcasted_iota(jnp.int32, (nq, cols), 0)
    col = lax.broadcasted_iota(jnp.int32, (nq, cols), 1)
    head = row % N_HEADS
    step = row // N_HEADS

    @pl.when(s_id == 0)
    def _():
        same_kv = (col % kv) == head // GROUP
        back = (step + MOBA_BLOCK - col // kv).astype(F32)
        bias_s[...] = jnp.where(same_kv, -(_row_slopes(head) * LOG2E) * back, NEG)
        m_s[...] = jnp.zeros_like(m_s)
        l_s[...] = jnp.zeros_like(l_s)

    slope_col = _row_slopes(head[:, 0:1]) * LOG2E
    lane = lax.broadcasted_iota(jnp.int32, (nq, 128), 1)
    per_step = pp // 2
    first_blk = s_id * per_step
    ksums, scores, probs, partials = [], [], [], []
    for i in range(per_step):
        k0, k1 = k_pages[2 * i][...], k_pages[2 * i + 1][...]
        ksums.append(jnp.sum(k0.reshape(-1, SUBLANES, HEAD_DIM), axis=0)
                     + jnp.sum(k1.reshape(-1, SUBLANES, HEAD_DIM), axis=0))
        kb = jnp.concatenate([k0, k1], axis=0).astype(BF16)
        scores.append(_dot_nt(qb, kb) + bias_s[...])
    for s in scores:
        m = jnp.max(s, axis=-1, keepdims=True)
        p = jnp.exp2(s - m)
        probs.append((m, jnp.sum(p, axis=-1, keepdims=True), p.astype(BF16)))
    for i, (m, l, p) in enumerate(probs):
        vb = jnp.concatenate([v_pages[2 * i][...], v_pages[2 * i + 1][...]], axis=0).astype(BF16)
        partials.append((m, l, _dot(p, vb)))
    m_new, l_new = m_s[...], l_s[...]
    for i, (m, l, o_part) in enumerate(partials):
        n = first_blk + i
        km_s[pl.ds(pl.multiple_of(n * SUBLANES, SUBLANES), SUBLANES), :] = ksums[i] * (1.0 / MOBA_BLOCK)
        op_s[n] = o_part
        m_new = jnp.where(lane == n, m + slope_col * ((n + 1) * MOBA_BLOCK - past).astype(F32), m_new)
        l_new = jnp.where(lane == n, l, l_new)
    m_s[...] = m_new
    l_s[...] = l_new

    @pl.when(s_id == pl.num_programs(1) - 1)
    def _():
        gcols = n_blk * SUBLANES
        gfull = _dot_nt(q, km_s[...], precision=lax.Precision.HIGHEST)
        gsum = gfull + pltpu.roll(gfull, gcols - kv, 1)
        grow = lax.broadcasted_iota(jnp.int32, (nq, gcols), 0)
        gcol = lax.broadcasted_iota(jnp.int32, (nq, gcols), 1)
        gs = jnp.where((gcol % SUBLANES) == (grow % N_HEADS) // GROUP, gsum, -jnp.inf)
        gcol_f = gcol.astype(F32)
        chosen = jnp.zeros((nq, 128), F32)
        for _ in range(MOBA_TOPK):
            best = jnp.max(gs, axis=-1, keepdims=True)
            first = jnp.min(jnp.where(gs == best, gcol_f, float(gcols)), axis=-1, keepdims=True)
            chosen = jnp.where(lane == (first.astype(jnp.int32) // SUBLANES), 1.0, chosen)
            gs = jnp.where(gcol_f == first, -jnp.inf, gs)
        is_chosen = chosen > 0.0

        nk = n_new * kv
        orow = lax.broadcasted_iota(jnp.int32, (nq, nk), 0)
        ocol = lax.broadcasted_iota(jnp.int32, (nq, nk), 1)
        ohead, ostep = orow % N_HEADS, orow // N_HEADS
        ok = jnp.logical_and((ocol % kv) == ohead // GROUP, ocol // kv <= ostep)
        so = _dot_nt(qb, kn_ref[...].astype(BF16)) - (_row_slopes(ohead) * LOG2E) * (ostep - ocol // kv).astype(F32)
        so = jnp.where(ok, so, NEG)

        m_all = jnp.maximum(jnp.max(so, axis=-1, keepdims=True),
                            jnp.max(jnp.where(is_chosen, m_s[...], NEG), axis=-1, keepdims=True))
        wgt = jnp.where(is_chosen, jnp.exp2(m_s[...] - m_all), 0.0)
        po = jnp.exp2(so - m_all)
        l_all = jnp.sum(wgt * l_s[...], axis=-1, keepdims=True) + jnp.sum(po, axis=-1, keepdims=True)
        acc = _dot(po.astype(BF16), vn_ref[...].astype(BF16))
        for n in range(n_blk):
            acc = acc + wgt[:, n:n + 1] * op_s[n]
        o_ref[...] = acc / l_all


def _attn_sample(q, k_new, v_new, cache_k, cache_v, page_ids, n_new):
    b, nq, _ = q.shape
    n_pages = page_ids.shape[1]
    past = n_pages * PAGE_SIZE
    pp = PAGES_PER_STEP
    assert MOBA_BLOCK == 2 * PAGE_SIZE and past % MOBA_BLOCK == 0 and n_pages % pp == 0
    n_blk = past // MOBA_BLOCK
    assert MOBA_TOPK <= n_blk <= 128 and n_new <= MOBA_BLOCK
    rows = PAGE_SIZE * N_KV_HEADS

    def page(i):
        return pl.BlockSpec((None, rows, HEAD_DIM), lambda bi, si, pt: (pt[bi * n_pages + si * pp + i], 0, 0))

    def per_seq(r):
        return pl.BlockSpec((None, r, HEAD_DIM), lambda bi, si, pt: (bi, 0, 0))

    return pl.pallas_call(
        functools.partial(_attn_sample_kernel, pp=pp, n_blk=n_blk, past=past, n_new=n_new),
        grid_spec=pltpu.PrefetchScalarGridSpec(
            num_scalar_prefetch=1,
            grid=(b, n_pages // pp),
            in_specs=[per_seq(nq), per_seq(n_new * N_KV_HEADS), per_seq(n_new * N_KV_HEADS)]
            + [page(i) for i in range(pp)] * 2,
            out_specs=per_seq(nq),
            scratch_shapes=[pltpu.VMEM((n_blk * SUBLANES, HEAD_DIM), F32), pltpu.VMEM((nq, 128), F32),
                            pltpu.VMEM((nq, 128), F32), pltpu.VMEM((n_blk, nq, HEAD_DIM), F32),
                            pltpu.VMEM((nq, MOBA_BLOCK * N_KV_HEADS), F32)]),
        out_shape=jax.ShapeDtypeStruct((b, nq, HEAD_DIM), F32),
        compiler_params=_params(2),
        name="attn_sample",
    )(page_ids.reshape(-1), q, k_new, v_new, *([cache_k] * pp), *([cache_v] * pp))


def _merge_ffn_kernel(x_ref, ya_ref, o_ref_in, sgb_ref, g2_ref, sh_ref, sc_ref, g3_ref, ng_ref,
                      wbb_ref, wout_ref, wg_ref, wu_ref, wd_ref, out_ref, *, chunk):
    yb = _dot(o_ref_in[...].astype(BF16), wbb_ref[...])
    merged = (ya_ref[...] + sgb_ref[...] * yb).astype(BF16)
    x = x_ref[...] + g2_ref[...] * _dot(merged, wout_ref[...])
    h = (_rms(x, ng_ref[...]) * (1.0 + sc_ref[...]) + sh_ref[...]).astype(BF16)
    out_ref[...] = x + 0.5 * g3_ref[...] * _swiglu_acc(h, wg_ref, wu_ref, wd_ref, chunk)


def _merge_ffn(x, ya, o, sgb, g2, sh, sc, g3, ng, wbb, wout, wg, wu, wd, tm, tiles_per_group):
    t, d = x.shape
    row = pl.BlockSpec((tm, d), lambda i: (i, 0))
    mod = functools.partial(_mod_spec, tm=tm, tiles_per_group=tiles_per_group)
    return pl.pallas_call(
        functools.partial(_merge_ffn_kernel, chunk=_ffn_chunk(wg.shape[1])),
        grid=(t // tm,),
        in_specs=[row, row, pl.BlockSpec((tm, o.shape[1]), lambda i: (i, 0)), row,
                  mod(g2), mod(sh), mod(sc), mod(g3), _resident((1, d)),
                  _resident(wbb.shape), _resident(wout.shape), _resident(wg.shape), _resident(wu.shape),
                  _resident(wd.shape)],
        out_specs=row,
        out_shape=jax.ShapeDtypeStruct((t, d), F32),
        compiler_params=_params(1),
        name="merge_ffn",
    )(x, ya, o, sgb, g2[0], sh[0], sc[0], g3[0], ng, wbb, wout, wg, wu, wd)


def _token_tile(t):
    return 512 if t % 512 == 0 else t


EARLY_BF16 = ("ffn1_gate", "ffn1_up", "ffn1_down")
MIXIN_BF16 = ("w_in", "w_branch_a")
MERGE_BF16 = ("w_branch_b", "w_out", "ffn2_gate", "ffn2_up", "ffn2_down")


def _layer_weights(p):
    w = dict(p)
    for name in EARLY_BF16:
        w[name] = p[name].astype(BF16)
    w["rg_w"] = jnp.concatenate([p["rg_wa"], p["rg_wx"]], axis=-1).astype(BF16)
    for name in ("norm1_g", "norm2_g", "norm3_g", "conv_b", "rg_ba", "rg_bx", "rg_lambda", "q_norm_g", "k_norm_g"):
        w[name] = p[name].reshape(1, -1)
    return w


def _hidden_casts(w, names, n_steps):
    if _cast_chunks([w[n] for n in names], n_steps):
        return tuple(w[n] for n in names)
    for n in names:
        w[n] = w[n].astype(BF16)
    return ()


def _prompt_layer(x, mod, w):
    b, s, d = x.shape
    t = b * s
    tm = _token_tile(s)
    tpg = s // tm
    sh1, sc1, g1, sh2, sc2, g2, sh3, sc3, g3 = [(mod[:, None, :], c) for c in range(3 * N_SUBLAYERS)]
    casts = _hidden_casts(w, MIXIN_BF16, t // tm)
    x1, done = _ffn(x.reshape(t, d), sh1, sc1, g1, w["norm1_g"], w["ffn1_gate"], w["ffn1_up"], w["ffn1_down"],
                    tm, tpg, casts)
    w.update(zip(MIXIN_BF16, done))
    d_rnn = w["w_branch_a"].shape[0]
    q, k, v, sgb, ya, k_heads, v_heads, new_conv, h_last = _mixin_prompt(
        x1.reshape(b, s, d), sh2, sc2, w["norm2_g"], w["w_in"], w["q_norm_g"], w["k_norm_g"],
        jnp.zeros((b, CONV_WIDTH - 1, d_rnn), F32), jnp.zeros((b, 1, d_rnn), F32),
        w["conv_w"], w["conv_b"], w["rg_w"], w["rg_ba"], w["rg_bx"], w["rg_lambda"], w["w_branch_a"], tm)
    sgb = sgb.reshape(t, d)
    casts = _hidden_casts(w, MERGE_BF16, b * N_KV_HEADS)
    o, done = _attn_prompt(q.reshape(b, s, -1), k.reshape(b, s, -1), v.reshape(b, s, -1), casts)
    w.update(zip(MERGE_BF16, done))
    y = _merge_ffn(x1, ya.reshape(t, d), o.reshape(t, -1), sgb, g2, sh3, sc3, g3, w["norm3_g"],
                   w["w_branch_b"], w["w_out"], w["ffn2_gate"], w["ffn2_up"], w["ffn2_down"], tm, tpg)
    return y.reshape(b, s, d), k_heads, v_heads, new_conv, h_last.reshape(b, d_rnn)


def _sample_layer(x, mod, conv_buf, h0, cache_k, cache_v, page_ids, w):
    b, s, d = x.shape
    t = b * s
    per_token = jnp.repeat(mod, s, axis=0)[None]
    sh1, sc1, g1, sh2, sc2, g2, sh3, sc3, g3 = [(per_token, c) for c in range(3 * N_SUBLAYERS)]
    x1, _ = _ffn(x.reshape(t, d), sh1, sc1, g1, w["norm1_g"], w["ffn1_gate"], w["ffn1_up"], w["ffn1_down"], t, 1)
    xr, ug, q, k, v, sga, sgb = _inproj(x1, sh2, sc2, w["norm2_g"], w["w_in"], w["q_norm_g"], w["k_norm_g"], t, 1)
    d_rnn = xr.shape[1]
    xp = jnp.concatenate([conv_buf, xr.reshape(b, s, d_rnn)], axis=1)
    taps = jnp.stack([xp[:, kk:kk + s].reshape(t, d_rnn) for kk in range(CONV_WIDTH)])
    ya, h_all = _rnn_sample(taps, ug, sga, jnp.repeat(h0, s, axis=0), w["conv_w"], w["conv_b"], w["rg_w"],
                            w["rg_ba"], w["rg_bx"], w["rg_lambda"], w["w_branch_a"], s)
    o = _attn_sample(q.reshape(b, s * N_HEADS, HEAD_DIM), k.reshape(b, s * N_KV_HEADS, HEAD_DIM),
                     v.reshape(b, s * N_KV_HEADS, HEAD_DIM), cache_k, cache_v, page_ids, s)
    y = _merge_ffn(x1, ya, o.reshape(t, -1), sgb, g2, sh3, sc3, g3, w["norm3_g"],
                   w["w_branch_b"], w["w_out"], w["ffn2_gate"], w["ffn2_up"], w["ffn2_down"], t, 1)
    return (y.reshape(b, s, d), k.reshape(b, s, N_KV_HEADS, HEAD_DIM), v.reshape(b, s, N_KV_HEADS, HEAD_DIM),
            xp[:, -(CONV_WIDTH - 1):], h_all.reshape(b, s, d_rnn)[:, -1])


def kernel(x_prompt, x_sample, c_prompt, c_sample, cache_k, cache_v, state_conv, state_rglru, page_table, w_ada, b_ada, norm1_g, ffn1_gate, ffn1_up, ffn1_down, norm2_g, w_in, conv_w, conv_b, rg_wa, rg_ba, rg_wx, rg_bx, rg_lambda, q_norm_g, k_norm_g, w_branch_a, w_branch_b, w_out, norm3_g, ffn2_gate, ffn2_up, ffn2_down):
    depth, n_pool = cache_k.shape[0], cache_k.shape[1]
    bp = x_prompt.shape[0]
    assert cache_k.shape[2:] == (PAGE_SIZE, N_KV_HEADS, HEAD_DIM)
    ck = cache_k.reshape(depth * n_pool, PAGE_SIZE * N_KV_HEADS, HEAD_DIM)
    cv = cache_v.reshape(depth * n_pool, PAGE_SIZE * N_KV_HEADS, HEAD_DIM)
    c_all = jnp.concatenate([c_prompt, c_sample], axis=0)
    pad = -c_all.shape[0] % SUBLANES
    c_all = jnp.pad(c_all, ((0, pad), (0, 0)))
    yp, ys = x_prompt, x_sample
    outs = [[] for _ in range(8)]
    for l in range(depth):
        p = dict(norm1_g=norm1_g[l], ffn1_gate=ffn1_gate[l], ffn1_up=ffn1_up[l], ffn1_down=ffn1_down[l],
                 norm2_g=norm2_g[l], w_in=w_in[l], conv_w=conv_w[l], conv_b=conv_b[l], rg_wa=rg_wa[l],
                 rg_ba=rg_ba[l], rg_wx=rg_wx[l], rg_bx=rg_bx[l], rg_lambda=rg_lambda[l], q_norm_g=q_norm_g[l],
                 k_norm_g=k_norm_g[l], w_branch_a=w_branch_a[l], w_branch_b=w_branch_b[l], w_out=w_out[l],
                 norm3_g=norm3_g[l], ffn2_gate=ffn2_gate[l], ffn2_up=ffn2_up[l], ffn2_down=ffn2_down[l])
        w = _layer_weights(p)
        mod = _ada(c_all, w_ada[l], b_ada[l])
        yp, kp, vp, cp, hp = _prompt_layer(yp, mod[:bp], w)
        ys, ks, vs, cs, hs = _sample_layer(ys, mod[bp:bp + x_sample.shape[0]], state_conv[l], state_rglru[l],
                                           ck, cv, page_table + l * n_pool, w)
        for lst, val in zip(outs, (kp, vp, cp, hp, ks, vs, cs, hs)):
            lst.append(val)
    return (yp, ys) + tuple(jnp.stack(o) for o in outs)
```

```python
import functools

import jax
import jax.numpy as jnp
from jax import lax
from jax.experimental import pallas as pl
from jax.experimental.pallas import tpu as pltpu

F32 = jnp.float32
BF16 = jnp.bfloat16

N_HEADS = 8
N_KV_HEADS = 4
HEAD_DIM = 128
GROUP = N_HEADS // N_KV_HEADS
MOBA_BLOCK = 256
MOBA_TOPK = 3
PAGE_SIZE = 128
N_RNN_BLOCKS = 8
RNN_BLOCK = 128
CONV_WIDTH = 4
RG_C = 8.0
N_SUBLAYERS = 3
EPS = 1e-6
ALIBI_SLOPES = tuple(2.0 ** (-8.0 * (h + 1) / N_HEADS) for h in range(N_HEADS))
ATTN_SCALE = HEAD_DIM ** -0.5
NEG = -1e30
LOG2E = 1.4426950408889634
SUBLANES = 8
VMEM_LIMIT = 56 * 1024 * 1024
PAGES_PER_STEP = 32
KV_BLOCKS_PER_STEP = 4
GATE_COST = 40
MXU_COLS = 256


def _params(n_axes):
    return pltpu.CompilerParams(dimension_semantics=("arbitrary",) * n_axes, vmem_limit_bytes=VMEM_LIMIT)


def _resident(shape):
    nd = len(shape)
    return pl.BlockSpec(shape, lambda *_: (0,) * nd, pipeline_mode=pl.Buffered(1))


def _dot(a, b):
    return jnp.dot(a, b, preferred_element_type=F32)


def _cast_chunks(weights, n_steps):
    return all(w.shape[0] % (n_steps * 2 * SUBLANES) == 0 for w in weights)


def _cast_spec(w, n_steps, step_of):
    rows = w.shape[0] // n_steps
    return pl.BlockSpec((rows, w.shape[1]), lambda *idx: (step_of(*idx), 0))


def _cast_rows(refs):
    n = len(refs) // 2
    for src, dst in zip(refs[:n], refs[n:]):
        dst[...] = src[...].astype(BF16)


def _dot_nt(a, b, precision=None):
    return lax.dot_general(a, b, (((1,), (1,)), ((), ())), precision=precision, preferred_element_type=F32)


def _dot_nt_split(a, b):
    a_hi = a.astype(BF16)
    b_hi = b.astype(BF16)
    a_lo = (a - a_hi.astype(F32)).astype(BF16)
    b_lo = (b - b_hi.astype(F32)).astype(BF16)
    return _dot_nt(a_hi, b_hi) + (_dot_nt(a_hi, b_lo) + _dot_nt(a_lo, b_hi))


def _rms(x, g):
    return x * lax.rsqrt(jnp.mean(x * x, axis=-1, keepdims=True) + EPS) * g


def _sigmoid(x):
    return 0.5 * jnp.tanh(0.5 * x) + 0.5


def _silu(x):
    return x * _sigmoid(x)


ADA_STEPS = 8


def _ada_kernel(c_ref, w_ref, b_ref, *rest, n_cast):
    cast_src, o_ref, cast_dst = rest[:n_cast], rest[n_cast], rest[n_cast + 1:]
    a = _silu(c_ref[...]).astype(BF16)
    o_ref[...] = _dot(a, w_ref[...].astype(BF16)) + b_ref[...]
    _cast_rows(cast_src + cast_dst)


def _ada(c, w, b, casts=()):
    m, d = c.shape
    n = w.shape[1]
    tn = n // ADA_STEPS
    cast_specs = [_cast_spec(x, ADA_STEPS, lambda i: i) for x in casts]
    out = pl.pallas_call(
        functools.partial(_ada_kernel, n_cast=len(casts)),
        grid=(ADA_STEPS,),
        in_specs=[pl.BlockSpec((m, d), lambda i: (0, 0)),
                  pl.BlockSpec((d, tn), lambda i: (0, i)),
                  pl.BlockSpec((1, tn), lambda i: (0, i))] + cast_specs,
        out_specs=[pl.BlockSpec((m, tn), lambda i: (0, i))] + cast_specs,
        out_shape=[jax.ShapeDtypeStruct((m, n), F32)] + [jax.ShapeDtypeStruct(x.shape, BF16) for x in casts],
        compiler_params=_params(1),
        name="ada",
    )(c, w, b.reshape(1, n), *casts)
    return out[0], tuple(out[1:])


def _swiglu_acc(h, wg_ref, wu_ref, wd_ref, chunk):
    d_ff = wg_ref.shape[1]
    acc = None
    for c in range(d_ff // chunk):
        sl = slice(c * chunk, (c + 1) * chunk)
        act = (_silu(_dot(h, wg_ref[:, sl])) * _dot(h, wu_ref[:, sl])).astype(BF16)
        part = _dot(act, wd_ref[sl, :])
        acc = part if acc is None else acc + part
    return acc


def _ffn_kernel(x_ref, sh_ref, sc_ref, g_ref, ng_ref, wg_ref, wu_ref, wd_ref, *rest, chunk, n_cast):
    cast_src, o_ref, cast_dst = rest[:n_cast], rest[n_cast], rest[n_cast + 1:]
    x = x_ref[...]
    h = (_rms(x, ng_ref[...]) * (1.0 + sc_ref[...]) + sh_ref[...]).astype(BF16)
    o_ref[...] = x + 0.5 * g_ref[...] * _swiglu_acc(h, wg_ref, wu_ref, wd_ref, chunk)
    _cast_rows(cast_src + cast_dst)


def _mod_spec(mod, tm, tiles_per_group):
    arr, chunk = mod
    r, d = arr.shape[1], arr.shape[2] // (3 * N_SUBLAYERS)
    return pl.BlockSpec((None, r, d), lambda i: (i // tiles_per_group, 0, chunk))


def _ffn_chunk(d_ff):
    return 256 if d_ff % 256 == 0 else 128


def _ffn(x, sh, sc, g, ng, wg, wu, wd, tm, tiles_per_group, casts=()):
    t, d = x.shape
    d_ff = wg.shape[1]
    n_steps = t // tm
    row = pl.BlockSpec((tm, d), lambda i: (i, 0))
    cast_specs = [_cast_spec(w, n_steps, lambda i: i) for w in casts]
    out = pl.pallas_call(
        functools.partial(_ffn_kernel, chunk=_ffn_chunk(d_ff), n_cast=len(casts)),
        grid=(n_steps,),
        in_specs=[row, _mod_spec(sh, tm, tiles_per_group), _mod_spec(sc, tm, tiles_per_group),
                  _mod_spec(g, tm, tiles_per_group), _resident((1, d)),
                  _resident(wg.shape), _resident(wu.shape), _resident(wd.shape)] + cast_specs,
        out_specs=[row] + cast_specs,
        out_shape=[jax.ShapeDtypeStruct((t, d), F32)] + [jax.ShapeDtypeStruct(w.shape, BF16) for w in casts],
        compiler_params=_params(1),
        name="ffn",
    )(x, sh[0], sc[0], g[0], ng, wg, wu, wd, *casts)
    return out[0], tuple(out[1:])


def _inproj_kernel(x_ref, sh_ref, sc_ref, ng_ref, w_ref, qg_ref, kg_ref,
                   xr_ref, ug_ref, q_ref, k_ref, v_ref, sga_ref, sgb_ref):
    d_rnn = xr_ref.shape[1]
    dq = q_ref.shape[1]
    dk = k_ref.shape[1]
    x = x_ref[...]
    h = (_rms(x, ng_ref[...]) * (1.0 + sc_ref[...]) + sh_ref[...]).astype(BF16)
    o = 0
    xr_ref[...] = _dot(h, w_ref[:, o:o + d_rnn])
    o += d_rnn
    ug_ref[...] = jax.nn.gelu(_dot(h, w_ref[:, o:o + d_rnn]))
    o += d_rnn
    for hd in range(dq // HEAD_DIM):
        sl = slice(hd * HEAD_DIM, (hd + 1) * HEAD_DIM)
        q_ref[:, sl] = _rms(_dot(h, w_ref[:, o + hd * HEAD_DIM:o + (hd + 1) * HEAD_DIM]), qg_ref[...])
    o += dq
    for hd in range(dk // HEAD_DIM):
        sl = slice(hd * HEAD_DIM, (hd + 1) * HEAD_DIM)
        k_ref[:, sl] = _rms(_dot(h, w_ref[:, o + hd * HEAD_DIM:o + (hd + 1) * HEAD_DIM]), kg_ref[...])
    o += dk
    v_ref[...] = _dot(h, w_ref[:, o:o + dk])
    o += dk
    d = sga_ref.shape[1]
    sga_ref[...] = _sigmoid(_dot(h, w_ref[:, o:o + d]))
    o += d
    sgb_ref[...] = _sigmoid(_dot(h, w_ref[:, o:o + d]))


def _inproj(x, sh, sc, ng, w_in, qg, kg, tm, tiles_per_group):
    t, d = x.shape
    dq, dk = N_HEADS * HEAD_DIM, N_KV_HEADS * HEAD_DIM
    d_rnn = (w_in.shape[1] - dq - 2 * dk - 2 * d) // 2

    def row(w):
        return pl.BlockSpec((tm, w), lambda i: (i, 0))

    widths = (d_rnn, d_rnn, dq, dk, dk, d, d)
    return pl.pallas_call(
        _inproj_kernel,
        grid=(t // tm,),
        in_specs=[row(d), _mod_spec(sh, tm, tiles_per_group), _mod_spec(sc, tm, tiles_per_group),
                  _resident((1, d)), _resident(w_in.shape), _resident((1, HEAD_DIM)), _resident((1, HEAD_DIM))],
        out_specs=[row(w) for w in widths],
        out_shape=[jax.ShapeDtypeStruct((t, w), F32) for w in widths],
        compiler_params=_params(1),
        name="inproj",
    )(x, sh[0], sc[0], ng, w_in, qg, kg)


def _rg_gate_block(xcn, n, wg_ref, bra_ref, brx_ref, sp, a_ref, b_ref):
    sl = slice(n * RNN_BLOCK, (n + 1) * RNN_BLOCK)
    z = _dot(xcn.astype(BF16), wg_ref[n])
    r = _sigmoid(z[:, :RNN_BLOCK] + bra_ref[:, sl])
    i = _sigmoid(z[:, RNN_BLOCK:] + brx_ref[:, sl])
    log_a = -RG_C * r * sp[:, sl]
    a = jnp.exp(log_a)
    a_ref[:, sl] = a
    b_ref[:, sl] = jnp.sqrt(1.0 - a * a) * (i * xcn)


def _rg_gates(xc, wg_ref, bra_ref, brx_ref, lam_ref, a_ref, b_ref):
    sp = jax.nn.softplus(-lam_ref[...])
    for n in range(N_RNN_BLOCKS):
        _rg_gate_block(xc[:, n * RNN_BLOCK:(n + 1) * RNN_BLOCK], n, wg_ref, bra_ref, brx_ref, sp, a_ref, b_ref)


def _scan_rows(a, b, row_in_seg, steps):
    for d in steps:
        keep = row_in_seg >= d
        a_prev = jnp.where(keep, pltpu.roll(a, d, 0), 1.0)
        b_prev = jnp.where(keep, pltpu.roll(b, d, 0), 0.0)
        b = a * b_prev + b
        a = a * a_prev
    return a, b


def _mixin_prompt_kernel(x_ref, sh_ref, sc_ref, ng_ref, w_ref, qg_ref, kg_ref, conv0_ref, h0_ref, cw_ref, cb_ref,
                         wg_ref, bra_ref, brx_ref, lam_ref, wba_ref,
                         q_ref, k_ref, v_ref, sgb_ref, ya_ref, k4_ref, v4_ref, nconv_ref, hlast_ref,
                         xbuf, a_s, b_s, hc_s, ug_s, sga_s, *, ts):
    s = pl.program_id(1)
    w = xbuf.shape[1]
    dq = q_ref.shape[1]
    dk = k_ref.shape[1]
    d = sgb_ref.shape[1]
    tail = CONV_WIDTH - 1
    lo = SUBLANES - tail

    @pl.when(s == 0)
    def _():
        xbuf[lo:SUBLANES, :] = conv0_ref[...]
        hc_s[...] = jnp.broadcast_to(h0_ref[...], hc_s.shape)

    x = x_ref[...]
    h = (_rms(x, ng_ref[...]) * (1.0 + sc_ref[...]) + sh_ref[...]).astype(BF16)
    o_gate, o_q = w, 2 * w
    o_k = o_q + dq
    o_v = o_k + dk
    o_ga = o_v + dk
    o_gb = o_ga + d
    cw_ = MXU_COLS

    def proj(lo):
        return _dot(h, w_ref[:, lo:lo + cw_])

    def normed_heads(ref, gain_ref, base, c):
        z = proj(base + c * cw_)
        for i in range(cw_ // HEAD_DIM):
            col = c * cw_ + i * HEAD_DIM
            ref[:, col:col + HEAD_DIM] = _rms(z[:, i * HEAD_DIM:(i + 1) * HEAD_DIM], gain_ref[...])

    def store(ref, base, c, fn):
        ref[:, c * cw_:(c + 1) * cw_] = fn(proj(base + c * cw_))

    jobs = [functools.partial(normed_heads, q_ref, qg_ref, o_q, c) for c in range(dq // cw_)]
    jobs += [functools.partial(normed_heads, k_ref, kg_ref, o_k, c) for c in range(dk // cw_)]
    jobs += [functools.partial(store, v_ref, o_v, c, lambda z: z) for c in range(dk // cw_)]
    jobs += [functools.partial(store, sgb_ref, o_gb, c, _sigmoid) for c in range(d // cw_)]
    jobs += [functools.partial(store, ug_s, o_gate, c, jax.nn.gelu) for c in range(w // cw_)]
    jobs += [functools.partial(store, sga_s, o_ga, c, _sigmoid) for c in range(d // cw_)]

    def by_head(src_ref, dst_ref, hd):
        dst_ref[:, hd, :] = src_ref[:, hd * HEAD_DIM:(hd + 1) * HEAD_DIM]
    cost = [GATE_COST] * N_RNN_BLOCKS + [1] * (ts // SUBLANES)
    due, acc_cost = [], 0
    for c_ in cost:
        acc_cost += c_
        due.append((acc_cost * len(jobs)) // sum(cost))
    done = 0

    for c in range(w // cw_):
        xbuf[SUBLANES:SUBLANES + ts, c * cw_:(c + 1) * cw_] = proj(c * cw_)
    sp = jax.nn.softplus(-lam_ref[...])
    for n in range(N_RNN_BLOCKS):
        sl = slice(n * RNN_BLOCK, (n + 1) * RNN_BLOCK)
        xcn = cb_ref[:, sl] + cw_ref[0:1, sl] * xbuf[lo:lo + ts, sl]
        for k in range(1, CONV_WIDTH):
            xcn = xcn + cw_ref[k:k + 1, sl] * xbuf[lo + k:lo + k + ts, sl]
        _rg_gate_block(xcn, n, wg_ref, bra_ref, brx_ref, sp, a_s, b_s)
        while done < due[n]:
            jobs[done]()
            done += 1
    new_tail = xbuf[ts + lo:ts + SUBLANES, :]
    nconv_ref[...] = new_tail
    xbuf[lo:SUBLANES, :] = new_tail

    row = lax.broadcasted_iota(jnp.int32, (SUBLANES, w), 0)
    hc = hc_s[...]
    for c in range(ts // SUBLANES):
        rows = slice(c * SUBLANES, (c + 1) * SUBLANES)
        a_cum, b_cum = _scan_rows(a_s[rows, :], b_s[rows, :], row, (1, 2, 4))
        hs = a_cum * hc + b_cum
        a_s[rows, :] = hs
        hc = jnp.broadcast_to(hs[SUBLANES - 1:SUBLANES, :], (SUBLANES, w))
        while done < due[N_RNN_BLOCKS + c]:
            jobs[done]()
            done += 1
    hc_s[...] = hc
    hlast_ref[...] = hc[0:1, :]
    u = (a_s[...] * ug_s[...]).astype(BF16)
    ya = _dot(u, wba_ref[...])
    for hd in range(dk // HEAD_DIM):
        by_head(k_ref, k4_ref, hd)
        by_head(v_ref, v4_ref, hd)
    ya_ref[...] = sga_s[...] * ya


def _mixin_prompt(x, sh, sc, ng, w_in, qg, kg, conv0, h0, cw, cb, wg, bra, brx, lam, wba, ts):
    b, s, d = x.shape
    w = wba.shape[0]
    dq, dk = N_HEADS * HEAD_DIM, N_KV_HEADS * HEAD_DIM
    tail = CONV_WIDTH - 1

    def seq(width):
        return pl.BlockSpec((None, ts, width), lambda i, j: (i, j, 0))

    def per_seq(rows, width):
        return pl.BlockSpec((None, rows, width), lambda i, j: (i, 0, 0))

    def mod_row(mod):
        return pl.BlockSpec((None, 1, d), lambda i, j: (i, 0, mod[1]))

    widths = (dq, dk, dk, d, d)
    by_head = pl.BlockSpec((None, ts, N_KV_HEADS, HEAD_DIM), lambda i, j: (i, j, 0, 0))
    return pl.pallas_call(
        functools.partial(_mixin_prompt_kernel, ts=ts),
        grid=(b, s // ts),
        in_specs=[seq(d), mod_row(sh), mod_row(sc), _resident((1, d)), _resident(w_in.shape),
                  _resident((1, HEAD_DIM)), _resident((1, HEAD_DIM)), per_seq(tail, w), per_seq(1, w),
                  _resident(cw.shape), _resident((1, w)), _resident(wg.shape), _resident((1, w)), _resident((1, w)),
                  _resident((1, w)), _resident(wba.shape)],
        out_specs=[seq(wd) for wd in widths] + [by_head, by_head, per_seq(tail, w), per_seq(1, w)],
        out_shape=[jax.ShapeDtypeStruct((b, s, wd), F32) for wd in widths]
        + [jax.ShapeDtypeStruct((b, s, N_KV_HEADS, HEAD_DIM), F32)] * 2
        + [jax.ShapeDtypeStruct((b, tail, w), F32), jax.ShapeDtypeStruct((b, 1, w), F32)],
        scratch_shapes=[pltpu.VMEM((ts + SUBLANES, w), F32), pltpu.VMEM((ts, w), F32), pltpu.VMEM((ts, w), F32),
                        pltpu.VMEM((SUBLANES, w), F32), pltpu.VMEM((ts, w), F32), pltpu.VMEM((ts, d), F32)],
        compiler_params=_params(2),
        name="mixin_prompt",
    )(x, sh[0], sc[0], ng, w_in, qg, kg, conv0, h0, cw, cb, wg, bra, brx, lam, wba)


def _rnn_sample_kernel(xs_ref, ug_ref, sga_ref, h0_ref, cw_ref, cb_ref, wg_ref, bra_ref, brx_ref, lam_ref, wba_ref,
                       ya_ref, h_ref, a_s, b_s, *, seg):
    m, w = ug_ref.shape
    xc = cb_ref[...] + cw_ref[0:1, :] * xs_ref[0]
    for k in range(1, CONV_WIDTH):
        xc = xc + cw_ref[k:k + 1, :] * xs_ref[k]
    _rg_gates(xc, wg_ref, bra_ref, brx_ref, lam_ref, a_s, b_s)
    row_in_seg = lax.broadcasted_iota(jnp.int32, (SUBLANES, w), 0) % seg
    steps = tuple(d for d in (1, 2, 4) if d < seg)
    for c in range(m // SUBLANES):
        rows = slice(c * SUBLANES, (c + 1) * SUBLANES)
        a_cum, b_cum = _scan_rows(a_s[rows, :], b_s[rows, :], row_in_seg, steps)
        h_ref[rows, :] = a_cum * h0_ref[rows, :] + b_cum
    u = (h_ref[...] * ug_ref[...]).astype(BF16)
    ya_ref[...] = sga_ref[...] * _dot(u, wba_ref[...])


def _rnn_sample(xs, ug, sga, h0_rows, cw, cb, wg, bra, brx, lam, wba, seg):
    m, w = ug.shape
    d = wba.shape[1]
    return pl.pallas_call(
        functools.partial(_rnn_sample_kernel, seg=seg),
        out_shape=[jax.ShapeDtypeStruct((m, d), F32), jax.ShapeDtypeStruct((m, w), F32)],
        scratch_shapes=[pltpu.VMEM((m, w), F32), pltpu.VMEM((m, w), F32)],
        compiler_params=pltpu.CompilerParams(vmem_limit_bytes=VMEM_LIMIT),
        name="rnn_sample",
    )(xs, ug, sga, h0_rows, cw, cb, wg, bra, brx, lam, wba)


V_ROWS = HEAD_DIM + 16


def _attn_prompt_kernel(slope_ref, q_ref, k_ref, v_ref, *rest, nb, gt, n_cast):
    cast_src, o_ref, cast_dst = rest[:n_cast], rest[n_cast], rest[n_cast + 1:2 * n_cast + 1]
    kb_s, vt_s, kbg_s, vtg_s, km_s, q2_s, bias_s, colb_s, acc_s = rest[2 * n_cast + 1:]
    g = pl.program_id(1)
    blk = MOBA_BLOCK
    s_len = nb * blk
    q_scale = ATTN_SCALE * LOG2E
    _cast_rows(cast_src + cast_dst)

    ones_rows = (lax.broadcasted_iota(jnp.int32, (V_ROWS - HEAD_DIM, blk), 0) == 0).astype(BF16)
    for n in range(nb):
        rows = slice(n * blk, (n + 1) * blk)
        grp, sub = n // gt, slice((n % gt) * blk, (n % gt + 1) * blk)
        kn = k_ref[rows, :]
        kb = kn.astype(BF16)
        vt = v_ref[rows, :].T.astype(BF16)
        kb_s[n] = kb
        kbg_s[grp, sub, :] = kb
        vt_s[n, 0:HEAD_DIM, :] = vt
        vt_s[n, HEAD_DIM:V_ROWS, :] = ones_rows
        vtg_s[grp, 0:HEAD_DIM, sub] = vt
        vtg_s[grp, HEAD_DIM:V_ROWS, sub] = ones_rows
        km_s[n:n + 1, :] = jnp.sum(kn, axis=0, keepdims=True) * (1.0 / blk)
        q2_s[n] = (q_ref[rows, :] * q_scale).astype(BF16)

    blk_id = lax.broadcasted_iota(jnp.int32, (nb, s_len), 0)
    q_blk = lax.broadcasted_iota(jnp.int32, (nb, s_len), 1) // blk
    fully_past = blk_id < q_blk
    blocks_ahead = ((blk_id - q_blk) * blk).astype(F32)
    key_off = lax.broadcasted_iota(jnp.int32, (blk, blk), 0)
    causal = key_off <= lax.broadcasted_iota(jnp.int32, (blk, blk), 1)
    km = km_s[...]
    for hh in range(GROUP):
        slope2 = slope_ref[g * GROUP + hh] * LOG2E
        cols = slice(hh * HEAD_DIM, (hh + 1) * HEAD_DIM)
        gs = jnp.where(fully_past, _dot_nt_split(km, q_ref[:, cols]), -jnp.inf)
        rank = jnp.zeros((nb, s_len), jnp.int32)
        for m in range(nb):
            gm = gs[m:m + 1, :]
            tie = (blk_id > m).astype(jnp.int32)
            rank = rank + jnp.where(gm > gs, 1, jnp.where(gm == gs, tie, 0))
        chosen = jnp.logical_and(fully_past, rank < MOBA_TOPK)
        bias = jnp.where(chosen, slope2 * blocks_ahead, NEG)
        for jq in range(nb):
            bias_s[hh, jq] = bias[:, jq * blk:(jq + 1) * blk]
        colb_s[hh] = slope2 * key_off.astype(F32)

    heads = [slice(hh * HEAD_DIM, (hh + 1) * HEAD_DIM) for hh in range(GROUP)]

    def group_scores(gi, j, q2):
        kg = kbg_s[gi]
        s_grp = [_dot_nt(kg, q2[:, cols]) for cols in heads]
        parts, tops = [], []
        for hh in range(GROUP):
            ps = [s_grp[hh][t * blk:(t + 1) * blk, :] + colb_s[hh] + bias_s[hh, j, pl.ds(gi * gt + t, 1), :]
                  for t in range(gt)]
            top = jnp.max(ps[0], axis=0, keepdims=True)
            for part in ps[1:]:
                top = jnp.maximum(top, jnp.max(part, axis=0, keepdims=True))
            parts.append(ps)
            tops.append(top)
        return parts, tops

    def write_out(jb):
        rows = pl.ds(pl.multiple_of(jb * blk, blk), blk)
        for hh in range(GROUP):
            acc = acc_s[hh]
            o = acc[0:HEAD_DIM, :] / acc[HEAD_DIM:HEAD_DIM + 1, :]
            o_ref[rows, heads[hh]] = o.T.astype(o_ref.dtype)

    acc_s[...] = jnp.ones_like(acc_s)

    def q_block(j, _):
        q2 = q2_s[j]
        kd = kb_s[j]
        vd = vt_s[j]
        s_own = [_dot_nt(kd, q2[:, cols]) for cols in heads]
        parts0, tops0 = group_scores(0, j, q2)
        write_out(jnp.maximum(j - 1, 0))
        m_run, probs = [], []
        for hh in range(GROUP):
            s = jnp.where(causal, s_own[hh] + colb_s[hh], NEG)
            m1 = jnp.maximum(jnp.max(s, axis=0, keepdims=True), tops0[hh])
            p_own = jnp.exp2(s - m1).astype(BF16)
            p_grp = jnp.concatenate([jnp.exp2(part - m1).astype(BF16) for part in parts0[hh]], axis=0)
            probs.append((p_own, p_grp))
            m_run.append(m1)
        vg0 = vtg_s[0]
        for hh in range(GROUP):
            acc_s[hh] = _dot(vd, probs[hh][0]) + _dot(vg0, probs[hh][1])

        def kv_group(gi, m_run):
            vg = vtg_s[gi]
            parts, tops = group_scores(gi, j, q2)
            out, probs = [], []
            for hh in range(GROUP):
                m_new = jnp.maximum(m_run[hh], tops[hh])
                probs.append(jnp.concatenate([jnp.exp2(part - m_new).astype(BF16) for part in parts[hh]], axis=0))
                out.append(m_new)
            pv = [_dot(vg, p) for p in probs]
            for hh in range(GROUP):
                acc_s[hh] = jnp.exp2(m_run[hh] - out[hh]) * acc_s[hh] + pv[hh]
            return tuple(out)

        lax.fori_loop(1, (j + gt - 1) // gt, kv_group, tuple(m_run))
        return 0

    lax.fori_loop(0, nb, q_block, 0)
    write_out(nb - 1)


def _attn_prompt(q, k, v, casts=()):
    b, s, _ = q.shape
    assert s % MOBA_BLOCK == 0
    nb = s // MOBA_BLOCK
    gt = KV_BLOCKS_PER_STEP if nb % KV_BLOCKS_PER_STEP == 0 else 1
    gw = GROUP * HEAD_DIM
    slopes = jnp.asarray(ALIBI_SLOPES, F32)
    kv_spec = pl.BlockSpec((None, s, HEAD_DIM), lambda i, g: (i, 0, g))
    q_spec = pl.BlockSpec((None, s, gw), lambda i, g: (i, 0, g))
    cast_specs = [_cast_spec(w, b * N_KV_HEADS, lambda i, g: i * N_KV_HEADS + g) for w in casts]
    out = pl.pallas_call(
        functools.partial(_attn_prompt_kernel, nb=nb, gt=gt, n_cast=len(casts)),
        grid=(b, N_KV_HEADS),
        in_specs=[pl.BlockSpec(memory_space=pltpu.SMEM), q_spec, kv_spec, kv_spec] + cast_specs,
        out_specs=[q_spec] + cast_specs,
        out_shape=[jax.ShapeDtypeStruct(q.shape, BF16)] + [jax.ShapeDtypeStruct(w.shape, BF16) for w in casts],
        scratch_shapes=[pltpu.VMEM((nb, MOBA_BLOCK, HEAD_DIM), BF16), pltpu.VMEM((nb, V_ROWS, MOBA_BLOCK), BF16),
                        pltpu.VMEM((nb // gt, gt * MOBA_BLOCK, HEAD_DIM), BF16),
                        pltpu.VMEM((nb // gt, V_ROWS, gt * MOBA_BLOCK), BF16),
                        pltpu.VMEM((nb, HEAD_DIM), F32), pltpu.VMEM((nb, MOBA_BLOCK, gw), BF16),
                        pltpu.VMEM((GROUP, nb, nb, MOBA_BLOCK), F32), pltpu.VMEM((GROUP, MOBA_BLOCK, MOBA_BLOCK), F32),
                        pltpu.VMEM((GROUP, V_ROWS, MOBA_BLOCK), F32)],
        compiler_params=_params(2),
        name="attn_prompt",
    )(slopes, q, k, v, *casts)
    return out[0], tuple(out[1:])


def _row_slopes(head):
    out = jnp.zeros(head.shape, F32)
    for h, sl in enumerate(ALIBI_SLOPES):
        out = jnp.where(head == h, sl, out)
    return out


def _attn_sample_kernel(pt_ref, q_ref, kn_ref, vn_ref, *rest, pp, n_blk, past, n_new):
    k_pages, v_pages = rest[:pp], rest[pp:2 * pp]
    o_ref = rest[2 * pp]
    km_s, m_s, l_s, op_s, bias_s = rest[2 * pp + 1:]
    del pt_ref
    s_id = pl.program_id(1)
    nq = q_ref.shape[0]
    kv = N_KV_HEADS
    cols = MOBA_BLOCK * kv
    q = q_ref[...]
    qb = (q * (ATTN_SCALE * LOG2E)).astype(BF16)

    row = lax.broadcasted_iota(jnp.int32, (nq, cols), 0)
    col = lax.broadcasted_iota(jnp.int32, (nq, cols), 1)
    head = row % N_HEADS
    step = row // N_HEADS

    @pl.when(s_id == 0)
    def _():
        same_kv = (col % kv) == head // GROUP
        back = (step + MOBA_BLOCK - col // kv).astype(F32)
        bias_s[...] = jnp.where(same_kv, -(_row_slopes(head) * LOG2E) * back, NEG)
        m_s[...] = jnp.zeros_like(m_s)
        l_s[...] = jnp.zeros_like(l_s)

    slope_col = _row_slopes(head[:, 0:1]) * LOG2E
    lane = lax.broadcasted_iota(jnp.int32, (nq, 128), 1)
    per_step = pp // 2
    first_blk = s_id * per_step
    ksums, scores, probs, partials = [], [], [], []
    for i in range(per_step):
        k0, k1 = k_pages[2 * i][...], k_pages[2 * i + 1][...]
        ksums.append(jnp.sum(k0.reshape(-1, SUBLANES, HEAD_DIM), axis=0)
                     + jnp.sum(k1.reshape(-1, SUBLANES, HEAD_DIM), axis=0))
        kb = jnp.concatenate([k0, k1], axis=0).astype(BF16)
        scores.append(_dot_nt(qb, kb) + bias_s[...])
    for s in scores:
        m = jnp.max(s, axis=-1, keepdims=True)
        p = jnp.exp2(s - m)
        probs.append((m, jnp.sum(p, axis=-1, keepdims=True), p.astype(BF16)))
    for i, (m, l, p) in enumerate(probs):
        vb = jnp.concatenate([v_pages[2 * i][...], v_pages[2 * i + 1][...]], axis=0).astype(BF16)
        partials.append((m, l, _dot(p, vb)))
    m_new, l_new = m_s[...], l_s[...]
    for i, (m, l, o_part) in enumerate(partials):
        n = first_blk + i
        km_s[pl.ds(pl.multiple_of(n * SUBLANES, SUBLANES), SUBLANES), :] = ksums[i] * (1.0 / MOBA_BLOCK)
        op_s[n] = o_part
        m_new = jnp.where(lane == n, m + slope_col * ((n + 1) * MOBA_BLOCK - past).astype(F32), m_new)
        l_new = jnp.where(lane == n, l, l_new)
    m_s[...] = m_new
    l_s[...] = l_new

    @pl.when(s_id == pl.num_programs(1) - 1)
    def _():
        gcols = n_blk * SUBLANES
        gfull = _dot_nt(q, km_s[...], precision=lax.Precision.HIGHEST)
        gsum = gfull + pltpu.roll(gfull, gcols - kv, 1)
        grow = lax.broadcasted_iota(jnp.int32, (nq, gcols), 0)
        gcol = lax.broadcasted_iota(jnp.int32, (nq, gcols), 1)
        gs = jnp.where((gcol % SUBLANES) == (grow % N_HEADS) // GROUP, gsum, -jnp.inf)
        gcol_f = gcol.astype(F32)
        chosen = jnp.zeros((nq, 128), F32)
        for _ in range(MOBA_TOPK):
            best = jnp.max(gs, axis=-1, keepdims=True)
            first = jnp.min(jnp.where(gs == best, gcol_f, float(gcols)), axis=-1, keepdims=True)
            chosen = jnp.where(lane == (first.astype(jnp.int32) // SUBLANES), 1.0, chosen)
            gs = jnp.where(gcol_f == first, -jnp.inf, gs)
        is_chosen = chosen > 0.0

        nk = n_new * kv
        orow = lax.broadcasted_iota(jnp.int32, (nq, nk), 0)
        ocol = lax.broadcasted_iota(jnp.int32, (nq, nk), 1)
        ohead, ostep = orow % N_HEADS, orow // N_HEADS
        ok = jnp.logical_and((ocol % kv) == ohead // GROUP, ocol // kv <= ostep)
        so = _dot_nt(qb, kn_ref[...].astype(BF16)) - (_row_slopes(ohead) * LOG2E) * (ostep - ocol // kv).astype(F32)
        so = jnp.where(ok, so, NEG)

        m_all = jnp.maximum(jnp.max(so, axis=-1, keepdims=True),
                            jnp.max(jnp.where(is_chosen, m_s[...], NEG), axis=-1, keepdims=True))
        wgt = jnp.where(is_chosen, jnp.exp2(m_s[...] - m_all), 0.0)
        po = jnp.exp2(so - m_all)
        l_all = jnp.sum(wgt * l_s[...], axis=-1, keepdims=True) + jnp.sum(po, axis=-1, keepdims=True)
        acc = _dot(po.astype(BF16), vn_ref[...].astype(BF16))
        for n in range(n_blk):
            acc = acc + wgt[:, n:n + 1] * op_s[n]
        o_ref[...] = acc / l_all


def _attn_sample(q, k_new, v_new, cache_k, cache_v, page_ids, n_new):
    b, nq, _ = q.shape
    n_pages = page_ids.shape[1]
    past = n_pages * PAGE_SIZE
    pp = PAGES_PER_STEP
    assert MOBA_BLOCK == 2 * PAGE_SIZE and past % MOBA_BLOCK == 0 and n_pages % pp == 0
    n_blk = past // MOBA_BLOCK
    assert MOBA_TOPK <= n_blk <= 128 and n_new <= MOBA_BLOCK
    rows = PAGE_SIZE * N_KV_HEADS

    def page(i):
        return pl.BlockSpec((None, rows, HEAD_DIM), lambda bi, si, pt: (pt[bi * n_pages + si * pp + i], 0, 0))

    def per_seq(r):
        return pl.BlockSpec((None, r, HEAD_DIM), lambda bi, si, pt: (bi, 0, 0))

    return pl.pallas_call(
        functools.partial(_attn_sample_kernel, pp=pp, n_blk=n_blk, past=past, n_new=n_new),
        grid_spec=pltpu.PrefetchScalarGridSpec(
            num_scalar_prefetch=1,
            grid=(b, n_pages // pp),
            in_specs=[per_seq(nq), per_seq(n_new * N_KV_HEADS), per_seq(n_new * N_KV_HEADS)]
            + [page(i) for i in range(pp)] * 2,
            out_specs=per_seq(nq),
            scratch_shapes=[pltpu.VMEM((n_blk * SUBLANES, HEAD_DIM), F32), pltpu.VMEM((nq, 128), F32),
                            pltpu.VMEM((nq, 128), F32), pltpu.VMEM((n_blk, nq, HEAD_DIM), F32),
                            pltpu.VMEM((nq, MOBA_BLOCK * N_KV_HEADS), F32)]),
        out_shape=jax.ShapeDtypeStruct((b, nq, HEAD_DIM), F32),
        compiler_params=_params(2),
        name="attn_sample",
    )(page_ids.reshape(-1), q, k_new, v_new, *([cache_k] * pp), *([cache_v] * pp))


def _merge_ffn_kernel(x_ref, ya_ref, o_ref_in, sgb_ref, g2_ref, sh_ref, sc_ref, g3_ref, ng_ref,
                      wbb_ref, wout_ref, wg_ref, wu_ref, wd_ref, out_ref, *, chunk):
    yb = _dot(o_ref_in[...].astype(BF16), wbb_ref[...])
    merged = (ya_ref[...] + sgb_ref[...] * yb).astype(BF16)
    x = x_ref[...] + g2_ref[...] * _dot(merged, wout_ref[...])
    h = (_rms(x, ng_ref[...]) * (1.0 + sc_ref[...]) + sh_ref[...]).astype(BF16)
    out_ref[...] = x + 0.5 * g3_ref[...] * _swiglu_acc(h, wg_ref, wu_ref, wd_ref, chunk)


def _merge_ffn(x, ya, o, sgb, g2, sh, sc, g3, ng, wbb, wout, wg, wu, wd, tm, tiles_per_group):
    t, d = x.shape
    row = pl.BlockSpec((tm, d), lambda i: (i, 0))
    mod = functools.partial(_mod_spec, tm=tm, tiles_per_group=tiles_per_group)
    return pl.pallas_call(
        functools.partial(_merge_ffn_kernel, chunk=_ffn_chunk(wg.shape[1])),
        grid=(t // tm,),
        in_specs=[row, row, pl.BlockSpec((tm, o.shape[1]), lambda i: (i, 0)), row,
                  mod(g2), mod(sh), mod(sc), mod(g3), _resident((1, d)),
                  _resident(wbb.shape), _resident(wout.shape), _resident(wg.shape), _resident(wu.shape),
                  _resident(wd.shape)],
        out_specs=row,
        out_shape=jax.ShapeDtypeStruct((t, d), F32),
        compiler_params=_params(1),
        name="merge_ffn",
    )(x, ya, o, sgb, g2[0], sh[0], sc[0], g3[0], ng, wbb, wout, wg, wu, wd)


def _token_tile(t):
    return 512 if t % 512 == 0 else t


EARLY_BF16 = ("ffn1_gate", "ffn1_up", "ffn1_down")
MIXIN_BF16 = ("w_in", "w_branch_a")
MERGE_BF16 = ("w_branch_b", "w_out", "ffn2_gate", "ffn2_up", "ffn2_down")


def _layer_weights(p):
    w = dict(p)
    w["rg_w"] = jnp.concatenate([p["rg_wa"], p["rg_wx"]], axis=-1).astype(BF16)
    for name in ("norm1_g", "norm2_g", "norm3_g", "conv_b", "rg_ba", "rg_bx", "rg_lambda", "q_norm_g", "k_norm_g"):
        w[name] = p[name].reshape(1, -1)
    return w


def _hidden_casts(w, names, n_steps):
    if _cast_chunks([w[n] for n in names], n_steps):
        return tuple(w[n] for n in names)
    for n in names:
        w[n] = w[n].astype(BF16)
    return ()


def _prompt_layer(x, mod, w):
    b, s, d = x.shape
    t = b * s
    tm = _token_tile(s)
    tpg = s // tm
    sh1, sc1, g1, sh2, sc2, g2, sh3, sc3, g3 = [(mod[:, None, :], c) for c in range(3 * N_SUBLAYERS)]
    casts = _hidden_casts(w, MIXIN_BF16, t // tm)
    x1, done = _ffn(x.reshape(t, d), sh1, sc1, g1, w["norm1_g"], w["ffn1_gate"], w["ffn1_up"], w["ffn1_down"],
                    tm, tpg, casts)
    w.update(zip(MIXIN_BF16, done))
    d_rnn = w["w_branch_a"].shape[0]
    q, k, v, sgb, ya, k_heads, v_heads, new_conv, h_last = _mixin_prompt(
        x1.reshape(b, s, d), sh2, sc2, w["norm2_g"], w["w_in"], w["q_norm_g"], w["k_norm_g"],
        jnp.zeros((b, CONV_WIDTH - 1, d_rnn), F32), jnp.zeros((b, 1, d_rnn), F32),
        w["conv_w"], w["conv_b"], w["rg_w"], w["rg_ba"], w["rg_bx"], w["rg_lambda"], w["w_branch_a"], tm)
    sgb = sgb.reshape(t, d)
    casts = _hidden_casts(w, MERGE_BF16, b * N_KV_HEADS)
    o, done = _attn_prompt(q.reshape(b, s, -1), k.reshape(b, s, -1), v.reshape(b, s, -1), casts)
    w.update(zip(MERGE_BF16, done))
    y = _merge_ffn(x1, ya.reshape(t, d), o.reshape(t, -1), sgb, g2, sh3, sc3, g3, w["norm3_g"],
                   w["w_branch_b"], w["w_out"], w["ffn2_gate"], w["ffn2_up"], w["ffn2_down"], tm, tpg)
    return y.reshape(b, s, d), k_heads, v_heads, new_conv, h_last.reshape(b, d_rnn)


def _sample_layer(x, mod, conv_buf, h0, cache_k, cache_v, page_ids, w):
    b, s, d = x.shape
    t = b * s
    per_token = jnp.repeat(mod, s, axis=0)[None]
    sh1, sc1, g1, sh2, sc2, g2, sh3, sc3, g3 = [(per_token, c) for c in range(3 * N_SUBLAYERS)]
    x1, _ = _ffn(x.reshape(t, d), sh1, sc1, g1, w["norm1_g"], w["ffn1_gate"], w["ffn1_up"], w["ffn1_down"], t, 1)
    xr, ug, q, k, v, sga, sgb = _inproj(x1, sh2, sc2, w["norm2_g"], w["w_in"], w["q_norm_g"], w["k_norm_g"], t, 1)
    d_rnn = xr.shape[1]
    xp = jnp.concatenate([conv_buf, xr.reshape(b, s, d_rnn)], axis=1)
    taps = jnp.stack([xp[:, kk:kk + s].reshape(t, d_rnn) for kk in range(CONV_WIDTH)])
    ya, h_all = _rnn_sample(taps, ug, sga, jnp.repeat(h0, s, axis=0), w["conv_w"], w["conv_b"], w["rg_w"],
                            w["rg_ba"], w["rg_bx"], w["rg_lambda"], w["w_branch_a"], s)
    o = _attn_sample(q.reshape(b, s * N_HEADS, HEAD_DIM), k.reshape(b, s * N_KV_HEADS, HEAD_DIM),
                     v.reshape(b, s * N_KV_HEADS, HEAD_DIM), cache_k, cache_v, page_ids, s)
    y = _merge_ffn(x1, ya, o.reshape(t, -1), sgb, g2, sh3, sc3, g3, w["norm3_g"],
                   w["w_branch_b"], w["w_out"], w["ffn2_gate"], w["ffn2_up"], w["ffn2_down"], t, 1)
    return (y.reshape(b, s, d), k.reshape(b, s, N_KV_HEADS, HEAD_DIM), v.reshape(b, s, N_KV_HEADS, HEAD_DIM),
            xp[:, -(CONV_WIDTH - 1):], h_all.reshape(b, s, d_rnn)[:, -1])


def kernel(x_prompt, x_sample, c_prompt, c_sample, cache_k, cache_v, state_conv, state_rglru, page_table, w_ada, b_ada, norm1_g, ffn1_gate, ffn1_up, ffn1_down, norm2_g, w_in, conv_w, conv_b, rg_wa, rg_ba, rg_wx, rg_bx, rg_lambda, q_norm_g, k_norm_g, w_branch_a, w_branch_b, w_out, norm3_g, ffn2_gate, ffn2_up, ffn2_down):
    depth, n_pool = cache_k.shape[0], cache_k.shape[1]
    bp = x_prompt.shape[0]
    assert cache_k.shape[2:] == (PAGE_SIZE, N_KV_HEADS, HEAD_DIM)
    ck = cache_k.reshape(depth * n_pool, PAGE_SIZE * N_KV_HEADS, HEAD_DIM)
    cv = cache_v.reshape(depth * n_pool, PAGE_SIZE * N_KV_HEADS, HEAD_DIM)
    c_all = jnp.concatenate([c_prompt, c_sample], axis=0)
    pad = -c_all.shape[0] % SUBLANES
    c_all = jnp.pad(c_all, ((0, pad), (0, 0)))
    yp, ys = x_prompt, x_sample
    outs = [[] for _ in range(8)]
    for l in range(depth):
        p = dict(norm1_g=norm1_g[l], ffn1_gate=ffn1_gate[l], ffn1_up=ffn1_up[l], ffn1_down=ffn1_down[l],
                 norm2_g=norm2_g[l], w_in=w_in[l], conv_w=conv_w[l], conv_b=conv_b[l], rg_wa=rg_wa[l],
                 rg_ba=rg_ba[l], rg_wx=rg_wx[l], rg_bx=rg_bx[l], rg_lambda=rg_lambda[l], q_norm_g=q_norm_g[l],
                 k_norm_g=k_norm_g[l], w_branch_a=w_branch_a[l], w_branch_b=w_branch_b[l], w_out=w_out[l],
                 norm3_g=norm3_g[l], ffn2_gate=ffn2_gate[l], ffn2_up=ffn2_up[l], ffn2_down=ffn2_down[l])
        w = _layer_weights(p)
        mod, done = _ada(c_all, w_ada[l], b_ada[l], _hidden_casts(w, EARLY_BF16, ADA_STEPS))
        w.update(zip(EARLY_BF16, done))
        yp, kp, vp, cp, hp = _prompt_layer(yp, mod[:bp], w)
        ys, ks, vs, cs, hs = _sample_layer(ys, mod[bp:bp + x_sample.shape[0]], state_conv[l], state_rglru[l],
                                           ck, cv, page_table + l * n_pool, w)
        for lst, val in zip(outs, (kp, vp, cp, hp, ks, vs, cs, hs)):
            lst.append(val)
    return (yp, ys) + tuple(jnp.stack(o) for o in outs)
```

```python
import functools

import jax
import jax.numpy as jnp
from jax import lax
from jax.experimental import pallas as pl
from jax.experimental.pallas import tpu as pltpu

F32 = jnp.float32
BF16 = jnp.bfloat16

N_HEADS = 8
N_KV_HEADS = 4
HEAD_DIM = 128
GROUP = N_HEADS // N_KV_HEADS
MOBA_BLOCK = 256
MOBA_TOPK = 3
PAGE_SIZE = 128
N_RNN_BLOCKS = 8
RNN_BLOCK = 128
CONV_WIDTH = 4
RG_C = 8.0
N_SUBLAYERS = 3
EPS = 1e-6
ALIBI_SLOPES = tuple(2.0 ** (-8.0 * (h + 1) / N_HEADS) for h in range(N_HEADS))
ATTN_SCALE = HEAD_DIM ** -0.5
NEG = -1e30
LOG2E = 1.4426950408889634
SUBLANES = 8
VMEM_LIMIT = 56 * 1024 * 1024
PAGES_PER_STEP = 32
KV_BLOCKS_PER_STEP = 4
GATE_COST = 60
MXU_COLS = 256


def _params(n_axes):
    return pltpu.CompilerParams(dimension_semantics=("arbitrary",) * n_axes, vmem_limit_bytes=VMEM_LIMIT)


def _resident(shape):
    nd = len(shape)
    return pl.BlockSpec(shape, lambda *_: (0,) * nd, pipeline_mode=pl.Buffered(1))


def _dot(a, b):
    return jnp.dot(a, b, preferred_element_type=F32)


def _cast_chunks(weights, n_steps):
    return all(w.shape[0] % (n_steps * 2 * SUBLANES) == 0 for w in weights)


def _cast_spec(w, n_steps, step_of):
    rows = w.shape[0] // n_steps
    return pl.BlockSpec((rows, w.shape[1]), lambda *idx: (step_of(*idx), 0))


def _cast_rows(refs):
    n = len(refs) // 2
    for src, dst in zip(refs[:n], refs[n:]):
        dst[...] = src[...].astype(BF16)


def _dot_nt(a, b, precision=None):
    return lax.dot_general(a, b, (((1,), (1,)), ((), ())), precision=precision, preferred_element_type=F32)


def _dot_nt_split(a, b):
    a_hi = a.astype(BF16)
    b_hi = b.astype(BF16)
    a_lo = (a - a_hi.astype(F32)).astype(BF16)
    b_lo = (b - b_hi.astype(F32)).astype(BF16)
    return _dot_nt(a_hi, b_hi) + (_dot_nt(a_hi, b_lo) + _dot_nt(a_lo, b_hi))


def _rms(x, g):
    return x * lax.rsqrt(jnp.mean(x * x, axis=-1, keepdims=True) + EPS) * g


def _sigmoid(x):
    return 0.5 * jnp.tanh(0.5 * x) + 0.5


def _silu(x):
    return x * _sigmoid(x)


ADA_STEPS = 8


def _ada_kernel(c_ref, w_ref, b_ref, *rest, n_cast):
    cast_src, o_ref, cast_dst = rest[:n_cast], rest[n_cast], rest[n_cast + 1:]
    a = _silu(c_ref[...]).astype(BF16)
    o_ref[...] = _dot(a, w_ref[...].astype(BF16)) + b_ref[...]
    _cast_rows(cast_src + cast_dst)


def _ada(c, w, b, casts=()):
    m, d = c.shape
    n = w.shape[1]
    tn = n // ADA_STEPS
    cast_specs = [_cast_spec(x, ADA_STEPS, lambda i: i) for x in casts]
    out = pl.pallas_call(
        functools.partial(_ada_kernel, n_cast=len(casts)),
        grid=(ADA_STEPS,),
        in_specs=[pl.BlockSpec((m, d), lambda i: (0, 0)),
                  pl.BlockSpec((d, tn), lambda i: (0, i)),
                  pl.BlockSpec((1, tn), lambda i: (0, i))] + cast_specs,
        out_specs=[pl.BlockSpec((m, tn), lambda i: (0, i))] + cast_specs,
        out_shape=[jax.ShapeDtypeStruct((m, n), F32)] + [jax.ShapeDtypeStruct(x.shape, BF16) for x in casts],
        compiler_params=_params(1),
        name="ada",
    )(c, w, b.reshape(1, n), *casts)
    return out[0], tuple(out[1:])


def _swiglu_acc(h, wg_ref, wu_ref, wd_ref, chunk):
    d_ff = wg_ref.shape[1]
    acc = None
    for c in range(d_ff // chunk):
        sl = slice(c * chunk, (c + 1) * chunk)
        act = (_silu(_dot(h, wg_ref[:, sl])) * _dot(h, wu_ref[:, sl])).astype(BF16)
        part = _dot(act, wd_ref[sl, :])
        acc = part if acc is None else acc + part
    return acc


def _ffn_kernel(x_ref, sh_ref, sc_ref, g_ref, ng_ref, wg_ref, wu_ref, wd_ref, *rest, chunk, n_cast):
    cast_src, o_ref, cast_dst = rest[:n_cast], rest[n_cast], rest[n_cast + 1:]
    x = x_ref[...]
    h = (_rms(x, ng_ref[...]) * (1.0 + sc_ref[...]) + sh_ref[...]).astype(BF16)
    o_ref[...] = x + 0.5 * g_ref[...] * _swiglu_acc(h, wg_ref, wu_ref, wd_ref, chunk)
    _cast_rows(cast_src + cast_dst)


def _mod_spec(mod, tm, tiles_per_group):
    arr, chunk = mod
    r, d = arr.shape[1], arr.shape[2] // (3 * N_SUBLAYERS)
    return pl.BlockSpec((None, r, d), lambda i: (i // tiles_per_group, 0, chunk))


def _ffn_chunk(d_ff):
    return 256 if d_ff % 256 == 0 else 128


def _ffn(x, sh, sc, g, ng, wg, wu, wd, tm, tiles_per_group, casts=()):
    t, d = x.shape
    d_ff = wg.shape[1]
    n_steps = t // tm
    row = pl.BlockSpec((tm, d), lambda i: (i, 0))
    cast_specs = [_cast_spec(w, n_steps, lambda i: i) for w in casts]
    out = pl.pallas_call(
        functools.partial(_ffn_kernel, chunk=_ffn_chunk(d_ff), n_cast=len(casts)),
        grid=(n_steps,),
        in_specs=[row, _mod_spec(sh, tm, tiles_per_group), _mod_spec(sc, tm, tiles_per_group),
                  _mod_spec(g, tm, tiles_per_group), _resident((1, d)),
                  _resident(wg.shape), _resident(wu.shape), _resident(wd.shape)] + cast_specs,
        out_specs=[row] + cast_specs,
        out_shape=[jax.ShapeDtypeStruct((t, d), F32)] + [jax.ShapeDtypeStruct(w.shape, BF16) for w in casts],
        compiler_params=_params(1),
        name="ffn",
    )(x, sh[0], sc[0], g[0], ng, wg, wu, wd, *casts)
    return out[0], tuple(out[1:])


def _inproj_kernel(x_ref, sh_ref, sc_ref, ng_ref, w_ref, qg_ref, kg_ref,
                   xr_ref, ug_ref, q_ref, k_ref, v_ref, sga_ref, sgb_ref):
    d_rnn = xr_ref.shape[1]
    dq = q_ref.shape[1]
    dk = k_ref.shape[1]
    x = x_ref[...]
    h = (_rms(x, ng_ref[...]) * (1.0 + sc_ref[...]) + sh_ref[...]).astype(BF16)
    o = 0
    xr_ref[...] = _dot(h, w_ref[:, o:o + d_rnn])
    o += d_rnn
    ug_ref[...] = jax.nn.gelu(_dot(h, w_ref[:, o:o + d_rnn]))
    o += d_rnn
    for hd in range(dq // HEAD_DIM):
        sl = slice(hd * HEAD_DIM, (hd + 1) * HEAD_DIM)
        q_ref[:, sl] = _rms(_dot(h, w_ref[:, o + hd * HEAD_DIM:o + (hd + 1) * HEAD_DIM]), qg_ref[...])
    o += dq
    for hd in range(dk // HEAD_DIM):
        sl = slice(hd * HEAD_DIM, (hd + 1) * HEAD_DIM)
        k_ref[:, sl] = _rms(_dot(h, w_ref[:, o + hd * HEAD_DIM:o + (hd + 1) * HEAD_DIM]), kg_ref[...])
    o += dk
    v_ref[...] = _dot(h, w_ref[:, o:o + dk])
    o += dk
    d = sga_ref.shape[1]
    sga_ref[...] = _sigmoid(_dot(h, w_ref[:, o:o + d]))
    o += d
    sgb_ref[...] = _sigmoid(_dot(h, w_ref[:, o:o + d]))


def _inproj(x, sh, sc, ng, w_in, qg, kg, tm, tiles_per_group):
    t, d = x.shape
    dq, dk = N_HEADS * HEAD_DIM, N_KV_HEADS * HEAD_DIM
    d_rnn = (w_in.shape[1] - dq - 2 * dk - 2 * d) // 2

    def row(w):
        return pl.BlockSpec((tm, w), lambda i: (i, 0))

    widths = (d_rnn, d_rnn, dq, dk, dk, d, d)
    return pl.pallas_call(
        _inproj_kernel,
        grid=(t // tm,),
        in_specs=[row(d), _mod_spec(sh, tm, tiles_per_group), _mod_spec(sc, tm, tiles_per_group),
                  _resident((1, d)), _resident(w_in.shape), _resident((1, HEAD_DIM)), _resident((1, HEAD_DIM))],
        out_specs=[row(w) for w in widths],
        out_shape=[jax.ShapeDtypeStruct((t, w), F32) for w in widths],
        compiler_params=_params(1),
        name="inproj",
    )(x, sh[0], sc[0], ng, w_in, qg, kg)


def _rg_gate_block(xcn, n, wg_ref, bra_ref, brx_ref, sp, a_ref, b_ref):
    sl = slice(n * RNN_BLOCK, (n + 1) * RNN_BLOCK)
    z = _dot(xcn.astype(BF16), wg_ref[n])
    r = _sigmoid(z[:, :RNN_BLOCK] + bra_ref[:, sl])
    i = _sigmoid(z[:, RNN_BLOCK:] + brx_ref[:, sl])
    log_a = -RG_C * r * sp[:, sl]
    a = jnp.exp(log_a)
    a_ref[:, sl] = a
    b_ref[:, sl] = jnp.sqrt(1.0 - a * a) * (i * xcn)


def _rg_gates(xc, wg_ref, bra_ref, brx_ref, lam_ref, a_ref, b_ref):
    sp = jax.nn.softplus(-lam_ref[...])
    for n in range(N_RNN_BLOCKS):
        _rg_gate_block(xc[:, n * RNN_BLOCK:(n + 1) * RNN_BLOCK], n, wg_ref, bra_ref, brx_ref, sp, a_ref, b_ref)


def _scan_rows(a, b, row_in_seg, steps):
    for d in steps:
        keep = row_in_seg >= d
        a_prev = jnp.where(keep, pltpu.roll(a, d, 0), 1.0)
        b_prev = jnp.where(keep, pltpu.roll(b, d, 0), 0.0)
        b = a * b_prev + b
        a = a * a_prev
    return a, b


def _mixin_prompt_kernel(x_ref, sh_ref, sc_ref, ng_ref, w_ref, qg_ref, kg_ref, conv0_ref, h0_ref, cw_ref, cb_ref,
                         wg_ref, bra_ref, brx_ref, lam_ref, wba_ref,
                         q_ref, k_ref, v_ref, sgb_ref, ya_ref, k4_ref, v4_ref, nconv_ref, hlast_ref,
                         xbuf, a_s, b_s, hc_s, ug_s, sga_s, *, ts):
    s = pl.program_id(1)
    w = xbuf.shape[1]
    dq = q_ref.shape[1]
    dk = k_ref.shape[1]
    d = sgb_ref.shape[1]
    tail = CONV_WIDTH - 1
    lo = SUBLANES - tail

    @pl.when(s == 0)
    def _():
        xbuf[lo:SUBLANES, :] = conv0_ref[...]
        hc_s[...] = jnp.broadcast_to(h0_ref[...], hc_s.shape)

    x = x_ref[...]
    h = (_rms(x, ng_ref[...]) * (1.0 + sc_ref[...]) + sh_ref[...]).astype(BF16)
    o_gate, o_q = w, 2 * w
    o_k = o_q + dq
    o_v = o_k + dk
    o_ga = o_v + dk
    o_gb = o_ga + d
    cw_ = MXU_COLS

    def proj(lo):
        return _dot(h, w_ref[:, lo:lo + cw_])

    def normed_heads(ref, gain_ref, base, c):
        z = proj(base + c * cw_)
        for i in range(cw_ // HEAD_DIM):
            col = c * cw_ + i * HEAD_DIM
            ref[:, col:col + HEAD_DIM] = _rms(z[:, i * HEAD_DIM:(i + 1) * HEAD_DIM], gain_ref[...])

    def store(ref, base, c, fn):
        ref[:, c * cw_:(c + 1) * cw_] = fn(proj(base + c * cw_))

    jobs = [functools.partial(normed_heads, q_ref, qg_ref, o_q, c) for c in range(dq // cw_)]
    jobs += [functools.partial(normed_heads, k_ref, kg_ref, o_k, c) for c in range(dk // cw_)]
    jobs += [functools.partial(store, v_ref, o_v, c, lambda z: z) for c in range(dk // cw_)]
    jobs += [functools.partial(store, sgb_ref, o_gb, c, _sigmoid) for c in range(d // cw_)]
    jobs += [functools.partial(store, ug_s, o_gate, c, jax.nn.gelu) for c in range(w // cw_)]
    jobs += [functools.partial(store, sga_s, o_ga, c, _sigmoid) for c in range(d // cw_)]

    def by_head(src_ref, dst_ref, hd):
        dst_ref[:, hd, :] = src_ref[:, hd * HEAD_DIM:(hd + 1) * HEAD_DIM]
    cost = [GATE_COST] * N_RNN_BLOCKS + [1] * (ts // SUBLANES)
    due, acc_cost = [], 0
    for c_ in cost:
        acc_cost += c_
        due.append((acc_cost * len(jobs)) // sum(cost))
    done = 0

    for c in range(w // cw_):
        xbuf[SUBLANES:SUBLANES + ts, c * cw_:(c + 1) * cw_] = proj(c * cw_)
    sp = jax.nn.softplus(-lam_ref[...])
    for n in range(N_RNN_BLOCKS):
        sl = slice(n * RNN_BLOCK, (n + 1) * RNN_BLOCK)
        xcn = cb_ref[:, sl] + cw_ref[0:1, sl] * xbuf[lo:lo + ts, sl]
        for k in range(1, CONV_WIDTH):
            xcn = xcn + cw_ref[k:k + 1, sl] * xbuf[lo + k:lo + k + ts, sl]
        _rg_gate_block(xcn, n, wg_ref, bra_ref, brx_ref, sp, a_s, b_s)
        while done < due[n]:
            jobs[done]()
            done += 1
    new_tail = xbuf[ts + lo:ts + SUBLANES, :]
    nconv_ref[...] = new_tail
    xbuf[lo:SUBLANES, :] = new_tail

    row = lax.broadcasted_iota(jnp.int32, (SUBLANES, w), 0)
    hc = hc_s[...]
    for c in range(ts // SUBLANES):
        rows = slice(c * SUBLANES, (c + 1) * SUBLANES)
        a_cum, b_cum = _scan_rows(a_s[rows, :], b_s[rows, :], row, (1, 2, 4))
        hs = a_cum * hc + b_cum
        a_s[rows, :] = hs
        hc = jnp.broadcast_to(hs[SUBLANES - 1:SUBLANES, :], (SUBLANES, w))
        while done < due[N_RNN_BLOCKS + c]:
            jobs[done]()
            done += 1
    hc_s[...] = hc
    hlast_ref[...] = hc[0:1, :]
    u = (a_s[...] * ug_s[...]).astype(BF16)
    ya = _dot(u, wba_ref[...])
    for hd in range(dk // HEAD_DIM):
        by_head(k_ref, k4_ref, hd)
        by_head(v_ref, v4_ref, hd)
    ya_ref[...] = sga_s[...] * ya


def _mixin_prompt(x, sh, sc, ng, w_in, qg, kg, conv0, h0, cw, cb, wg, bra, brx, lam, wba, ts):
    b, s, d = x.shape
    w = wba.shape[0]
    dq, dk = N_HEADS * HEAD_DIM, N_KV_HEADS * HEAD_DIM
    tail = CONV_WIDTH - 1

    def seq(width):
        return pl.BlockSpec((None, ts, width), lambda i, j: (i, j, 0))

    def per_seq(rows, width):
        return pl.BlockSpec((None, rows, width), lambda i, j: (i, 0, 0))

    def mod_row(mod):
        return pl.BlockSpec((None, 1, d), lambda i, j: (i, 0, mod[1]))

    widths = (dq, dk, dk, d, d)
    by_head = pl.BlockSpec((None, ts, N_KV_HEADS, HEAD_DIM), lambda i, j: (i, j, 0, 0))
    return pl.pallas_call(
        functools.partial(_mixin_prompt_kernel, ts=ts),
        grid=(b, s // ts),
        in_specs=[seq(d), mod_row(sh), mod_row(sc), _resident((1, d)), _resident(w_in.shape),
                  _resident((1, HEAD_DIM)), _resident((1, HEAD_DIM)), per_seq(tail, w), per_seq(1, w),
                  _resident(cw.shape), _resident((1, w)), _resident(wg.shape), _resident((1, w)), _resident((1, w)),
                  _resident((1, w)), _resident(wba.shape)],
        out_specs=[seq(wd) for wd in widths] + [by_head, by_head, per_seq(tail, w), per_seq(1, w)],
        out_shape=[jax.ShapeDtypeStruct((b, s, wd), F32) for wd in widths]
        + [jax.ShapeDtypeStruct((b, s, N_KV_HEADS, HEAD_DIM), F32)] * 2
        + [jax.ShapeDtypeStruct((b, tail, w), F32), jax.ShapeDtypeStruct((b, 1, w), F32)],
        scratch_shapes=[pltpu.VMEM((ts + SUBLANES, w), F32), pltpu.VMEM((ts, w), F32), pltpu.VMEM((ts, w), F32),
                        pltpu.VMEM((SUBLANES, w), F32), pltpu.VMEM((ts, w), F32), pltpu.VMEM((ts, d), F32)],
        compiler_params=_params(2),
        name="mixin_prompt",
    )(x, sh[0], sc[0], ng, w_in, qg, kg, conv0, h0, cw, cb, wg, bra, brx, lam, wba)


def _rnn_sample_kernel(xs_ref, ug_ref, sga_ref, h0_ref, cw_ref, cb_ref, wg_ref, bra_ref, brx_ref, lam_ref, wba_ref,
                       ya_ref, h_ref, a_s, b_s, *, seg):
    m, w = ug_ref.shape
    xc = cb_ref[...] + cw_ref[0:1, :] * xs_ref[0]
    for k in range(1, CONV_WIDTH):
        xc = xc + cw_ref[k:k + 1, :] * xs_ref[k]
    _rg_gates(xc, wg_ref, bra_ref, brx_ref, lam_ref, a_s, b_s)
    row_in_seg = lax.broadcasted_iota(jnp.int32, (SUBLANES, w), 0) % seg
    steps = tuple(d for d in (1, 2, 4) if d < seg)
    for c in range(m // SUBLANES):
        rows = slice(c * SUBLANES, (c + 1) * SUBLANES)
        a_cum, b_cum = _scan_rows(a_s[rows, :], b_s[rows, :], row_in_seg, steps)
        h_ref[rows, :] = a_cum * h0_ref[rows, :] + b_cum
    u = (h_ref[...] * ug_ref[...]).astype(BF16)
    ya_ref[...] = sga_ref[...] * _dot(u, wba_ref[...])


def _rnn_sample(xs, ug, sga, h0_rows, cw, cb, wg, bra, brx, lam, wba, seg):
    m, w = ug.shape
    d = wba.shape[1]
    return pl.pallas_call(
        functools.partial(_rnn_sample_kernel, seg=seg),
        out_shape=[jax.ShapeDtypeStruct((m, d), F32), jax.ShapeDtypeStruct((m, w), F32)],
        scratch_shapes=[pltpu.VMEM((m, w), F32), pltpu.VMEM((m, w), F32)],
        compiler_params=pltpu.CompilerParams(vmem_limit_bytes=VMEM_LIMIT),
        name="rnn_sample",
    )(xs, ug, sga, h0_rows, cw, cb, wg, bra, brx, lam, wba)


V_ROWS = HEAD_DIM + 16


def _attn_prompt_kernel(slope_ref, q_ref, k_ref, v_ref, *rest, nb, gt, n_cast):
    cast_src, o_ref, cast_dst = rest[:n_cast], rest[n_cast], rest[n_cast + 1:2 * n_cast + 1]
    kb_s, vt_s, kbg_s, vtg_s, km_s, q2_s, bias_s, colb_s, acc_s = rest[2 * n_cast + 1:]
    g = pl.program_id(1)
    blk = MOBA_BLOCK
    s_len = nb * blk
    q_scale = ATTN_SCALE * LOG2E
    _cast_rows(cast_src + cast_dst)

    ones_rows = (lax.broadcasted_iota(jnp.int32, (V_ROWS - HEAD_DIM, blk), 0) == 0).astype(BF16)
    for n in range(nb):
        rows = slice(n * blk, (n + 1) * blk)
        grp, sub = n // gt, slice((n % gt) * blk, (n % gt + 1) * blk)
        kn = k_ref[rows, :]
        kb = kn.astype(BF16)
        vt = v_ref[rows, :].T.astype(BF16)
        kb_s[n] = kb
        kbg_s[grp, sub, :] = kb
        vt_s[n, 0:HEAD_DIM, :] = vt
        vt_s[n, HEAD_DIM:V_ROWS, :] = ones_rows
        vtg_s[grp, 0:HEAD_DIM, sub] = vt
        vtg_s[grp, HEAD_DIM:V_ROWS, sub] = ones_rows
        km_s[n:n + 1, :] = jnp.sum(kn, axis=0, keepdims=True) * (1.0 / blk)
        q2_s[n] = (q_ref[rows, :] * q_scale).astype(BF16)

    blk_id = lax.broadcasted_iota(jnp.int32, (nb, s_len), 0)
    q_blk = lax.broadcasted_iota(jnp.int32, (nb, s_len), 1) // blk
    fully_past = blk_id < q_blk
    blocks_ahead = ((blk_id - q_blk) * blk).astype(F32)
    key_off = lax.broadcasted_iota(jnp.int32, (blk, blk), 0)
    causal = key_off <= lax.broadcasted_iota(jnp.int32, (blk, blk), 1)
    km = km_s[...]
    for hh in range(GROUP):
        slope2 = slope_ref[g * GROUP + hh] * LOG2E
        cols = slice(hh * HEAD_DIM, (hh + 1) * HEAD_DIM)
        gs = jnp.where(fully_past, _dot_nt_split(km, q_ref[:, cols]), -jnp.inf)
        rank = jnp.zeros((nb, s_len), jnp.int32)
        for m in range(nb):
            gm = gs[m:m + 1, :]
            tie = (blk_id > m).astype(jnp.int32)
            rank = rank + jnp.where(gm > gs, 1, jnp.where(gm == gs, tie, 0))
        chosen = jnp.logical_and(fully_past, rank < MOBA_TOPK)
        bias = jnp.where(chosen, slope2 * blocks_ahead, NEG)
        for jq in range(nb):
            bias_s[hh, jq] = bias[:, jq * blk:(jq + 1) * blk]
        colb_s[hh] = slope2 * key_off.astype(F32)

    heads = [slice(hh * HEAD_DIM, (hh + 1) * HEAD_DIM) for hh in range(GROUP)]

    def group_scores(gi, j, q2):
        kg = kbg_s[gi]
        s_grp = [_dot_nt(kg, q2[:, cols]) for cols in heads]
        parts, tops = [], []
        for hh in range(GROUP):
            ps = [s_grp[hh][t * blk:(t + 1) * blk, :] + colb_s[hh] + bias_s[hh, j, pl.ds(gi * gt + t, 1), :]
                  for t in range(gt)]
            top = jnp.max(ps[0], axis=0, keepdims=True)
            for part in ps[1:]:
                top = jnp.maximum(top, jnp.max(part, axis=0, keepdims=True))
            parts.append(ps)
            tops.append(top)
        return parts, tops

    def write_out(jb):
        rows = pl.ds(pl.multiple_of(jb * blk, blk), blk)
        for hh in range(GROUP):
            acc = acc_s[hh]
            o = acc[0:HEAD_DIM, :] / acc[HEAD_DIM:HEAD_DIM + 1, :]
            o_ref[rows, heads[hh]] = o.T.astype(o_ref.dtype)

    acc_s[...] = jnp.ones_like(acc_s)

    def q_block(j, _):
        q2 = q2_s[j]
        kd = kb_s[j]
        vd = vt_s[j]
        s_own = [_dot_nt(kd, q2[:, cols]) for cols in heads]
        parts0, tops0 = group_scores(0, j, q2)
        write_out(jnp.maximum(j - 1, 0))
        m_run, probs = [], []
        for hh in range(GROUP):
            s = jnp.where(causal, s_own[hh] + colb_s[hh], NEG)
            m1 = jnp.maximum(jnp.max(s, axis=0, keepdims=True), tops0[hh])
            p_own = jnp.exp2(s - m1).astype(BF16)
            p_grp = jnp.concatenate([jnp.exp2(part - m1).astype(BF16) for part in parts0[hh]], axis=0)
            probs.append((p_own, p_grp))
            m_run.append(m1)
        vg0 = vtg_s[0]
        for hh in range(GROUP):
            acc_s[hh] = _dot(vd, probs[hh][0]) + _dot(vg0, probs[hh][1])

        def kv_group(gi, m_run):
            vg = vtg_s[gi]
            parts, tops = group_scores(gi, j, q2)
            out, probs = [], []
            for hh in range(GROUP):
                m_new = jnp.maximum(m_run[hh], tops[hh])
                probs.append(jnp.concatenate([jnp.exp2(part - m_new).astype(BF16) for part in parts[hh]], axis=0))
                out.append(m_new)
            pv = [_dot(vg, p) for p in probs]
            for hh in range(GROUP):
                acc_s[hh] = jnp.exp2(m_run[hh] - out[hh]) * acc_s[hh] + pv[hh]
            return tuple(out)

        lax.fori_loop(1, (j + gt - 1) // gt, kv_group, tuple(m_run))
        return 0

    lax.fori_loop(0, nb, q_block, 0)
    write_out(nb - 1)


def _attn_prompt(q, k, v, casts=()):
    b, s, _ = q.shape
    assert s % MOBA_BLOCK == 0
    nb = s // MOBA_BLOCK
    gt = KV_BLOCKS_PER_STEP if nb % KV_BLOCKS_PER_STEP == 0 else 1
    gw = GROUP * HEAD_DIM
    slopes = jnp.asarray(ALIBI_SLOPES, F32)
    kv_spec = pl.BlockSpec((None, s, HEAD_DIM), lambda i, g: (i, 0, g))
    q_spec = pl.BlockSpec((None, s, gw), lambda i, g: (i, 0, g))
    cast_specs = [_cast_spec(w, b * N_KV_HEADS, lambda i, g: i * N_KV_HEADS + g) for w in casts]
    out = pl.pallas_call(
        functools.partial(_attn_prompt_kernel, nb=nb, gt=gt, n_cast=len(casts)),
        grid=(b, N_KV_HEADS),
        in_specs=[pl.BlockSpec(memory_space=pltpu.SMEM), q_spec, kv_spec, kv_spec] + cast_specs,
        out_specs=[q_spec] + cast_specs,
        out_shape=[jax.ShapeDtypeStruct(q.shape, BF16)] + [jax.ShapeDtypeStruct(w.shape, BF16) for w in casts],
        scratch_shapes=[pltpu.VMEM((nb, MOBA_BLOCK, HEAD_DIM), BF16), pltpu.VMEM((nb, V_ROWS, MOBA_BLOCK), BF16),
                        pltpu.VMEM((nb // gt, gt * MOBA_BLOCK, HEAD_DIM), BF16),
                        pltpu.VMEM((nb // gt, V_ROWS, gt * MOBA_BLOCK), BF16),
                        pltpu.VMEM((nb, HEAD_DIM), F32), pltpu.VMEM((nb, MOBA_BLOCK, gw), BF16),
                        pltpu.VMEM((GROUP, nb, nb, MOBA_BLOCK), F32), pltpu.VMEM((GROUP, MOBA_BLOCK, MOBA_BLOCK), F32),
                        pltpu.VMEM((GROUP, V_ROWS, MOBA_BLOCK), F32)],
        compiler_params=_params(2),
        name="attn_prompt",
    )(slopes, q, k, v, *casts)
    return out[0], tuple(out[1:])


def _row_slopes(head):
    out = jnp.zeros(head.shape, F32)
    for h, sl in enumerate(ALIBI_SLOPES):
        out = jnp.where(head == h, sl, out)
    return out


def _attn_sample_kernel(pt_ref, q_ref, kn_ref, vn_ref, *rest, pp, n_blk, past, n_new):
    k_pages, v_pages = rest[:pp], rest[pp:2 * pp]
    o_ref = rest[2 * pp]
    km_s, m_s, l_s, op_s, bias_s = rest[2 * pp + 1:]
    del pt_ref
    s_id = pl.program_id(1)
    nq = q_ref.shape[0]
    kv = N_KV_HEADS
    cols = MOBA_BLOCK * kv
    q = q_ref[...]
    qb = (q * (ATTN_SCALE * LOG2E)).astype(BF16)

    row = lax.broadcasted_iota(jnp.int32, (nq, cols), 0)
    col = lax.broadcasted_iota(jnp.int32, (nq, cols), 1)
    head = row % N_HEADS
    step = row // N_HEADS

    @pl.when(s_id == 0)
    def _():
        same_kv = (col % kv) == head // GROUP
        back = (step + MOBA_BLOCK - col // kv).astype(F32)
        bias_s[...] = jnp.where(same_kv, -(_row_slopes(head) * LOG2E) * back, NEG)
        m_s[...] = jnp.zeros_like(m_s)
        l_s[...] = jnp.zeros_like(l_s)

    slope_col = _row_slopes(head[:, 0:1]) * LOG2E
    lane = lax.broadcasted_iota(jnp.int32, (nq, 128), 1)
    per_step = pp // 2
    first_blk = s_id * per_step
    ksums, scores, probs, partials = [], [], [], []
    for i in range(per_step):
        k0, k1 = k_pages[2 * i][...], k_pages[2 * i + 1][...]
        ksums.append(jnp.sum(k0.reshape(-1, SUBLANES, HEAD_DIM), axis=0)
                     + jnp.sum(k1.reshape(-1, SUBLANES, HEAD_DIM), axis=0))
        kb = jnp.concatenate([k0, k1], axis=0).astype(BF16)
        scores.append(_dot_nt(qb, kb) + bias_s[...])
    for s in scores:
        m = jnp.max(s, axis=-1, keepdims=True)
        p = jnp.exp2(s - m)
        probs.append((m, jnp.sum(p, axis=-1, keepdims=True), p.astype(BF16)))
    for i, (m, l, p) in enumerate(probs):
        vb = jnp.concatenate([v_pages[2 * i][...], v_pages[2 * i + 1][...]], axis=0).astype(BF16)
        partials.append((m, l, _dot(p, vb)))
    m_new, l_new = m_s[...], l_s[...]
    for i, (m, l, o_part) in enumerate(partials):
        n = first_blk + i
        km_s[pl.ds(pl.multiple_of(n * SUBLANES, SUBLANES), SUBLANES), :] = ksums[i] * (1.0 / MOBA_BLOCK)
        op_s[n] = o_part
        m_new = jnp.where(lane == n, m + slope_col * ((n + 1) * MOBA_BLOCK - past).astype(F32), m_new)
        l_new = jnp.where(lane == n, l, l_new)
    m_s[...] = m_new
    l_s[...] = l_new

    @pl.when(s_id == pl.num_programs(1) - 1)
    def _():
        gcols = n_blk * SUBLANES
        gfull = _dot_nt(q, km_s[...], precision=lax.Precision.HIGHEST)
        gsum = gfull + pltpu.roll(gfull, gcols - kv, 1)
        grow = lax.broadcasted_iota(jnp.int32, (nq, gcols), 0)
        gcol = lax.broadcasted_iota(jnp.int32, (nq, gcols), 1)
        gs = jnp.where((gcol % SUBLANES) == (grow % N_HEADS) // GROUP, gsum, -jnp.inf)
        gcol_f = gcol.astype(F32)
        chosen = jnp.zeros((nq, 128), F32)
        for _ in range(MOBA_TOPK):
            best = jnp.max(gs, axis=-1, keepdims=True)
            first = jnp.min(jnp.where(gs == best, gcol_f, float(gcols)), axis=-1, keepdims=True)
            chosen = jnp.where(lane == (first.astype(jnp.int32) // SUBLANES), 1.0, chosen)
            gs = jnp.where(gcol_f == first, -jnp.inf, gs)
        is_chosen = chosen > 0.0

        nk = n_new * kv
        orow = lax.broadcasted_iota(jnp.int32, (nq, nk), 0)
        ocol = lax.broadcasted_iota(jnp.int32, (nq, nk), 1)
        ohead, ostep = orow % N_HEADS, orow // N_HEADS
        ok = jnp.logical_and((ocol % kv) == ohead // GROUP, ocol // kv <= ostep)
        so = _dot_nt(qb, kn_ref[...].astype(BF16)) - (_row_slopes(ohead) * LOG2E) * (ostep - ocol // kv).astype(F32)
        so = jnp.where(ok, so, NEG)

        m_all = jnp.maximum(jnp.max(so, axis=-1, keepdims=True),
                            jnp.max(jnp.where(is_chosen, m_s[...], NEG), axis=-1, keepdims=True))
        wgt = jnp.where(is_chosen, jnp.exp2(m_s[...] - m_all), 0.0)
        po = jnp.exp2(so - m_all)
        l_all = jnp.sum(wgt * l_s[...], axis=-1, keepdims=True) + jnp.sum(po, axis=-1, keepdims=True)
        acc = _dot(po.astype(BF16), vn_ref[...].astype(BF16))
        for n in range(n_blk):
            acc = acc + wgt[:, n:n + 1] * op_s[n]
        o_ref[...] = acc / l_all


def _attn_sample(q, k_new, v_new, cache_k, cache_v, page_ids, n_new):
    b, nq, _ = q.shape
    n_pages = page_ids.shape[1]
    past = n_pages * PAGE_SIZE
    pp = PAGES_PER_STEP
    assert MOBA_BLOCK == 2 * PAGE_SIZE and past % MOBA_BLOCK == 0 and n_pages % pp == 0
    n_blk = past // MOBA_BLOCK
    assert MOBA_TOPK <= n_blk <= 128 and n_new <= MOBA_BLOCK
    rows = PAGE_SIZE * N_KV_HEADS

    def page(i):
        return pl.BlockSpec((None, rows, HEAD_DIM), lambda bi, si, pt: (pt[bi * n_pages + si * pp + i], 0, 0))

    def per_seq(r):
        return pl.BlockSpec((None, r, HEAD_DIM), lambda bi, si, pt: (bi, 0, 0))

    return pl.pallas_call(
        functools.partial(_attn_sample_kernel, pp=pp, n_blk=n_blk, past=past, n_new=n_new),
        grid_spec=pltpu.PrefetchScalarGridSpec(
            num_scalar_prefetch=1,
            grid=(b, n_pages // pp),
            in_specs=[per_seq(nq), per_seq(n_new * N_KV_HEADS), per_seq(n_new * N_KV_HEADS)]
            + [page(i) for i in range(pp)] * 2,
            out_specs=per_seq(nq),
            scratch_shapes=[pltpu.VMEM((n_blk * SUBLANES, HEAD_DIM), F32), pltpu.VMEM((nq, 128), F32),
                            pltpu.VMEM((nq, 128), F32), pltpu.VMEM((n_blk, nq, HEAD_DIM), F32),
                            pltpu.VMEM((nq, MOBA_BLOCK * N_KV_HEADS), F32)]),
        out_shape=jax.ShapeDtypeStruct((b, nq, HEAD_DIM), F32),
        compiler_params=_params(2),
        name="attn_sample",
    )(page_ids.reshape(-1), q, k_new, v_new, *([cache_k] * pp), *([cache_v] * pp))


def _merge_ffn_kernel(x_ref, ya_ref, o_ref_in, sgb_ref, g2_ref, sh_ref, sc_ref, g3_ref, ng_ref,
                      wbb_ref, wout_ref, wg_ref, wu_ref, wd_ref, out_ref, *, chunk):
    yb = _dot(o_ref_in[...].astype(BF16), wbb_ref[...])
    merged = (ya_ref[...] + sgb_ref[...] * yb).astype(BF16)
    x = x_ref[...] + g2_ref[...] * _dot(merged, wout_ref[...])
    h = (_rms(x, ng_ref[...]) * (1.0 + sc_ref[...]) + sh_ref[...]).astype(BF16)
    out_ref[...] = x + 0.5 * g3_ref[...] * _swiglu_acc(h, wg_ref, wu_ref, wd_ref, chunk)


def _merge_ffn(x, ya, o, sgb, g2, sh, sc, g3, ng, wbb, wout, wg, wu, wd, tm, tiles_per_group):
    t, d = x.shape
    row = pl.BlockSpec((tm, d), lambda i: (i, 0))
    mod = functools.partial(_mod_spec, tm=tm, tiles_per_group=tiles_per_group)
    return pl.pallas_call(
        functools.partial(_merge_ffn_kernel, chunk=_ffn_chunk(wg.shape[1])),
        grid=(t // tm,),
        in_specs=[row, row, pl.BlockSpec((tm, o.shape[1]), lambda i: (i, 0)), row,
                  mod(g2), mod(sh), mod(sc), mod(g3), _resident((1, d)),
                  _resident(wbb.shape), _resident(wout.shape), _resident(wg.shape), _resident(wu.shape),
                  _resident(wd.shape)],
        out_specs=row,
        out_shape=jax.ShapeDtypeStruct((t, d), F32),
        compiler_params=_params(1),
        name="merge_ffn",
    )(x, ya, o, sgb, g2[0], sh[0], sc[0], g3[0], ng, wbb, wout, wg, wu, wd)


def _token_tile(t):
    return 512 if t % 512 == 0 else t


EARLY_BF16 = ("ffn1_gate", "ffn1_up", "ffn1_down")
MIXIN_BF16 = ("w_in", "w_branch_a")
MERGE_BF16 = ("w_branch_b", "w_out", "ffn2_gate", "ffn2_up", "ffn2_down")


def _layer_weights(p):
    w = dict(p)
    w["rg_w"] = jnp.concatenate([p["rg_wa"], p["rg_wx"]], axis=-1).astype(BF16)
    for name in ("norm1_g", "norm2_g", "norm3_g", "conv_b", "rg_ba", "rg_bx", "rg_lambda", "q_norm_g", "k_norm_g"):
        w[name] = p[name].reshape(1, -1)
    return w


def _hidden_casts(w, names, n_steps):
    if _cast_chunks([w[n] for n in names], n_steps):
        return tuple(w[n] for n in names)
    for n in names:
        w[n] = w[n].astype(BF16)
    return ()


def _prompt_layer(x, mod, w):
    b, s, d = x.shape
    t = b * s
    tm = _token_tile(s)
    tpg = s // tm
    sh1, sc1, g1, sh2, sc2, g2, sh3, sc3, g3 = [(mod[:, None, :], c) for c in range(3 * N_SUBLAYERS)]
    casts = _hidden_casts(w, MIXIN_BF16, t // tm)
    x1, done = _ffn(x.reshape(t, d), sh1, sc1, g1, w["norm1_g"], w["ffn1_gate"], w["ffn1_up"], w["ffn1_down"],
                    tm, tpg, casts)
    w.update(zip(MIXIN_BF16, done))
    d_rnn = w["w_branch_a"].shape[0]
    q, k, v, sgb, ya, k_heads, v_heads, new_conv, h_last = _mixin_prompt(
        x1.reshape(b, s, d), sh2, sc2, w["norm2_g"], w["w_in"], w["q_norm_g"], w["k_norm_g"],
        jnp.zeros((b, CONV_WIDTH - 1, d_rnn), F32), jnp.zeros((b, 1, d_rnn), F32),
        w["conv_w"], w["conv_b"], w["rg_w"], w["rg_ba"], w["rg_bx"], w["rg_lambda"], w["w_branch_a"], tm)
    sgb = sgb.reshape(t, d)
    casts = _hidden_casts(w, MERGE_BF16, b * N_KV_HEADS)
    o, done = _attn_prompt(q.reshape(b, s, -1), k.reshape(b, s, -1), v.reshape(b, s, -1), casts)
    w.update(zip(MERGE_BF16, done))
    y = _merge_ffn(x1, ya.reshape(t, d), o.reshape(t, -1), sgb, g2, sh3, sc3, g3, w["norm3_g"],
                   w["w_branch_b"], w["w_out"], w["ffn2_gate"], w["ffn2_up"], w["ffn2_down"], tm, tpg)
    return y.reshape(b, s, d), k_heads, v_heads, new_conv, h_last.reshape(b, d_rnn)


def _sample_layer(x, mod, conv_buf, h0, cache_k, cache_v, page_ids, w):
    b, s, d = x.shape
    t = b * s
    per_token = jnp.repeat(mod, s, axis=0)[None]
    sh1, sc1, g1, sh2, sc2, g2, sh3, sc3, g3 = [(per_token, c) for c in range(3 * N_SUBLAYERS)]
    x1, _ = _ffn(x.reshape(t, d), sh1, sc1, g1, w["norm1_g"], w["ffn1_gate"], w["ffn1_up"], w["ffn1_down"], t, 1)
    xr, ug, q, k, v, sga, sgb = _inproj(x1, sh2, sc2, w["norm2_g"], w["w_in"], w["q_norm_g"], w["k_norm_g"], t, 1)
    d_rnn = xr.shape[1]
    xp = jnp.concatenate([conv_buf, xr.reshape(b, s, d_rnn)], axis=1)
    taps = jnp.stack([xp[:, kk:kk + s].reshape(t, d_rnn) for kk in range(CONV_WIDTH)])
    ya, h_all = _rnn_sample(taps, ug, sga, jnp.repeat(h0, s, axis=0), w["conv_w"], w["conv_b"], w["rg_w"],
                            w["rg_ba"], w["rg_bx"], w["rg_lambda"], w["w_branch_a"], s)
    o = _attn_sample(q.reshape(b, s * N_HEADS, HEAD_DIM), k.reshape(b, s * N_KV_HEADS, HEAD_DIM),
                     v.reshape(b, s * N_KV_HEADS, HEAD_DIM), cache_k, cache_v, page_ids, s)
    y = _merge_ffn(x1, ya, o.reshape(t, -1), sgb, g2, sh3, sc3, g3, w["norm3_g"],
                   w["w_branch_b"], w["w_out"], w["ffn2_gate"], w["ffn2_up"], w["ffn2_down"], t, 1)
    return (y.reshape(b, s, d), k.reshape(b, s, N_KV_HEADS, HEAD_DIM), v.reshape(b, s, N_KV_HEADS, HEAD_DIM),
            xp[:, -(CONV_WIDTH - 1):], h_all.reshape(b, s, d_rnn)[:, -1])


def kernel(x_prompt, x_sample, c_prompt, c_sample, cache_k, cache_v, state_conv, state_rglru, page_table, w_ada, b_ada, norm1_g, ffn1_gate, ffn1_up, ffn1_down, norm2_g, w_in, conv_w, conv_b, rg_wa, rg_ba, rg_wx, rg_bx, rg_lambda, q_norm_g, k_norm_g, w_branch_a, w_branch_b, w_out, norm3_g, ffn2_gate, ffn2_up, ffn2_down):
    depth, n_pool = cache_k.shape[0], cache_k.shape[1]
    bp = x_prompt.shape[0]
    assert cache_k.shape[2:] == (PAGE_SIZE, N_KV_HEADS, HEAD_DIM)
    ck = cache_k.reshape(depth * n_pool, PAGE_SIZE * N_KV_HEADS, HEAD_DIM)
    cv = cache_v.reshape(depth * n_pool, PAGE_SIZE * N_KV_HEADS, HEAD_DIM)
    c_all = jnp.concatenate([c_prompt, c_sample], axis=0)
    pad = -c_all.shape[0] % SUBLANES
    c_all = jnp.pad(c_all, ((0, pad), (0, 0)))
    yp, ys = x_prompt, x_sample
    outs = [[] for _ in range(8)]
    for l in range(depth):
        p = dict(norm1_g=norm1_g[l], ffn1_gate=ffn1_gate[l], ffn1_up=ffn1_up[l], ffn1_down=ffn1_down[l],
                 norm2_g=norm2_g[l], w_in=w_in[l], conv_w=conv_w[l], conv_b=conv_b[l], rg_wa=rg_wa[l],
                 rg_ba=rg_ba[l], rg_wx=rg_wx[l], rg_bx=rg_bx[l], rg_lambda=rg_lambda[l], q_norm_g=q_norm_g[l],
                 k_norm_g=k_norm_g[l], w_branch_a=w_branch_a[l], w_branch_b=w_branch_b[l], w_out=w_out[l],
                 norm3_g=norm3_g[l], ffn2_gate=ffn2_gate[l], ffn2_up=ffn2_up[l], ffn2_down=ffn2_down[l])
        w = _layer_weights(p)
        mod, done = _ada(c_all, w_ada[l], b_ada[l], _hidden_casts(w, EARLY_BF16, ADA_STEPS))
        w.update(zip(EARLY_BF16, done))
        yp, kp, vp, cp, hp = _prompt_layer(yp, mod[:bp], w)
        ys, ks, vs, cs, hs = _sample_layer(ys, mod[bp:bp + x_sample.shape[0]], state_conv[l], state_rglru[l],
                                           ck, cv, page_table + l * n_pool, w)
        for lst, val in zip(outs, (kp, vp, cp, hp, ks, vs, cs, hs)):
            lst.append(val)
    return (yp, ys) + tuple(jnp.stack(o) for o in outs)
```
